```python
import math
import jax, jax.numpy as jnp
from jax import lax
import numpy as np

D_MODEL = 1024
BATCH = 8
SEQ = 2048
DEPTH = 2
DEC_BATCH = 128
DEC_SEQ = 8
PAST_LEN = 16384
PAGE_SIZE = 128

D_MIX = D_MODEL
D_HALF = D_MIX // 2
LRU_HEADS = 8
LRU_HEAD_DIM = D_HALF // LRU_HEADS
LRU_CONV = 4
RG_C = 8.0
SC_CONV = 3
FFN_CONV = 3
D_FF = 2816
N_IN = 5 * D_HALF
EPS = 1e-6

kernel_name = "hybrid_rglru_shortconv_convffn_step"


def rmsnorm(x, g):
    xf = x.astype(jnp.float32)
    y = xf * lax.rsqrt(jnp.mean(xf * xf, axis=-1, keepdims=True) + EPS)
    return (y * g.astype(jnp.float32)).astype(x.dtype)


def causal_conv(x, buf, w):
    K = w.shape[0]
    T = x.shape[1]
    xp = jnp.concatenate([buf.astype(x.dtype), x], axis=1)
    y = xp[:, 0:T] * w[0]
    for k in range(1, K):
        y = y + xp[:, k:k + T] * w[k]
    return y, xp[:, T:]


def block_diag(x, w, b):
    B_, T_, _ = x.shape
    xh = x.reshape(B_, T_, LRU_HEADS, LRU_HEAD_DIM)
    y = jnp.einsum('bthi,hij->bthj', xh, w).reshape(B_, T_, D_HALF)
    return y + b


def rglru(x, r, i, lam, h0):
    xf = x.astype(jnp.float32)
    rf = r.astype(jnp.float32)
    if_ = i.astype(jnp.float32)
    log_a = -RG_C * rf * jax.nn.softplus(-lam.astype(jnp.float32))
    a = jnp.exp(log_a)
    mult = jnp.sqrt(-jnp.expm1(2.0 * log_a))
    b = mult * (if_ * xf)
    b = b.at[:, 0].add(a[:, 0] * h0.astype(jnp.float32))

    def comb(left, right):
        a1, b1 = left
        a2, b2 = right
        return a1 * a2, a2 * b1 + b2

    _, h = lax.associative_scan(comb, (a, b), axis=1)
    return h.astype(x.dtype), h[:, -1].astype(h0.dtype)


def layer(x, c, st_h, st_lc, st_sc, st_fc,
          w_ada, b_ada, norm1_g, norm2_g, w_in, lru_conv_w, lru_conv_b,
          lru_wa, lru_ba, lru_wx, lru_bx, lru_lambda, sc_conv_w, w_out,
          ffn_w_up, ffn_conv_w, ffn_w_down):
    mod = (jax.nn.silu(c) @ w_ada + b_ada)[:, None, :]
    sh1, sc1, g1, sh2, sc2, g2 = jnp.split(mod, 6, axis=-1)

    hn = rmsnorm(x, norm1_g) * (1.0 + sc1) + sh1
    proj = hn @ w_in
    xl, gl, bs, cs, hs = jnp.split(proj, 5, axis=-1)
    xc, new_lc = causal_conv(xl, st_lc, lru_conv_w)
    xc = xc + lru_conv_b
    r = jax.nn.sigmoid(block_diag(xc, lru_wa, lru_ba))
    ig = jax.nn.sigmoid(block_diag(xc, lru_wx, lru_bx))
    h_seq, h_last = rglru(xc, r, ig, lru_lambda, st_h)
    lru_out = h_seq * jax.nn.gelu(gl)
    zc, new_sc = causal_conv(cs * hs, st_sc, sc_conv_w)
    sc_out = bs * zc
    mix = jnp.concatenate([lru_out, sc_out], axis=-1) @ w_out
    x = x + g1 * mix

    hn2 = rmsnorm(x, norm2_g) * (1.0 + sc2) + sh2
    u, v = jnp.split(hn2 @ ffn_w_up, 2, axis=-1)
    uc, new_fc = causal_conv(u, st_fc, ffn_conv_w)
    x = x + g2 * ((jax.nn.gelu(uc) * v) @ ffn_w_down)
    return x, h_last, new_lc, new_sc, new_fc


def run_trunk(x, c, st_h, st_lc, st_sc, st_fc, w_ada, b_ada, norm1_g, norm2_g,
              w_in, lru_conv_w, lru_conv_b, lru_wa, lru_ba, lru_wx, lru_bx,
              lru_lambda, sc_conv_w, w_out, ffn_w_up, ffn_conv_w, ffn_w_down,
              final_g):
    hs, lcs, scs, fcs = [], [], [], []
    for l in range(DEPTH):
        x, h, lc, sc, fc = layer(
            x, c, st_h[l], st_lc[l], st_sc[l], st_fc[l],
            w_ada[l], b_ada[l], norm1_g[l], norm2_g[l], w_in[l], lru_conv_w[l],
            lru_conv_b[l], lru_wa[l], lru_ba[l], lru_wx[l], lru_bx[l],
            lru_lambda[l], sc_conv_w[l], w_out[l], ffn_w_up[l], ffn_conv_w[l],
            ffn_w_down[l])
        hs.append(h); lcs.append(lc); scs.append(sc); fcs.append(fc)
    y = rmsnorm(x, final_g)
    return y, jnp.stack(hs), jnp.stack(lcs), jnp.stack(scs), jnp.stack(fcs)


def setup_inputs(seed: int = 0) -> dict:
    key = jax.random.key(seed)
    ks = iter(jax.random.split(key, 40))
    nrm = lambda shape, s: jax.random.normal(next(ks), shape, jnp.float32) * s
    d = {}
    d["x_prompt"] = nrm((BATCH, SEQ, D_MODEL), 1.0)
    d["x_sample"] = nrm((DEC_BATCH, DEC_SEQ, D_MODEL), 1.0)
    d["c_prompt"] = nrm((BATCH, D_MODEL), 1.0)
    d["c_sample"] = nrm((DEC_BATCH, D_MODEL), 1.0)
    d["state_lru_h"] = nrm((DEPTH, DEC_BATCH, D_HALF), 0.5)
    d["state_lru_conv"] = nrm((DEPTH, DEC_BATCH, LRU_CONV - 1, D_HALF), 0.5)
    d["state_sc_conv"] = nrm((DEPTH, DEC_BATCH, SC_CONV - 1, D_HALF), 0.5)
    d["state_ffn_conv"] = nrm((DEPTH, DEC_BATCH, FFN_CONV - 1, D_FF), 0.5)
    d["w_ada"] = nrm((DEPTH, D_MODEL, 6 * D_MODEL), 0.5 * D_MODEL ** -0.5)
    d["b_ada"] = nrm((DEPTH, 6 * D_MODEL), 0.01)
    d["norm1_g"] = 1.0 + nrm((DEPTH, D_MODEL), 0.01)
    d["norm2_g"] = 1.0 + nrm((DEPTH, D_MODEL), 0.01)
    d["w_in"] = nrm((DEPTH, D_MODEL, N_IN), D_MODEL ** -0.5)
    d["lru_conv_w"] = nrm((DEPTH, LRU_CONV, D_HALF), LRU_CONV ** -0.5)
    d["lru_conv_b"] = nrm((DEPTH, D_HALF), 0.01)
    d["lru_wa"] = nrm((DEPTH, LRU_HEADS, LRU_HEAD_DIM, LRU_HEAD_DIM), LRU_HEAD_DIM ** -0.5)
    d["lru_ba"] = nrm((DEPTH, D_HALF), 0.01)
    d["lru_wx"] = nrm((DEPTH, LRU_HEADS, LRU_HEAD_DIM, LRU_HEAD_DIM), LRU_HEAD_DIM ** -0.5)
    d["lru_bx"] = nrm((DEPTH, D_HALF), 0.01)
    u = jax.random.uniform(next(ks), (DEPTH, D_HALF), jnp.float32, 0.9, 0.999)
    a0 = u ** (1.0 / RG_C)
    d["lru_lambda"] = jnp.log(a0) - jnp.log1p(-a0)
    d["sc_conv_w"] = nrm((DEPTH, SC_CONV, D_HALF), SC_CONV ** -0.5)
    d["w_out"] = nrm((DEPTH, D_MIX, D_MODEL), D_MIX ** -0.5)
    d["ffn_w_up"] = nrm((DEPTH, D_MODEL, 2 * D_FF), D_MODEL ** -0.5)
    d["ffn_conv_w"] = nrm((DEPTH, FFN_CONV, D_FF), FFN_CONV ** -0.5)
    d["ffn_w_down"] = nrm((DEPTH, D_FF, D_MODEL), D_FF ** -0.5)
    d["final_g"] = 1.0 + nrm((D_MODEL,), 0.01)
    return d


def reference(x_prompt, x_sample, c_prompt, c_sample, state_lru_h, state_lru_conv,
              state_sc_conv, state_ffn_conv, w_ada, b_ada, norm1_g, norm2_g, w_in,
              lru_conv_w, lru_conv_b, lru_wa, lru_ba, lru_wx, lru_bx, lru_lambda,
              sc_conv_w, w_out, ffn_w_up, ffn_conv_w, ffn_w_down, final_g):
    dt = x_prompt.dtype
    p_h = jnp.zeros((DEPTH, BATCH, D_HALF), dt)
    p_lc = jnp.zeros((DEPTH, BATCH, LRU_CONV - 1, D_HALF), dt)
    p_sc = jnp.zeros((DEPTH, BATCH, SC_CONV - 1, D_HALF), dt)
    p_fc = jnp.zeros((DEPTH, BATCH, FFN_CONV - 1, D_FF), dt)
    y_prompt, nh_p, nlc_p, nsc_p, nfc_p = run_trunk(
        x_prompt, c_prompt, p_h, p_lc, p_sc, p_fc, w_ada, b_ada, norm1_g, norm2_g,
        w_in, lru_conv_w, lru_conv_b, lru_wa, lru_ba, lru_wx, lru_bx, lru_lambda,
        sc_conv_w, w_out, ffn_w_up, ffn_conv_w, ffn_w_down, final_g)
    y_sample, nh_s, nlc_s, nsc_s, nfc_s = run_trunk(
        x_sample, c_sample, state_lru_h, state_lru_conv, state_sc_conv,
        state_ffn_conv, w_ada, b_ada, norm1_g, norm2_g, w_in, lru_conv_w,
        lru_conv_b, lru_wa, lru_ba, lru_wx, lru_bx, lru_lambda, sc_conv_w, w_out,
        ffn_w_up, ffn_conv_w, ffn_w_down, final_g)
    return (y_prompt, y_sample, nh_p, nlc_p, nsc_p, nfc_p, nh_s, nlc_s, nsc_s, nfc_s)
```

```python
import functools

import jax
import jax.numpy as jnp
from jax.experimental import pallas as pl
from jax.experimental.pallas import tpu as pltpu

D_MODEL = 1024
DEPTH = 2
D_HALF = 512
LRU_HEADS = 8
LRU_HEAD_DIM = 64
LRU_CONV = 4
SC_CONV = 3
FFN_CONV = 3
RG_C = 8.0
D_FF = 2816
EPS = 1e-6

MXU_COLS = 256
VMEM_LIMIT_BYTES = 56 * 1024 * 1024

BF16 = jnp.bfloat16
F32 = jnp.float32


def _dot(a, b):
    return jnp.dot(a, b, preferred_element_type=F32)


def _modulated_norm(x3, gain, scale, shift):
    ms = jnp.mean(x3 * x3, axis=-1, keepdims=True)
    y = x3 * jax.lax.rsqrt(ms + EPS)
    return y * (gain * (1.0 + scale))[None] + shift[None]


def _ada_kernel(c_ref, w_ref, b_ref, o_ref):
    c = c_ref[...]
    o_ref[...] = _dot(jax.nn.silu(c).astype(BF16), w_ref[...]) + b_ref[...]


def _ada_call(c_all, w_ada, b_ada):
    n = c_all.shape[0]
    return pl.pallas_call(
        _ada_kernel,
        grid=(DEPTH, 6),
        in_specs=[
            pl.BlockSpec((n, D_MODEL), lambda l, k: (0, 0)),
            pl.BlockSpec((None, D_MODEL, D_MODEL), lambda l, k: (l, 0, k)),
            pl.BlockSpec((None, 1, D_MODEL), lambda l, k: (l, 0, k)),
        ],
        out_specs=pl.BlockSpec((None, n, D_MODEL), lambda l, k: (l, 0, k)),
        out_shape=jax.ShapeDtypeStruct((DEPTH, n, 6 * D_MODEL), F32),
        name="ada",
    )(c_all, w_ada, b_ada.reshape(DEPTH, 1, 6 * D_MODEL))


def _mix_kernel(x_ref, sh_ref, sc_ref, g_ref, ng_ref, sth_ref, stlc_ref, stsc_ref,
                win_ref, cw_ref, cb_ref, wg_ref, bg_ref, lam_ref, scw_ref, wout_ref,
                y_ref, nh_ref, nlc_ref, nsc_ref,
                hn_buf, xl_buf, z_buf, a_buf, b_buf, mix_buf, h_car, *, tT, nB):
    R = tT * nB
    C = D_HALF
    j = pl.program_id(1)

    @pl.when(j == 0)
    def _():
        xl_buf[0:3 * nB] = stlc_ref[...].reshape(3 * nB, C)
        z_buf[0:2 * nB] = stsc_ref[...].reshape(2 * nB, C)
        h_car[...] = sth_ref[...]

    x3 = x_ref[...]
    hn = _modulated_norm(x3, ng_ref[...], sc_ref[...], sh_ref[...])
    hn_buf[...] = hn.reshape(R, D_MODEL).astype(BF16)

    xl_buf[3 * nB:3 * nB + R] = _dot(hn_buf[...], win_ref[:, 0:C])
    xc = cb_ref[...] + cw_ref[0:1] * xl_buf[0:R]
    for k in range(1, LRU_CONV):
        xc = xc + cw_ref[k:k + 1] * xl_buf[k * nB:k * nB + R]

    neg_c_sp = -RG_C * jnp.logaddexp(-lam_ref[...], 0.0)
    xc_bf = xc.astype(BF16)
    for hf in range(C // MXU_COLS):
        lo, hi = hf * MXU_COLS, (hf + 1) * MXU_COLS
        gates = _dot(xc_bf[:, lo:hi], wg_ref[hf]) + bg_ref[hf]
        r = jax.nn.sigmoid(gates[:, :MXU_COLS])
        ig = jax.nn.sigmoid(gates[:, MXU_COLS:])
        log_a = r * neg_c_sp[:, lo:hi]
        a = jnp.exp(log_a)
        mult = jnp.sqrt(-jnp.tanh(log_a) * (1.0 + a * a))
        a_buf[:, lo:hi] = a
        b_buf[:, lo:hi] = mult * (ig * xc[:, lo:hi])

    h = h_car[...]
    for t in range(tT):
        rows = pl.ds(t * nB, nB)
        h = a_buf[rows] * h + b_buf[rows]
        b_buf[rows] = h
    h_car[...] = h
    nh_ref[...] = h

    gl = _dot(hn_buf[...], win_ref[:, C:2 * C])
    mix_buf[:, 0:C] = (b_buf[...] * jax.nn.gelu(gl)).astype(BF16)

    z_buf[2 * nB:2 * nB + R] = (_dot(hn_buf[...], win_ref[:, 3 * C:4 * C])
                                * _dot(hn_buf[...], win_ref[:, 4 * C:5 * C]))
    zc = scw_ref[0:1] * z_buf[0:R]
    for k in range(1, SC_CONV):
        zc = zc + scw_ref[k:k + 1] * z_buf[k * nB:k * nB + R]
    bs = _dot(hn_buf[...], win_ref[:, 2 * C:3 * C])
    mix_buf[:, C:2 * C] = (bs * zc).astype(BF16)

    mix = _dot(mix_buf[...], wout_ref[...])
    y_ref[...] = x_ref[...] + g_ref[...][None] * mix.reshape(tT, nB, D_MODEL)

    lc_tail = xl_buf[R:R + 3 * nB]
    sc_tail = z_buf[R:R + 2 * nB]
    nlc_ref[...] = lc_tail.reshape(3, nB, C)
    nsc_ref[...] = sc_tail.reshape(2, nB, C)
    xl_buf[0:3 * nB] = lc_tail
    z_buf[0:2 * nB] = sc_tail


def _mix_call(l, x, mod, mod_blk0, st_h, st_lc, st_sc, w, *, tT, nB):
    T, B, _ = x.shape
    R = tT * nB
    C = D_HALF
    const2 = lambda i, j: (l, 0, 0)
    mod_spec = lambda k: pl.BlockSpec((None, nB, D_MODEL), lambda i, j: (l, mod_blk0 + i, k))
    once = pl.Buffered(1)
    return pl.pallas_call(
        functools.partial(_mix_kernel, tT=tT, nB=nB),
        grid=(B // nB, T // tT),
        in_specs=[
            pl.BlockSpec((tT, nB, D_MODEL), lambda i, j: (j, i, 0)),
            mod_spec(0), mod_spec(1), mod_spec(2),
            pl.BlockSpec((None, 1, D_MODEL), const2),
            pl.BlockSpec((None, nB, C), lambda i, j: (l, i, 0)),
            pl.BlockSpec((None, 3, nB, C), lambda i, j: (l, 0, i, 0)),
            pl.BlockSpec((None, 2, nB, C), lambda i, j: (l, 0, i, 0)),
            pl.BlockSpec((None, D_MODEL, 5 * C), const2, pipeline_mode=once),
            pl.BlockSpec((None, LRU_CONV, C), const2),
            pl.BlockSpec((None, 1, C), const2),
            pl.BlockSpec((None, C // MXU_COLS, MXU_COLS, 2 * MXU_COLS),
                         lambda i, j: (l, 0, 0, 0), pipeline_mode=once),
            pl.BlockSpec((None, C // MXU_COLS, 1, 2 * MXU_COLS), lambda i, j: (l, 0, 0, 0)),
            pl.BlockSpec((None, 1, C), const2),
            pl.BlockSpec((None, SC_CONV, C), const2),
            pl.BlockSpec((None, D_MODEL, D_MODEL), const2, pipeline_mode=once),
        ],
        out_specs=[
            pl.BlockSpec((tT, nB, D_MODEL), lambda i, j: (j, i, 0)),
            pl.BlockSpec((nB, C), lambda i, j: (i, 0)),
            pl.BlockSpec((3, nB, C), lambda i, j: (0, i, 0)),
            pl.BlockSpec((2, nB, C), lambda i, j: (0, i, 0)),
        ],
        out_shape=[
            jax.ShapeDtypeStruct((T, B, D_MODEL), F32),
            jax.ShapeDtypeStruct((B, C), F32),
            jax.ShapeDtypeStruct((3, B, C), F32),
            jax.ShapeDtypeStruct((2, B, C), F32),
        ],
        scratch_shapes=[
            pltpu.VMEM((R, D_MODEL), BF16),
            pltpu.VMEM((R + 3 * nB, C), F32),
            pltpu.VMEM((R + 2 * nB, C), F32),
            pltpu.VMEM((R, C), F32),
            pltpu.VMEM((R, C), F32),
            pltpu.VMEM((R, D_MODEL), BF16),
            pltpu.VMEM((nB, C), F32),
        ],
        compiler_params=pltpu.CompilerParams(
            dimension_semantics=("arbitrary", "arbitrary"),
            vmem_limit_bytes=VMEM_LIMIT_BYTES),
        name="mix",
    )(x, mod, mod, mod, w["norm1_g"], st_h, st_lc, st_sc,
      w["w_in"], w["lru_conv_w"], w["lru_conv_b"], w["w_gate"], w["b_gate"],
      w["lru_lambda"], w["sc_conv_w"], w["w_out"])


def _ffn_kernel(x_ref, sh_ref, sc_ref, g_ref, ng_ref, fg_ref, stfc_ref,
                wup_ref, cw_ref, wdn_ref,
                y_ref, nfc_ref,
                hn_buf, u_buf, act_buf, *, tT, nB, final):
    R = tT * nB
    j = pl.program_id(1)

    @pl.when(j == 0)
    def _():
        u_buf[0:2 * nB] = stfc_ref[...].reshape(2 * nB, D_FF)

    x3 = x_ref[...]
    hn = _modulated_norm(x3, ng_ref[...], sc_ref[...], sh_ref[...])
    hn_buf[...] = hn.reshape(R, D_MODEL).astype(BF16)

    for c0 in range(0, D_FF, MXU_COLS):
        cols = slice(c0, c0 + MXU_COLS)
        u_buf[2 * nB:2 * nB + R, cols] = _dot(hn_buf[...], wup_ref[:, cols])
        v = _dot(hn_buf[...], wup_ref[:, D_FF + c0:D_FF + c0 + MXU_COLS])
        uc = cw_ref[0:1, cols] * u_buf[0:R, cols]
        for k in range(1, FFN_CONV):
            uc = uc + cw_ref[k:k + 1, cols] * u_buf[k * nB:k * nB + R, cols]
        act_buf[:, cols] = (jax.nn.gelu(uc) * v).astype(BF16)

    out = _dot(act_buf[...], wdn_ref[...])
    xn = x_ref[...] + g_ref[...][None] * out.reshape(tT, nB, D_MODEL)
    if final:
        ms = jnp.mean(xn * xn, axis=-1, keepdims=True)
        xn = xn * jax.lax.rsqrt(ms + EPS) * fg_ref[...][None]
    y_ref[...] = xn

    tail = u_buf[R:R + 2 * nB]
    nfc_ref[...] = tail.reshape(2, nB, D_FF)
    u_buf[0:2 * nB] = tail


def _ffn_call(l, x, mod, mod_blk0, st_fc, w, *, tT, nB, final):
    T, B, _ = x.shape
    R = tT * nB
    const2 = lambda i, j: (l, 0, 0)
    mod_spec = lambda k: pl.BlockSpec((None, nB, D_MODEL), lambda i, j: (l, mod_blk0 + i, k))
    once = pl.Buffered(1)
    return pl.pallas_call(
        functools.partial(_ffn_kernel, tT=tT, nB=nB, final=final),
        grid=(B // nB, T // tT),
        in_specs=[
            pl.BlockSpec((tT, nB, D_MODEL), lambda i, j: (j, i, 0)),
            mod_spec(3), mod_spec(4), mod_spec(5),
            pl.BlockSpec((None, 1, D_MODEL), const2),
            pl.BlockSpec((1, D_MODEL), lambda i, j: (0, 0)),
            pl.BlockSpec((None, 2, nB, D_FF), lambda i, j: (l, 0, i, 0)),
            pl.BlockSpec((None, D_MODEL, 2 * D_FF), const2, pipeline_mode=once),
            pl.BlockSpec((None, FFN_CONV, D_FF), const2),
            pl.BlockSpec((None, D_FF, D_MODEL), const2, pipeline_mode=once),
        ],
        out_specs=[
            pl.BlockSpec((tT, nB, D_MODEL), lambda i, j: (j, i, 0)),
            pl.BlockSpec((2, nB, D_FF), lambda i, j: (0, i, 0)),
        ],
        out_shape=[
            jax.ShapeDtypeStruct((T, B, D_MODEL), F32),
            jax.ShapeDtypeStruct((2, B, D_FF), F32),
        ],
        scratch_shapes=[
            pltpu.VMEM((R, D_MODEL), BF16),
            pltpu.VMEM((R + 2 * nB, D_FF), F32),
            pltpu.VMEM((R, D_FF), BF16),
        ],
        compiler_params=pltpu.CompilerParams(
            dimension_semantics=("arbitrary", "arbitrary"),
            vmem_limit_bytes=VMEM_LIMIT_BYTES),
        name="ffn",
    )(x, mod, mod, mod, w["norm2_g"], w["final_g"], st_fc,
      w["ffn_w_up"], w["ffn_conv_w"], w["ffn_w_down"])


def _gate_weights(lru_wa, lru_wx, lru_ba, lru_bx):
    eye = jnp.eye(LRU_HEADS, dtype=lru_wa.dtype)
    dense = lambda w: jnp.einsum("lhij,hk->lhikj", w, eye).reshape(DEPTH, D_HALF, D_HALF)
    wa, wx = dense(lru_wa), dense(lru_wx)
    ws, bs = [], []
    for hf in range(D_HALF // MXU_COLS):
        s = slice(hf * MXU_COLS, (hf + 1) * MXU_COLS)
        ws.append(jnp.concatenate([wa[:, s, s], wx[:, s, s]], axis=-1))
        bs.append(jnp.concatenate([lru_ba[:, s], lru_bx[:, s]], axis=-1)[:, None, :])
    return jnp.stack(ws, axis=1).astype(BF16), jnp.stack(bs, axis=1)


def _run_trunk(x_tm, mod, mod_blk0, st_h, st_lc, st_sc, st_fc, w, *, tT, nB):
    hs, lcs, scs, fcs = [], [], [], []
    for l in range(DEPTH):
        x_tm, nh, nlc, nsc = _mix_call(l, x_tm, mod, mod_blk0, st_h, st_lc, st_sc, w,
                                       tT=tT, nB=nB)
        x_tm, nfc = _ffn_call(l, x_tm, mod, mod_blk0, st_fc, w, tT=tT, nB=nB,
                              final=(l == DEPTH - 1))
        hs.append(nh); lcs.append(nlc); scs.append(nsc); fcs.append(nfc)
    to_bm = lambda xs: jnp.swapaxes(jnp.stack(xs), 1, 2)
    return (jnp.swapaxes(x_tm, 0, 1), jnp.stack(hs), to_bm(lcs), to_bm(scs), to_bm(fcs))


def kernel(x_prompt, x_sample, c_prompt, c_sample, state_lru_h, state_lru_conv, state_sc_conv, state_ffn_conv, w_ada, b_ada, norm1_g, norm2_g, w_in, lru_conv_w, lru_conv_b, lru_wa, lru_ba, lru_wx, lru_bx, lru_lambda, sc_conv_w, w_out, ffn_w_up, ffn_conv_w, ffn_w_down, final_g):
    nb_p, t_p = x_prompt.shape[0], x_prompt.shape[1]
    nb_s, t_s = x_sample.shape[0], x_sample.shape[1]
    w_gate, b_gate = _gate_weights(lru_wa, lru_wx, lru_ba, lru_bx)
    w = dict(
        norm1_g=norm1_g[:, None, :], norm2_g=norm2_g[:, None, :], final_g=final_g[None, :],
        w_in=w_in.astype(BF16), lru_conv_w=lru_conv_w, lru_conv_b=lru_conv_b[:, None, :],
        w_gate=w_gate, b_gate=b_gate, lru_lambda=lru_lambda[:, None, :],
        sc_conv_w=sc_conv_w, w_out=w_out.astype(BF16),
        ffn_w_up=ffn_w_up.astype(BF16), ffn_conv_w=ffn_conv_w,
        ffn_w_down=ffn_w_down.astype(BF16))

    mod = _ada_call(jnp.concatenate([c_sample, c_prompt], axis=0), w_ada.astype(BF16), b_ada)

    dt = x_prompt.dtype
    zeros = lambda k, c: jnp.zeros((DEPTH, k, nb_p, c), dt)
    tT_p = 64
    out_p = _run_trunk(jnp.swapaxes(x_prompt, 0, 1), mod, nb_s // nb_p,
                       jnp.zeros((DEPTH, nb_p, D_HALF), dt), zeros(LRU_CONV - 1, D_HALF),
                       zeros(SC_CONV - 1, D_HALF), zeros(FFN_CONV - 1, D_FF),
                       w, tT=tT_p, nB=nb_p)
    nB_s = 64
    out_s = _run_trunk(jnp.swapaxes(x_sample, 0, 1), mod, 0,
                       state_lru_h, jnp.swapaxes(state_lru_conv, 1, 2),
                       jnp.swapaxes(state_sc_conv, 1, 2), jnp.swapaxes(state_ffn_conv, 1, 2),
                       w, tT=t_s, nB=nB_s)
    return (out_p[0], out_s[0]) + out_p[1:] + out_s[1:]
```

```python
import functools

import jax
import jax.numpy as jnp
from jax.experimental import pallas as pl
from jax.experimental.pallas import tpu as pltpu

D_MODEL = 1024
DEPTH = 2
D_HALF = 512
LRU_HEADS = 8
LRU_HEAD_DIM = 64
LRU_CONV = 4
SC_CONV = 3
FFN_CONV = 3
RG_C = 8.0
D_FF = 2816
EPS = 1e-6

MXU_COLS = 256
VMEM_LIMIT_BYTES = 56 * 1024 * 1024

BF16 = jnp.bfloat16
F32 = jnp.float32


def _dot(a, b):
    return jnp.dot(a, b, preferred_element_type=F32)


def _modulated_norm(x3, gain, scale, shift):
    ms = jnp.mean(x3 * x3, axis=-1, keepdims=True)
    y = x3 * jax.lax.rsqrt(ms + EPS)
    return y * (gain * (1.0 + scale))[None] + shift[None]


def _load_rows(buf, st_ref, n_rows, nB):
    if st_ref is None:
        buf[0:n_rows * nB] = jnp.zeros((n_rows * nB, buf.shape[1]), buf.dtype)
    else:
        for k in range(n_rows):
            buf[k * nB:(k + 1) * nB] = st_ref[:, k, :]


def _store_rows(out_ref, tail, n_rows, nB):
    for k in range(n_rows):
        out_ref[:, k, :] = tail[k * nB:(k + 1) * nB]


def _ada_kernel(cp_ref, cs_ref, w_ref, b_ref, op_ref, os_ref):
    w = w_ref[...].astype(BF16)
    op_ref[...] = _dot(jax.nn.silu(cp_ref[...]).astype(BF16), w) + b_ref[...]
    os_ref[...] = _dot(jax.nn.silu(cs_ref[...]).astype(BF16), w) + b_ref[...]


def _ada_call(c_prompt, c_sample, w_ada, b_ada):
    n_p, n_s = c_prompt.shape[0], c_sample.shape[0]
    c_spec = lambda n: pl.BlockSpec((n, D_MODEL), lambda l, k: (0, 0))
    o_spec = lambda n: pl.BlockSpec((None, n, D_MODEL), lambda l, k: (l, 0, k))
    return pl.pallas_call(
        _ada_kernel,
        grid=(DEPTH, 6),
        in_specs=[
            c_spec(n_p), c_spec(n_s),
            pl.BlockSpec((None, D_MODEL, D_MODEL), lambda l, k: (l, 0, k)),
            pl.BlockSpec((None, 1, D_MODEL), lambda l, k: (l, 0, k)),
        ],
        out_specs=[o_spec(n_p), o_spec(n_s)],
        out_shape=[jax.ShapeDtypeStruct((DEPTH, n_p, 6 * D_MODEL), F32),
                   jax.ShapeDtypeStruct((DEPTH, n_s, 6 * D_MODEL), F32)],
        name="ada",
    )(c_prompt, c_sample, w_ada, b_ada.reshape(DEPTH, 1, 6 * D_MODEL))


def _mix_kernel(*refs, tT, nB, x_bm, has_state):
    refs = list(refs)
    x_ref, sh_ref, sc_ref, g_ref, ng_ref = refs[:5]
    del refs[:5]
    sth_ref = stlc_ref = stsc_ref = None
    if has_state:
        sth_ref, stlc_ref, stsc_ref = refs[:3]
        del refs[:3]
    (win_ref, cw_ref, cb_ref, wg_ref, bg_ref, lam_ref, scw_ref, wout_ref,
     y_ref, nh_ref, nlc_ref, nsc_ref,
     hn_buf, xl_buf, z_buf, a_buf, b_buf, mix_buf, h_car) = refs[:19]
    xt_buf = refs[19] if x_bm else None

    R = tT * nB
    C = D_HALF
    j = pl.program_id(1)

    @pl.when(j == 0)
    def _():
        _load_rows(xl_buf, stlc_ref, LRU_CONV - 1, nB)
        _load_rows(z_buf, stsc_ref, SC_CONV - 1, nB)
        h_car[...] = sth_ref[...] if has_state else jnp.zeros((nB, C), F32)

    if x_bm:
        xt_buf[...] = jnp.swapaxes(x_ref[...], 0, 1)
        x_tm = xt_buf
    else:
        x_tm = x_ref
    hn = _modulated_norm(x_tm[...], ng_ref[...], sc_ref[...], sh_ref[...])
    hn_buf[...] = hn.reshape(R, D_MODEL).astype(BF16)

    xl_buf[3 * nB:3 * nB + R] = _dot(hn_buf[...], win_ref[:, 0:C])
    xc = cb_ref[...] + cw_ref[0:1] * xl_buf[0:R]
    for k in range(1, LRU_CONV):
        xc = xc + cw_ref[k:k + 1] * xl_buf[k * nB:k * nB + R]

    neg_c_sp = -RG_C * jnp.logaddexp(-lam_ref[...], 0.0)
    xc_bf = xc.astype(BF16)
    for hf in range(C // MXU_COLS):
        lo, hi = hf * MXU_COLS, (hf + 1) * MXU_COLS
        gates = _dot(xc_bf[:, lo:hi], wg_ref[hf]) + bg_ref[hf]
        r = jax.nn.sigmoid(gates[:, :MXU_COLS])
        ig = jax.nn.sigmoid(gates[:, MXU_COLS:])
        log_a = r * neg_c_sp[:, lo:hi]
        a = jnp.exp(log_a)
        mult = jnp.sqrt(-jnp.tanh(log_a) * (1.0 + a * a))
        a_buf[:, lo:hi] = a
        b_buf[:, lo:hi] = mult * (ig * xc[:, lo:hi])

    h = h_car[...]
    for t in range(tT):
        rows = pl.ds(t * nB, nB)
        h = a_buf[rows] * h + b_buf[rows]
        b_buf[rows] = h
    h_car[...] = h
    nh_ref[...] = h

    gl = _dot(hn_buf[...], win_ref[:, C:2 * C])
    mix_buf[:, 0:C] = (b_buf[...] * jax.nn.gelu(gl)).astype(BF16)

    z_buf[2 * nB:2 * nB + R] = (_dot(hn_buf[...], win_ref[:, 3 * C:4 * C])
                                * _dot(hn_buf[...], win_ref[:, 4 * C:5 * C]))
    zc = scw_ref[0:1] * z_buf[0:R]
    for k in range(1, SC_CONV):
        zc = zc + scw_ref[k:k + 1] * z_buf[k * nB:k * nB + R]
    bs = _dot(hn_buf[...], win_ref[:, 2 * C:3 * C])
    mix_buf[:, C:2 * C] = (bs * zc).astype(BF16)

    mix = _dot(mix_buf[...], wout_ref[...])
    y_ref[...] = x_tm[...] + g_ref[...][None] * mix.reshape(tT, nB, D_MODEL)

    lc_tail = xl_buf[R:R + 3 * nB]
    sc_tail = z_buf[R:R + 2 * nB]
    _store_rows(nlc_ref, lc_tail, LRU_CONV - 1, nB)
    _store_rows(nsc_ref, sc_tail, SC_CONV - 1, nB)
    xl_buf[0:3 * nB] = lc_tail
    z_buf[0:2 * nB] = sc_tail


def _mix_call(l, x, mod, state, w, *, tT, nB, x_bm):
    if x_bm:
        B, T, _ = x.shape
        x_spec = pl.BlockSpec((nB, tT, D_MODEL), lambda i, j: (i, j, 0))
    else:
        T, B, _ = x.shape
        x_spec = pl.BlockSpec((tT, nB, D_MODEL), lambda i, j: (j, i, 0))
    R = tT * nB
    C = D_HALF
    const2 = lambda i, j: (l, 0, 0)
    mod_spec = lambda k: pl.BlockSpec((None, nB, D_MODEL), lambda i, j: (l, i, k))
    once = pl.Buffered(1)
    state_specs, state_args = [], []
    if state is not None:
        state_specs = [
            pl.BlockSpec((None, nB, C), lambda i, j: (l, i, 0)),
            pl.BlockSpec((None, nB, LRU_CONV - 1, C), lambda i, j: (l, i, 0, 0)),
            pl.BlockSpec((None, nB, SC_CONV - 1, C), lambda i, j: (l, i, 0, 0)),
        ]
        state_args = list(state)
    scratch = [
        pltpu.VMEM((R, D_MODEL), BF16),
        pltpu.VMEM((R + 3 * nB, C), F32),
        pltpu.VMEM((R + 2 * nB, C), F32),
        pltpu.VMEM((R, C), F32),
        pltpu.VMEM((R, C), F32),
        pltpu.VMEM((R, D_MODEL), BF16),
        pltpu.VMEM((nB, C), F32),
    ]
    if x_bm:
        scratch.append(pltpu.VMEM((tT, nB, D_MODEL), F32))
    return pl.pallas_call(
        functools.partial(_mix_kernel, tT=tT, nB=nB, x_bm=x_bm, has_state=state is not None),
        grid=(B // nB, T // tT),
        in_specs=[
            x_spec, mod_spec(0), mod_spec(1), mod_spec(2),
            pl.BlockSpec((None, 1, D_MODEL), const2),
            *state_specs,
            pl.BlockSpec((None, D_MODEL, 5 * C), const2, pipeline_mode=once),
            pl.BlockSpec((None, LRU_CONV, C), const2),
            pl.BlockSpec((None, 1, C), const2),
            pl.BlockSpec((None, C // MXU_COLS, MXU_COLS, 2 * MXU_COLS),
                         lambda i, j: (l, 0, 0, 0), pipeline_mode=once),
            pl.BlockSpec((None, C // MXU_COLS, 1, 2 * MXU_COLS), lambda i, j: (l, 0, 0, 0)),
            pl.BlockSpec((None, 1, C), const2),
            pl.BlockSpec((None, SC_CONV, C), const2),
            pl.BlockSpec((None, D_MODEL, D_MODEL), const2, pipeline_mode=once),
        ],
        out_specs=[
            pl.BlockSpec((tT, nB, D_MODEL), lambda i, j: (j, i, 0)),
            pl.BlockSpec((nB, C), lambda i, j: (i, 0)),
            pl.BlockSpec((nB, LRU_CONV - 1, C), lambda i, j: (i, 0, 0)),
            pl.BlockSpec((nB, SC_CONV - 1, C), lambda i, j: (i, 0, 0)),
        ],
        out_shape=[
            jax.ShapeDtypeStruct((T, B, D_MODEL), F32),
            jax.ShapeDtypeStruct((B, C), F32),
            jax.ShapeDtypeStruct((B, LRU_CONV - 1, C), F32),
            jax.ShapeDtypeStruct((B, SC_CONV - 1, C), F32),
        ],
        scratch_shapes=scratch,
        compiler_params=pltpu.CompilerParams(
            dimension_semantics=("arbitrary", "arbitrary"),
            vmem_limit_bytes=VMEM_LIMIT_BYTES),
        name="mix",
    )(x, mod, mod, mod, w["norm1_g"], *state_args,
      w["w_in"], w["lru_conv_w"], w["lru_conv_b"], w["w_gate"], w["b_gate"],
      w["lru_lambda"], w["sc_conv_w"], w["w_out"])


def _ffn_kernel(*refs, tT, nB, final, has_state):
    refs = list(refs)
    x_ref, sh_ref, sc_ref, g_ref, ng_ref, fg_ref = refs[:6]
    del refs[:6]
    stfc_ref = None
    if has_state:
        stfc_ref = refs.pop(0)
    wup_ref, cw_ref, wdn_ref, y_ref, nfc_ref, hn_buf, u_buf, act_buf = refs

    R = tT * nB
    j = pl.program_id(1)

    @pl.when(j == 0)
    def _():
        _load_rows(u_buf, stfc_ref, FFN_CONV - 1, nB)

    hn = _modulated_norm(x_ref[...], ng_ref[...], sc_ref[...], sh_ref[...])
    hn_buf[...] = hn.reshape(R, D_MODEL).astype(BF16)

    for c0 in range(0, D_FF, MXU_COLS):
        cols = slice(c0, c0 + MXU_COLS)
        u_buf[2 * nB:2 * nB + R, cols] = _dot(hn_buf[...], wup_ref[:, cols])
        v = _dot(hn_buf[...], wup_ref[:, D_FF + c0:D_FF + c0 + MXU_COLS])
        uc = cw_ref[0:1, cols] * u_buf[0:R, cols]
        for k in range(1, FFN_CONV):
            uc = uc + cw_ref[k:k + 1, cols] * u_buf[k * nB:k * nB + R, cols]
        act_buf[:, cols] = (jax.nn.gelu(uc) * v).astype(BF16)

    out = _dot(act_buf[...], wdn_ref[...])
    xn = x_ref[...] + g_ref[...][None] * out.reshape(tT, nB, D_MODEL)
    if final:
        ms = jnp.mean(xn * xn, axis=-1, keepdims=True)
        xn = xn * jax.lax.rsqrt(ms + EPS) * fg_ref[...][None]
        y_ref[...] = jnp.swapaxes(xn, 0, 1)
    else:
        y_ref[...] = xn

    tail = u_buf[R:R + 2 * nB]
    _store_rows(nfc_ref, tail, FFN_CONV - 1, nB)
    u_buf[0:2 * nB] = tail


def _ffn_call(l, x, mod, st_fc, w, *, tT, nB, final):
    T, B, _ = x.shape
    R = tT * nB
    const2 = lambda i, j: (l, 0, 0)
    mod_spec = lambda k: pl.BlockSpec((None, nB, D_MODEL), lambda i, j: (l, i, k))
    once = pl.Buffered(1)
    state_specs, state_args = [], []
    if st_fc is not None:
        state_specs = [pl.BlockSpec((None, nB, FFN_CONV - 1, D_FF), lambda i, j: (l, i, 0, 0))]
        state_args = [st_fc]
    if final:
        y_spec = pl.BlockSpec((nB, tT, D_MODEL), lambda i, j: (i, j, 0))
        y_shape = jax.ShapeDtypeStruct((B, T, D_MODEL), F32)
    else:
        y_spec = pl.BlockSpec((tT, nB, D_MODEL), lambda i, j: (j, i, 0))
        y_shape = jax.ShapeDtypeStruct((T, B, D_MODEL), F32)
    return pl.pallas_call(
        functools.partial(_ffn_kernel, tT=tT, nB=nB, final=final, has_state=st_fc is not None),
        grid=(B // nB, T // tT),
        in_specs=[
            pl.BlockSpec((tT, nB, D_MODEL), lambda i, j: (j, i, 0)),
            mod_spec(3), mod_spec(4), mod_spec(5),
            pl.BlockSpec((None, 1, D_MODEL), const2),
            pl.BlockSpec((1, D_MODEL), lambda i, j: (0, 0)),
            *state_specs,
            pl.BlockSpec((None, D_MODEL, 2 * D_FF), const2, pipeline_mode=once),
            pl.BlockSpec((None, FFN_CONV, D_FF), const2),
            pl.BlockSpec((None, D_FF, D_MODEL), const2, pipeline_mode=once),
        ],
        out_specs=[
            y_spec,
            pl.BlockSpec((nB, FFN_CONV - 1, D_FF), lambda i, j: (i, 0, 0)),
        ],
        out_shape=[
            y_shape,
            jax.ShapeDtypeStruct((B, FFN_CONV - 1, D_FF), F32),
        ],
        scratch_shapes=[
            pltpu.VMEM((R, D_MODEL), BF16),
            pltpu.VMEM((R + 2 * nB, D_FF), F32),
            pltpu.VMEM((R, D_FF), BF16),
        ],
        compiler_params=pltpu.CompilerParams(
            dimension_semantics=("arbitrary", "arbitrary"),
            vmem_limit_bytes=VMEM_LIMIT_BYTES),
        name="ffn",
    )(x, mod, mod, mod, w["norm2_g"], w["final_g"], *state_args,
      w["ffn_w_up"], w["ffn_conv_w"], w["ffn_w_down"])


def _gate_weights(lru_wa, lru_wx, lru_ba, lru_bx):
    eye = jnp.eye(LRU_HEADS, dtype=lru_wa.dtype)
    dense = lambda w: jnp.einsum("lhij,hk->lhikj", w, eye).reshape(DEPTH, D_HALF, D_HALF)
    wa, wx = dense(lru_wa), dense(lru_wx)
    ws, bs = [], []
    for hf in range(D_HALF // MXU_COLS):
        s = slice(hf * MXU_COLS, (hf + 1) * MXU_COLS)
        ws.append(jnp.concatenate([wa[:, s, s], wx[:, s, s]], axis=-1))
        bs.append(jnp.concatenate([lru_ba[:, s], lru_bx[:, s]], axis=-1)[:, None, :])
    return jnp.stack(ws, axis=1).astype(BF16), jnp.stack(bs, axis=1)


def _run_trunk(x_bm, mod, state, w, *, tT, nB):
    hs, lcs, scs, fcs = [], [], [], []
    x = x_bm
    for l in range(DEPTH):
        x, nh, nlc, nsc = _mix_call(l, x, mod, None if state is None else state[:3], w,
                                    tT=tT, nB=nB, x_bm=(l == 0))
        x, nfc = _ffn_call(l, x, mod, None if state is None else state[3], w,
                           tT=tT, nB=nB, final=(l == DEPTH - 1))
        hs.append(nh); lcs.append(nlc); scs.append(nsc); fcs.append(nfc)
    return (x, jnp.stack(hs), jnp.stack(lcs), jnp.stack(scs), jnp.stack(fcs))


def kernel(x_prompt, x_sample, c_prompt, c_sample, state_lru_h, state_lru_conv, state_sc_conv, state_ffn_conv, w_ada, b_ada, norm1_g, norm2_g, w_in, lru_conv_w, lru_conv_b, lru_wa, lru_ba, lru_wx, lru_bx, lru_lambda, sc_conv_w, w_out, ffn_w_up, ffn_conv_w, ffn_w_down, final_g):
    w_gate, b_gate = _gate_weights(lru_wa, lru_wx, lru_ba, lru_bx)
    w = dict(
        norm1_g=norm1_g[:, None, :], norm2_g=norm2_g[:, None, :], final_g=final_g[None, :],
        w_in=w_in.astype(BF16), lru_conv_w=lru_conv_w, lru_conv_b=lru_conv_b[:, None, :],
        w_gate=w_gate, b_gate=b_gate, lru_lambda=lru_lambda[:, None, :],
        sc_conv_w=sc_conv_w, w_out=w_out.astype(BF16),
        ffn_w_up=ffn_w_up.astype(BF16), ffn_conv_w=ffn_conv_w,
        ffn_w_down=ffn_w_down.astype(BF16))

    mod_p, mod_s = _ada_call(c_prompt, c_sample, w_ada, b_ada)

    out_p = _run_trunk(x_prompt, mod_p, None, w, tT=64, nB=x_prompt.shape[0])
    out_s = _run_trunk(x_sample, mod_s,
                       (state_lru_h, state_lru_conv, state_sc_conv, state_ffn_conv),
                       w, tT=x_sample.shape[1], nB=64)
    return (out_p[0], out_s[0]) + out_p[1:] + out_s[1:]
```

```python
import functools

import jax
import jax.numpy as jnp
from jax.experimental import pallas as pl
from jax.experimental.pallas import tpu as pltpu

D_MODEL = 1024
DEPTH = 2
D_HALF = 512
LRU_HEADS = 8
LRU_HEAD_DIM = 64
LRU_CONV = 4
SC_CONV = 3
FFN_CONV = 3
RG_C = 8.0
D_FF = 2816
EPS = 1e-6

MXU_COLS = 256
VMEM_LIMIT_BYTES = 56 * 1024 * 1024

BF16 = jnp.bfloat16
F32 = jnp.float32


def _dot(a, b):
    return jnp.dot(a, b, preferred_element_type=F32)


def _modulated_norm(x3, gain, scale, shift):
    ms = jnp.mean(x3 * x3, axis=-1, keepdims=True)
    y = x3 * jax.lax.rsqrt(ms + EPS)
    return y * (gain * (1.0 + scale))[None] + shift[None]


def _load_rows(buf, st_ref, n_rows, nB):
    if st_ref is None:
        buf[0:n_rows * nB] = jnp.zeros((n_rows * nB, buf.shape[1]), buf.dtype)
    else:
        for k in range(n_rows):
            buf[k * nB:(k + 1) * nB] = st_ref[:, k, :]


def _store_rows(out_ref, tail, n_rows, nB):
    for k in range(n_rows):
        out_ref[:, k, :] = tail[k * nB:(k + 1) * nB]


def _ada_kernel(cp_ref, cs_ref, w_ref, b_ref, op_ref, os_ref):
    w = w_ref[...].astype(BF16)
    op_ref[...] = _dot(jax.nn.silu(cp_ref[...]).astype(BF16), w) + b_ref[...]
    os_ref[...] = _dot(jax.nn.silu(cs_ref[...]).astype(BF16), w) + b_ref[...]


def _ada_call(c_prompt, c_sample, w_ada, b_ada):
    n_p, n_s = c_prompt.shape[0], c_sample.shape[0]
    c_spec = lambda n: pl.BlockSpec((n, D_MODEL), lambda l, k: (0, 0))
    o_spec = lambda n: pl.BlockSpec((None, n, D_MODEL), lambda l, k: (l, 0, k))
    return pl.pallas_call(
        _ada_kernel,
        grid=(DEPTH, 6),
        in_specs=[
            c_spec(n_p), c_spec(n_s),
            pl.BlockSpec((None, D_MODEL, D_MODEL), lambda l, k: (l, 0, k)),
            pl.BlockSpec((None, 1, D_MODEL), lambda l, k: (l, 0, k)),
        ],
        out_specs=[o_spec(n_p), o_spec(n_s)],
        out_shape=[jax.ShapeDtypeStruct((DEPTH, n_p, 6 * D_MODEL), F32),
                   jax.ShapeDtypeStruct((DEPTH, n_s, 6 * D_MODEL), F32)],
        name="ada",
    )(c_prompt, c_sample, w_ada, b_ada.reshape(DEPTH, 1, 6 * D_MODEL))


def _mix_kernel(*refs, tT, nB, x_bm, has_state):
    refs = list(refs)
    x_ref, sh_ref, sc_ref, g_ref, ng_ref = refs[:5]
    del refs[:5]
    sth_ref = stlc_ref = stsc_ref = None
    if has_state:
        sth_ref, stlc_ref, stsc_ref = refs[:3]
        del refs[:3]
    (win_ref, cw_ref, cb_ref, wg_ref, bg_ref, lam_ref, scw_ref, wout_ref,
     y_ref, nh_ref, nlc_ref, nsc_ref,
     hn_buf, xl_buf, z_buf, a_buf, b_buf, mix_buf, h_car) = refs[:19]
    xt_buf = refs[19] if x_bm else None

    R = tT * nB
    C = D_HALF
    j = pl.program_id(1)

    @pl.when(j == 0)
    def _():
        _load_rows(xl_buf, stlc_ref, LRU_CONV - 1, nB)
        _load_rows(z_buf, stsc_ref, SC_CONV - 1, nB)
        h_car[...] = sth_ref[...] if has_state else jnp.zeros((nB, C), F32)

    if x_bm:
        xt_buf[...] = jnp.swapaxes(x_ref[...], 0, 1)
        x_tm = xt_buf
    else:
        x_tm = x_ref
    hn = _modulated_norm(x_tm[...], ng_ref[...], sc_ref[...], sh_ref[...])
    hn_buf[...] = hn.reshape(R, D_MODEL).astype(BF16)

    in_proj = lambda k: _dot(hn_buf[...], win_ref[:, k * C:(k + 1) * C])

    xl_buf[3 * nB:3 * nB + R] = in_proj(0)
    xc = cb_ref[...] + cw_ref[0:1] * xl_buf[0:R]
    for k in range(1, LRU_CONV):
        xc = xc + cw_ref[k:k + 1] * xl_buf[k * nB:k * nB + R]

    neg_c_sp = -RG_C * jnp.logaddexp(-lam_ref[...], 0.0)
    xc_bf = xc.astype(BF16)
    for hf in range(C // MXU_COLS):
        lo, hi = hf * MXU_COLS, (hf + 1) * MXU_COLS
        gates = _dot(xc_bf[:, lo:hi], wg_ref[hf]) + bg_ref[hf]
        r = jax.nn.sigmoid(gates[:, :MXU_COLS])
        ig = jax.nn.sigmoid(gates[:, MXU_COLS:])
        log_a = r * neg_c_sp[:, lo:hi]
        a = jnp.exp(log_a)
        mult = jnp.sqrt(-jnp.tanh(log_a) * (1.0 + a * a))
        a_buf[:, lo:hi] = a
        b_buf[:, lo:hi] = mult * (ig * xc[:, lo:hi])

    h = h_car[...]
    for t in range(tT):
        rows = pl.ds(t * nB, nB)
        h = a_buf[rows] * h + b_buf[rows]
        b_buf[rows] = h
    h_car[...] = h
    nh_ref[...] = h

    mix_buf[:, 0:C] = (b_buf[...] * jax.nn.gelu(in_proj(1))).astype(BF16)

    z_buf[2 * nB:2 * nB + R] = in_proj(3) * in_proj(4)
    zc = scw_ref[0:1] * z_buf[0:R]
    for k in range(1, SC_CONV):
        zc = zc + scw_ref[k:k + 1] * z_buf[k * nB:k * nB + R]
    mix_buf[:, C:2 * C] = (in_proj(2) * zc).astype(BF16)

    mix = _dot(mix_buf[...], wout_ref[...])
    y_ref[...] = x_tm[...] + g_ref[...][None] * mix.reshape(tT, nB, D_MODEL)

    lc_tail = xl_buf[R:R + 3 * nB]
    sc_tail = z_buf[R:R + 2 * nB]
    _store_rows(nlc_ref, lc_tail, LRU_CONV - 1, nB)
    _store_rows(nsc_ref, sc_tail, SC_CONV - 1, nB)
    xl_buf[0:3 * nB] = lc_tail
    z_buf[0:2 * nB] = sc_tail


def _mix_call(l, x, mod, state, w, *, tT, nB, x_bm):
    if x_bm:
        B, T, _ = x.shape
        x_spec = pl.BlockSpec((nB, tT, D_MODEL), lambda i, j: (i, j, 0))
    else:
        T, B, _ = x.shape
        x_spec = pl.BlockSpec((tT, nB, D_MODEL), lambda i, j: (j, i, 0))
    R = tT * nB
    C = D_HALF
    const2 = lambda i, j: (l, 0, 0)
    mod_spec = lambda k: pl.BlockSpec((None, nB, D_MODEL), lambda i, j: (l, i, k))
    once = pl.Buffered(1)
    state_specs, state_args = [], []
    if state is not None:
        state_specs = [
            pl.BlockSpec((None, nB, C), lambda i, j: (l, i, 0)),
            pl.BlockSpec((None, nB, LRU_CONV - 1, C), lambda i, j: (l, i, 0, 0)),
            pl.BlockSpec((None, nB, SC_CONV - 1, C), lambda i, j: (l, i, 0, 0)),
        ]
        state_args = list(state)
    scratch = [
        pltpu.VMEM((R, D_MODEL), BF16),
        pltpu.VMEM((R + 3 * nB, C), F32),
        pltpu.VMEM((R + 2 * nB, C), F32),
        pltpu.VMEM((R, C), F32),
        pltpu.VMEM((R, C), F32),
        pltpu.VMEM((R, D_MODEL), BF16),
        pltpu.VMEM((nB, C), F32),
    ]
    if x_bm:
        scratch.append(pltpu.VMEM((tT, nB, D_MODEL), F32))
    return pl.pallas_call(
        functools.partial(_mix_kernel, tT=tT, nB=nB, x_bm=x_bm, has_state=state is not None),
        grid=(B // nB, T // tT),
        in_specs=[
            x_spec, mod_spec(0), mod_spec(1), mod_spec(2),
            pl.BlockSpec((None, 1, D_MODEL), const2),
            *state_specs,
            pl.BlockSpec((None, D_MODEL, 5 * C), const2, pipeline_mode=once),
            pl.BlockSpec((None, LRU_CONV, C), const2),
            pl.BlockSpec((None, 1, C), const2),
            pl.BlockSpec((None, C // MXU_COLS, MXU_COLS, 2 * MXU_COLS),
                         lambda i, j: (l, 0, 0, 0), pipeline_mode=once),
            pl.BlockSpec((None, C // MXU_COLS, 1, 2 * MXU_COLS), lambda i, j: (l, 0, 0, 0)),
            pl.BlockSpec((None, 1, C), const2),
            pl.BlockSpec((None, SC_CONV, C), const2),
            pl.BlockSpec((None, D_MODEL, D_MODEL), const2, pipeline_mode=once),
        ],
        out_specs=[
            pl.BlockSpec((tT, nB, D_MODEL), lambda i, j: (j, i, 0)),
            pl.BlockSpec((nB, C), lambda i, j: (i, 0)),
            pl.BlockSpec((nB, LRU_CONV - 1, C), lambda i, j: (i, 0, 0)),
            pl.BlockSpec((nB, SC_CONV - 1, C), lambda i, j: (i, 0, 0)),
        ],
        out_shape=[
            jax.ShapeDtypeStruct((T, B, D_MODEL), F32),
            jax.ShapeDtypeStruct((B, C), F32),
            jax.ShapeDtypeStruct((B, LRU_CONV - 1, C), F32),
            jax.ShapeDtypeStruct((B, SC_CONV - 1, C), F32),
        ],
        scratch_shapes=scratch,
        compiler_params=pltpu.CompilerParams(
            dimension_semantics=("arbitrary", "arbitrary"),
            vmem_limit_bytes=VMEM_LIMIT_BYTES),
        name="mix",
    )(x, mod, mod, mod, w["norm1_g"], *state_args,
      w["w_in"], w["lru_conv_w"], w["lru_conv_b"], w["w_gate"], w["b_gate"],
      w["lru_lambda"], w["sc_conv_w"], w["w_out"])


def _ffn_kernel(*refs, tT, nB, final, has_state):
    refs = list(refs)
    x_ref, sh_ref, sc_ref, g_ref, ng_ref, fg_ref = refs[:6]
    del refs[:6]
    stfc_ref = None
    if has_state:
        stfc_ref = refs.pop(0)
    wup_ref, cw_ref, wdn_ref, y_ref, nfc_ref, hn_buf, u_buf, u_car, act_buf = refs

    R = tT * nB
    j = pl.program_id(1)

    @pl.when(j == 0)
    def _():
        _load_rows(u_car, stfc_ref, FFN_CONV - 1, nB)

    hn = _modulated_norm(x_ref[...], ng_ref[...], sc_ref[...], sh_ref[...])
    hn_buf[...] = hn.reshape(R, D_MODEL).astype(BF16)

    for ci, c0 in enumerate(range(0, D_FF, MXU_COLS)):
        cols = slice(c0, c0 + MXU_COLS)
        ub = u_buf.at[ci % 2]
        ub[0:2 * nB] = u_car[:, cols]
        ub[2 * nB:2 * nB + R] = _dot(hn_buf[...], wup_ref[:, cols])
        v = _dot(hn_buf[...], wup_ref[:, D_FF + c0:D_FF + c0 + MXU_COLS])
        uc = cw_ref[0:1, cols] * ub[0:R]
        for k in range(1, FFN_CONV):
            uc = uc + cw_ref[k:k + 1, cols] * ub[k * nB:k * nB + R]
        act_buf[:, cols] = (jax.nn.gelu(uc) * v).astype(BF16)
        u_car[:, cols] = ub[R:R + 2 * nB]

    out = _dot(act_buf[...], wdn_ref[...])
    xn = x_ref[...] + g_ref[...][None] * out.reshape(tT, nB, D_MODEL)
    if final:
        ms = jnp.mean(xn * xn, axis=-1, keepdims=True)
        xn = xn * jax.lax.rsqrt(ms + EPS) * fg_ref[...][None]
        y_ref[...] = jnp.swapaxes(xn, 0, 1)
    else:
        y_ref[...] = xn

    _store_rows(nfc_ref, u_car[...], FFN_CONV - 1, nB)


def _ffn_call(l, x, mod, st_fc, w, *, tT, nB, final):
    T, B, _ = x.shape
    R = tT * nB
    const2 = lambda i, j: (l, 0, 0)
    mod_spec = lambda k: pl.BlockSpec((None, nB, D_MODEL), lambda i, j: (l, i, k))
    once = pl.Buffered(1)
    state_specs, state_args = [], []
    if st_fc is not None:
        state_specs = [pl.BlockSpec((None, nB, FFN_CONV - 1, D_FF), lambda i, j: (l, i, 0, 0))]
        state_args = [st_fc]
    if final:
        y_spec = pl.BlockSpec((nB, tT, D_MODEL), lambda i, j: (i, j, 0))
        y_shape = jax.ShapeDtypeStruct((B, T, D_MODEL), F32)
    else:
        y_spec = pl.BlockSpec((tT, nB, D_MODEL), lambda i, j: (j, i, 0))
        y_shape = jax.ShapeDtypeStruct((T, B, D_MODEL), F32)
    return pl.pallas_call(
        functools.partial(_ffn_kernel, tT=tT, nB=nB, final=final, has_state=st_fc is not None),
        grid=(B // nB, T // tT),
        in_specs=[
            pl.BlockSpec((tT, nB, D_MODEL), lambda i, j: (j, i, 0)),
            mod_spec(3), mod_spec(4), mod_spec(5),
            pl.BlockSpec((None, 1, D_MODEL), const2),
            pl.BlockSpec((1, D_MODEL), lambda i, j: (0, 0)),
            *state_specs,
            pl.BlockSpec((None, D_MODEL, 2 * D_FF), const2, pipeline_mode=once),
            pl.BlockSpec((None, FFN_CONV, D_FF), const2),
            pl.BlockSpec((None, D_FF, D_MODEL), const2, pipeline_mode=once),
        ],
        out_specs=[
            y_spec,
            pl.BlockSpec((nB, FFN_CONV - 1, D_FF), lambda i, j: (i, 0, 0)),
        ],
        out_shape=[
            y_shape,
            jax.ShapeDtypeStruct((B, FFN_CONV - 1, D_FF), F32),
        ],
        scratch_shapes=[
            pltpu.VMEM((R, D_MODEL), BF16),
            pltpu.VMEM((2, R + 2 * nB, MXU_COLS), F32),
            pltpu.VMEM((2 * nB, D_FF), F32),
            pltpu.VMEM((R, D_FF), BF16),
        ],
        compiler_params=pltpu.CompilerParams(
            dimension_semantics=("arbitrary", "arbitrary"),
            vmem_limit_bytes=VMEM_LIMIT_BYTES),
        name="ffn",
    )(x, mod, mod, mod, w["norm2_g"], w["final_g"], *state_args,
      w["ffn_w_up"], w["ffn_conv_w"], w["ffn_w_down"])


def _gate_weights(lru_wa, lru_wx, lru_ba, lru_bx):
    eye = jnp.eye(LRU_HEADS, dtype=lru_wa.dtype)
    dense = lambda w: jnp.einsum("lhij,hk->lhikj", w, eye).reshape(DEPTH, D_HALF, D_HALF)
    wa, wx = dense(lru_wa), dense(lru_wx)
    ws, bs = [], []
    for hf in range(D_HALF // MXU_COLS):
        s = slice(hf * MXU_COLS, (hf + 1) * MXU_COLS)
        ws.append(jnp.concatenate([wa[:, s, s], wx[:, s, s]], axis=-1))
        bs.append(jnp.concatenate([lru_ba[:, s], lru_bx[:, s]], axis=-1)[:, None, :])
    return jnp.stack(ws, axis=1).astype(BF16), jnp.stack(bs, axis=1)


def _run_trunk(x_bm, mod, state, w, *, tT, nB):
    hs, lcs, scs, fcs = [], [], [], []
    x = x_bm
    for l in range(DEPTH):
        x, nh, nlc, nsc = _mix_call(l, x, mod, None if state is None else state[:3], w,
                                    tT=tT, nB=nB, x_bm=(l == 0))
        x, nfc = _ffn_call(l, x, mod, None if state is None else state[3], w,
                           tT=tT, nB=nB, final=(l == DEPTH - 1))
        hs.append(nh); lcs.append(nlc); scs.append(nsc); fcs.append(nfc)
    return (x, jnp.stack(hs), jnp.stack(lcs), jnp.stack(scs), jnp.stack(fcs))


def kernel(x_prompt, x_sample, c_prompt, c_sample, state_lru_h, state_lru_conv, state_sc_conv, state_ffn_conv, w_ada, b_ada, norm1_g, norm2_g, w_in, lru_conv_w, lru_conv_b, lru_wa, lru_ba, lru_wx, lru_bx, lru_lambda, sc_conv_w, w_out, ffn_w_up, ffn_conv_w, ffn_w_down, final_g):
    w_gate, b_gate = _gate_weights(lru_wa, lru_wx, lru_ba, lru_bx)
    w = dict(
        norm1_g=norm1_g[:, None, :], norm2_g=norm2_g[:, None, :], final_g=final_g[None, :],
        w_in=w_in.astype(BF16), lru_conv_w=lru_conv_w, lru_conv_b=lru_conv_b[:, None, :],
        w_gate=w_gate, b_gate=b_gate, lru_lambda=lru_lambda[:, None, :],
        sc_conv_w=sc_conv_w, w_out=w_out.astype(BF16),
        ffn_w_up=ffn_w_up.astype(BF16), ffn_conv_w=ffn_conv_w,
        ffn_w_down=ffn_w_down.astype(BF16))

    mod_p, mod_s = _ada_call(c_prompt, c_sample, w_ada, b_ada)

    out_p = _run_trunk(x_prompt, mod_p, None, w, tT=128, nB=x_prompt.shape[0])
    out_s = _run_trunk(x_sample, mod_s,
                       (state_lru_h, state_lru_conv, state_sc_conv, state_ffn_conv),
                       w, tT=x_sample.shape[1], nB=x_sample.shape[0])
    return (out_p[0], out_s[0]) + out_p[1:] + out_s[1:]
```

```python
import functools

import jax
import jax.numpy as jnp
from jax.experimental import pallas as pl
from jax.experimental.pallas import tpu as pltpu

D_MODEL = 1024
DEPTH = 2
D_HALF = 512
LRU_HEADS = 8
LRU_HEAD_DIM = 64
LRU_CONV = 4
SC_CONV = 3
FFN_CONV = 3
RG_C = 8.0
D_FF = 2816
EPS = 1e-6
N_MOD = 6

MXU_COLS = 256
BF16_SUBLANES = 16
VMEM_LIMIT_BYTES = 56 * 1024 * 1024
ADA_COL_BLOCKS = 8

BF16 = jnp.bfloat16
F32 = jnp.float32


def _dot(a, b):
    return jnp.dot(a, b, preferred_element_type=F32)


def _take(refs, n):
    head = refs[:n]
    del refs[:n]
    return head


def _modulated_norm(x3, gain, scale, shift):
    ms = jnp.mean(x3 * x3, axis=-1, keepdims=True)
    y = x3 * jax.lax.rsqrt(ms + EPS)
    return y * (gain * (1.0 + scale))[None] + shift[None]


def _load_rows(buf, st_ref, n_rows, nB):
    if st_ref is None:
        buf[0:n_rows * nB] = jnp.zeros((n_rows * nB, buf.shape[1]), buf.dtype)
    else:
        for k in range(n_rows):
            buf[k * nB:(k + 1) * nB] = st_ref[:, k, :]


def _store_rows(out_ref, tail, n_rows, nB):
    for k in range(n_rows):
        out_ref[:, k, :] = tail[k * nB:(k + 1) * nB]


def _rider_specs(riders, n_steps, step_of):
    in_specs, out_specs, out_shapes = [], [], []
    for src, layer in riders:
        _, K, N = src.shape
        hold = 1
        while (K * hold) % n_steps or (K * hold // n_steps) % BF16_SUBLANES:
            hold *= 2
            assert hold <= n_steps, (K, n_steps)
        rows = K * hold // n_steps
        in_specs.append(pl.BlockSpec(
            (None, rows, N), lambda *g, l=layer, h=hold: (l, step_of(*g) // h, 0)))
        out_specs.append(pl.BlockSpec((rows, N), lambda *g, h=hold: (step_of(*g) // h, 0)))
        out_shapes.append(jax.ShapeDtypeStruct((K, N), BF16))
    return in_specs, out_specs, out_shapes


def _cast_riders(ride_in, ride_out):
    for src, dst in zip(ride_in, ride_out):
        dst[...] = src[...].astype(BF16)


def _ada_kernel(*refs, n_ride):
    refs = list(refs)
    cp_ref, cs_ref, w_ref, b_ref = _take(refs, 4)
    ride_in = _take(refs, n_ride)
    op_ref, os_ref = _take(refs, 2)
    ride_out = _take(refs, n_ride)
    w = w_ref[...].astype(BF16)
    op_ref[...] = _dot(jax.nn.silu(cp_ref[...]).astype(BF16), w) + b_ref[...]
    os_ref[...] = _dot(jax.nn.silu(cs_ref[...]).astype(BF16), w) + b_ref[...]
    _cast_riders(ride_in, ride_out)


def _ada_call(c_prompt, c_sample, w_ada, b_ada, riders):
    n_p, n_s = c_prompt.shape[0], c_sample.shape[0]
    n_steps = DEPTH * ADA_COL_BLOCKS
    cols = N_MOD * D_MODEL // ADA_COL_BLOCKS
    lk = lambda s: (s // ADA_COL_BLOCKS, 0, s % ADA_COL_BLOCKS)
    c_spec = lambda n: pl.BlockSpec((n, D_MODEL), lambda s: (0, 0))
    o_spec = lambda n: pl.BlockSpec((None, n, cols), lk)
    r_in, r_out, r_shapes = _rider_specs(riders, n_steps, lambda s: s)
    outs = pl.pallas_call(
        functools.partial(_ada_kernel, n_ride=len(riders)),
        grid=(n_steps,),
        in_specs=[c_spec(n_p), c_spec(n_s),
                  pl.BlockSpec((None, D_MODEL, cols), lk),
                  pl.BlockSpec((None, 1, cols), lk),
                  *r_in],
        out_specs=[o_spec(n_p), o_spec(n_s), *r_out],
        out_shape=[jax.ShapeDtypeStruct((DEPTH, n_p, N_MOD * D_MODEL), F32),
                   jax.ShapeDtypeStruct((DEPTH, n_s, N_MOD * D_MODEL), F32),
                   *r_shapes],
        name="ada",
    )(c_prompt, c_sample, w_ada, b_ada.reshape(DEPTH, 1, N_MOD * D_MODEL),
      *[src for src, _ in riders])
    return outs[:2], outs[2:]


def _mix_kernel(*refs, tT, nB, x_bm, has_state, n_ride):
    refs = list(refs)
    x_ref, sh_ref, sc_ref, g_ref, ng_ref = _take(refs, 5)
    sth_ref, stlc_ref, stsc_ref = _take(refs, 3) if has_state else (None, None, None)
    win_ref, cw_ref, cb_ref, wg_ref, bg_ref, lam_ref, scw_ref, wout_ref = _take(refs, 8)
    ride_in = _take(refs, n_ride)
    y_ref, nh_ref, nlc_ref, nsc_ref = _take(refs, 4)
    ride_out = _take(refs, n_ride)
    hn_buf, xl_buf, z_buf, a_buf, b_buf, mix_buf, h_car = _take(refs, 7)
    xt_buf = refs[0] if x_bm else None

    R = tT * nB
    C = D_HALF
    j = pl.program_id(1)

    @pl.when(j == 0)
    def _():
        _load_rows(xl_buf, stlc_ref, LRU_CONV - 1, nB)
        _load_rows(z_buf, stsc_ref, SC_CONV - 1, nB)
        h_car[...] = sth_ref[...] if has_state else jnp.zeros((nB, C), F32)

    if x_bm:
        xt_buf[...] = jnp.swapaxes(x_ref[...], 0, 1)
        x_tm = xt_buf
    else:
        x_tm = x_ref
    hn = _modulated_norm(x_tm[...], ng_ref[...], sc_ref[...], sh_ref[...])
    hn_buf[...] = hn.reshape(R, D_MODEL).astype(BF16)

    in_proj = lambda k: _dot(hn_buf[...], win_ref[:, k * C:(k + 1) * C])

    xl_buf[3 * nB:3 * nB + R] = in_proj(0)
    xc = cb_ref[...] + cw_ref[0:1] * xl_buf[0:R]
    for k in range(1, LRU_CONV):
        xc = xc + cw_ref[k:k + 1] * xl_buf[k * nB:k * nB + R]

    neg_c_sp = -RG_C * jnp.logaddexp(-lam_ref[...], 0.0)
    xc_bf = xc.astype(BF16)
    for hf in range(C // MXU_COLS):
        lo, hi = hf * MXU_COLS, (hf + 1) * MXU_COLS
        gates = _dot(xc_bf[:, lo:hi], wg_ref[hf]) + bg_ref[hf]
        r = jax.nn.sigmoid(gates[:, :MXU_COLS])
        ig = jax.nn.sigmoid(gates[:, MXU_COLS:])
        log_a = r * neg_c_sp[:, lo:hi]
        a = jnp.exp(log_a)
        mult = jnp.sqrt(-jnp.tanh(log_a) * (1.0 + a * a))
        a_buf[:, lo:hi] = a
        b_buf[:, lo:hi] = mult * (ig * xc[:, lo:hi])

    h = h_car[...]
    for t in range(tT):
        rows = pl.ds(t * nB, nB)
        h = a_buf[rows] * h + b_buf[rows]
        b_buf[rows] = h
    h_car[...] = h
    nh_ref[...] = h

    mix_buf[:, 0:C] = (b_buf[...] * jax.nn.gelu(in_proj(1))).astype(BF16)

    z_buf[2 * nB:2 * nB + R] = in_proj(3) * in_proj(4)
    zc = scw_ref[0:1] * z_buf[0:R]
    for k in range(1, SC_CONV):
        zc = zc + scw_ref[k:k + 1] * z_buf[k * nB:k * nB + R]
    mix_buf[:, C:2 * C] = (in_proj(2) * zc).astype(BF16)

    mix = _dot(mix_buf[...], wout_ref[...])
    y_ref[...] = x_tm[...] + g_ref[...][None] * mix.reshape(tT, nB, D_MODEL)

    lc_tail = xl_buf[R:R + 3 * nB]
    sc_tail = z_buf[R:R + 2 * nB]
    _store_rows(nlc_ref, lc_tail, LRU_CONV - 1, nB)
    _store_rows(nsc_ref, sc_tail, SC_CONV - 1, nB)
    xl_buf[0:3 * nB] = lc_tail
    z_buf[0:2 * nB] = sc_tail

    _cast_riders(ride_in, ride_out)


def _mix_call(l, x, mod, state, small, big, riders, *, tT, nB, x_bm):
    if x_bm:
        B, T, _ = x.shape
        x_spec = pl.BlockSpec((nB, tT, D_MODEL), lambda i, j: (i, j, 0))
    else:
        T, B, _ = x.shape
        x_spec = pl.BlockSpec((tT, nB, D_MODEL), lambda i, j: (j, i, 0))
    R = tT * nB
    C = D_HALF
    n_steps = T // tT
    assert not riders or B == nB
    const2 = lambda i, j: (l, 0, 0)
    whole = lambda i, j: (0, 0)
    mod_spec = lambda k: pl.BlockSpec((None, nB, D_MODEL), lambda i, j: (l, i, k))
    once = pl.Buffered(1)
    state_specs, state_args = [], []
    if state is not None:
        state_specs = [
            pl.BlockSpec((None, nB, C), lambda i, j: (l, i, 0)),
            pl.BlockSpec((None, nB, LRU_CONV - 1, C), lambda i, j: (l, i, 0, 0)),
            pl.BlockSpec((None, nB, SC_CONV - 1, C), lambda i, j: (l, i, 0, 0)),
        ]
        state_args = list(state)
    r_in, r_out, r_shapes = _rider_specs(riders, n_steps, lambda i, j: j)
    scratch = [
        pltpu.VMEM((R, D_MODEL), BF16),
        pltpu.VMEM((R + 3 * nB, C), F32),
        pltpu.VMEM((R + 2 * nB, C), F32),
        pltpu.VMEM((R, C), F32),
        pltpu.VMEM((R, C), F32),
        pltpu.VMEM((R, D_MODEL), BF16),
        pltpu.VMEM((nB, C), F32),
    ]
    if x_bm:
        scratch.append(pltpu.VMEM((tT, nB, D_MODEL), F32))
    outs = pl.pallas_call(
        functools.partial(_mix_kernel, tT=tT, nB=nB, x_bm=x_bm, has_state=state is not None,
                          n_ride=len(riders)),
        grid=(B // nB, n_steps),
        in_specs=[
            x_spec, mod_spec(0), mod_spec(1), mod_spec(2),
            pl.BlockSpec((None, 1, D_MODEL), const2),
            *state_specs,
            pl.BlockSpec((D_MODEL, 5 * C), whole, pipeline_mode=once),
            pl.BlockSpec((None, LRU_CONV, C), const2),
            pl.BlockSpec((None, 1, C), const2),
            pl.BlockSpec((None, C // MXU_COLS, MXU_COLS, 2 * MXU_COLS),
                         lambda i, j: (l, 0, 0, 0), pipeline_mode=once),
            pl.BlockSpec((None, C // MXU_COLS, 1, 2 * MXU_COLS), lambda i, j: (l, 0, 0, 0)),
            pl.BlockSpec((None, 1, C), const2),
            pl.BlockSpec((None, SC_CONV, C), const2),
            pl.BlockSpec((D_MODEL, D_MODEL), whole, pipeline_mode=once),
            *r_in,
        ],
        out_specs=[
            pl.BlockSpec((tT, nB, D_MODEL), lambda i, j: (j, i, 0)),
            pl.BlockSpec((nB, C), lambda i, j: (i, 0)),
            pl.BlockSpec((nB, LRU_CONV - 1, C), lambda i, j: (i, 0, 0)),
            pl.BlockSpec((nB, SC_CONV - 1, C), lambda i, j: (i, 0, 0)),
            *r_out,
        ],
        out_shape=[
            jax.ShapeDtypeStruct((T, B, D_MODEL), F32),
            jax.ShapeDtypeStruct((B, C), F32),
            jax.ShapeDtypeStruct((B, LRU_CONV - 1, C), F32),
            jax.ShapeDtypeStruct((B, SC_CONV - 1, C), F32),
            *r_shapes,
        ],
        scratch_shapes=scratch,
        compiler_params=pltpu.CompilerParams(
            dimension_semantics=("arbitrary", "arbitrary"),
            vmem_limit_bytes=VMEM_LIMIT_BYTES),
        name="mix",
    )(x, mod, mod, mod, small["norm1_g"], *state_args,
      big["w_in"], small["lru_conv_w"], small["lru_conv_b"], small["w_gate"], small["b_gate"],
      small["lru_lambda"], small["sc_conv_w"], big["w_out"], *[src for src, _ in riders])
    return outs[:4], outs[4:]


def _ffn_kernel(*refs, tT, nB, final, has_state, n_ride):
    refs = list(refs)
    x_ref, sh_ref, sc_ref, g_ref, ng_ref, fg_ref = _take(refs, 6)
    stfc_ref = _take(refs, 1)[0] if has_state else None
    wup_ref, cw_ref, wdn_ref = _take(refs, 3)
    ride_in = _take(refs, n_ride)
    y_ref, nfc_ref = _take(refs, 2)
    ride_out = _take(refs, n_ride)
    hn_buf, u_buf, u_car, act_buf = refs

    R = tT * nB
    j = pl.program_id(1)

    @pl.when(j == 0)
    def _():
        _load_rows(u_car, stfc_ref, FFN_CONV - 1, nB)

    hn = _modulated_norm(x_ref[...], ng_ref[...], sc_ref[...], sh_ref[...])
    hn_buf[...] = hn.reshape(R, D_MODEL).astype(BF16)

    for ci, c0 in enumerate(range(0, D_FF, MXU_COLS)):
        cols = slice(c0, c0 + MXU_COLS)
        ub = u_buf.at[ci % 2]
        ub[0:2 * nB] = u_car[:, cols]
        ub[2 * nB:2 * nB + R] = _dot(hn_buf[...], wup_ref[:, cols])
        v = _dot(hn_buf[...], wup_ref[:, D_FF + c0:D_FF + c0 + MXU_COLS])
        uc = cw_ref[0:1, cols] * ub[0:R]
        for k in range(1, FFN_CONV):
            uc = uc + cw_ref[k:k + 1, cols] * ub[k * nB:k * nB + R]
        act_buf[:, cols] = (jax.nn.gelu(uc) * v).astype(BF16)
        u_car[:, cols] = ub[R:R + 2 * nB]

    out = _dot(act_buf[...], wdn_ref[...])
    xn = x_ref[...] + g_ref[...][None] * out.reshape(tT, nB, D_MODEL)
    if final:
        ms = jnp.mean(xn * xn, axis=-1, keepdims=True)
        xn = xn * jax.lax.rsqrt(ms + EPS) * fg_ref[...][None]
        y_ref[...] = jnp.swapaxes(xn, 0, 1)
    else:
        y_ref[...] = xn

    _store_rows(nfc_ref, u_car[...], FFN_CONV - 1, nB)
    _cast_riders(ride_in, ride_out)


def _ffn_call(l, x, mod, st_fc, small, big, riders, *, tT, nB, final):
    T, B, _ = x.shape
    R = tT * nB
    n_steps = T // tT
    assert not riders or B == nB
    const2 = lambda i, j: (l, 0, 0)
    whole = lambda i, j: (0, 0)
    mod_spec = lambda k: pl.BlockSpec((None, nB, D_MODEL), lambda i, j: (l, i, k))
    once = pl.Buffered(1)
    state_specs, state_args = [], []
    if st_fc is not None:
        state_specs = [pl.BlockSpec((None, nB, FFN_CONV - 1, D_FF), lambda i, j: (l, i, 0, 0))]
        state_args = [st_fc]
    if final:
        y_spec = pl.BlockSpec((nB, tT, D_MODEL), lambda i, j: (i, j, 0))
        y_shape = jax.ShapeDtypeStruct((B, T, D_MODEL), F32)
    else:
        y_spec = pl.BlockSpec((tT, nB, D_MODEL), lambda i, j: (j, i, 0))
        y_shape = jax.ShapeDtypeStruct((T, B, D_MODEL), F32)
    r_in, r_out, r_shapes = _rider_specs(riders, n_steps, lambda i, j: j)
    outs = pl.pallas_call(
        functools.partial(_ffn_kernel, tT=tT, nB=nB, final=final, has_state=st_fc is not None,
                          n_ride=len(riders)),
        grid=(B // nB, n_steps),
        in_specs=[
            pl.BlockSpec((tT, nB, D_MODEL), lambda i, j: (j, i, 0)),
            mod_spec(3), mod_spec(4), mod_spec(5),
            pl.BlockSpec((None, 1, D_MODEL), const2),
            pl.BlockSpec((1, D_MODEL), whole),
            *state_specs,
            pl.BlockSpec((D_MODEL, 2 * D_FF), whole, pipeline_mode=once),
            pl.BlockSpec((None, FFN_CONV, D_FF), const2),
            pl.BlockSpec((D_FF, D_MODEL), whole, pipeline_mode=once),
            *r_in,
        ],
        out_specs=[
            y_spec,
            pl.BlockSpec((nB, FFN_CONV - 1, D_FF), lambda i, j: (i, 0, 0)),
            *r_out,
        ],
        out_shape=[
            y_shape,
            jax.ShapeDtypeStruct((B, FFN_CONV - 1, D_FF), F32),
            *r_shapes,
        ],
        scratch_shapes=[
            pltpu.VMEM((R, D_MODEL), BF16),
            pltpu.VMEM((2, R + 2 * nB, MXU_COLS), F32),
            pltpu.VMEM((2 * nB, D_FF), F32),
            pltpu.VMEM((R, D_FF), BF16),
        ],
        compiler_params=pltpu.CompilerParams(
            dimension_semantics=("arbitrary", "arbitrary"),
            vmem_limit_bytes=VMEM_LIMIT_BYTES),
        name="ffn",
    )(x, mod, mod, mod, small["norm2_g"], small["final_g"], *state_args,
      big["w_up"], small["ffn_conv_w"], big["w_dn"], *[src for src, _ in riders])
    return outs[:2], outs[2:]


def _gate_weights(lru_wa, lru_wx, lru_ba, lru_bx):
    eye = jnp.eye(LRU_HEADS, dtype=lru_wa.dtype)
    dense = lambda w: jnp.einsum("lhij,hk->lhikj", w, eye).reshape(DEPTH, D_HALF, D_HALF)
    wa, wx = dense(lru_wa), dense(lru_wx)
    ws, bs = [], []
    for hf in range(D_HALF // MXU_COLS):
        s = slice(hf * MXU_COLS, (hf + 1) * MXU_COLS)
        ws.append(jnp.concatenate([wa[:, s, s], wx[:, s, s]], axis=-1))
        bs.append(jnp.concatenate([lru_ba[:, s], lru_bx[:, s]], axis=-1)[:, None, :])
    return jnp.stack(ws, axis=1).astype(BF16), jnp.stack(bs, axis=1)


def _run_trunk(x_bm, mod, state, small, big, f32w, *, tT, nB):
    hs, lcs, scs, fcs = [], [], [], []
    x = x_bm
    for l in range(DEPTH):
        riders = [(f32w["w_up"], l), (f32w["w_dn"], l)] if f32w else []
        (x, nh, nlc, nsc), cast = _mix_call(
            l, x, mod, None if state is None else state[:3], small, big[l], riders,
            tT=tT, nB=nB, x_bm=(l == 0))
        if cast:
            big[l]["w_up"], big[l]["w_dn"] = cast
        riders = [(f32w["w_in"], l + 1), (f32w["w_out"], l + 1)] if f32w and l + 1 < DEPTH else []
        (x, nfc), cast = _ffn_call(
            l, x, mod, None if state is None else state[3], small, big[l], riders,
            tT=tT, nB=nB, final=(l == DEPTH - 1))
        if cast:
            big[l + 1]["w_in"], big[l + 1]["w_out"] = cast
        hs.append(nh); lcs.append(nlc); scs.append(nsc); fcs.append(nfc)
    return (x, jnp.stack(hs), jnp.stack(lcs), jnp.stack(scs), jnp.stack(fcs))


def kernel(x_prompt, x_sample, c_prompt, c_sample, state_lru_h, state_lru_conv, state_sc_conv, state_ffn_conv, w_ada, b_ada, norm1_g, norm2_g, w_in, lru_conv_w, lru_conv_b, lru_wa, lru_ba, lru_wx, lru_bx, lru_lambda, sc_conv_w, w_out, ffn_w_up, ffn_conv_w, ffn_w_down, final_g):
    w_gate, b_gate = _gate_weights(lru_wa, lru_wx, lru_ba, lru_bx)
    small = dict(
        norm1_g=norm1_g[:, None, :], norm2_g=norm2_g[:, None, :], final_g=final_g[None, :],
        lru_conv_w=lru_conv_w, lru_conv_b=lru_conv_b[:, None, :],
        w_gate=w_gate, b_gate=b_gate, lru_lambda=lru_lambda[:, None, :],
        sc_conv_w=sc_conv_w, ffn_conv_w=ffn_conv_w)
    f32w = dict(w_in=w_in, w_out=w_out, w_up=ffn_w_up, w_dn=ffn_w_down)

    (mod_p, mod_s), cast = _ada_call(c_prompt, c_sample, w_ada, b_ada,
                                     [(w_in, 0), (w_out, 0)])
    big = [dict() for _ in range(DEPTH)]
    big[0]["w_in"], big[0]["w_out"] = cast

    out_p = _run_trunk(x_prompt, mod_p, None, small, big, f32w,
                       tT=64, nB=x_prompt.shape[0])
    out_s = _run_trunk(x_sample, mod_s,
                       (state_lru_h, state_lru_conv, state_sc_conv, state_ffn_conv),
                       small, big, None, tT=x_sample.shape[1], nB=64)
    return (out_p[0], out_s[0]) + out_p[1:] + out_s[1:]
```

```python
import functools

import jax
import jax.numpy as jnp
from jax.experimental import pallas as pl
from jax.experimental.pallas import tpu as pltpu

D_MODEL = 1024
DEPTH = 2
D_HALF = 512
LRU_HEADS = 8
LRU_HEAD_DIM = 64
LRU_CONV = 4
SC_CONV = 3
FFN_CONV = 3
RG_C = 8.0
D_FF = 2816
EPS = 1e-6
N_MOD = 6

MXU_COLS = 256
BF16_SUBLANES = 16
LANES = 128
VMEM_LIMIT_BYTES = 56 * 1024 * 1024
ADA_COL_BLOCKS = 8

BF16 = jnp.bfloat16
F32 = jnp.float32


def _dot(a, b):
    return jnp.dot(a, b, preferred_element_type=F32)


def _take(refs, n):
    head = refs[:n]
    del refs[:n]
    return head


def _modulated_norm(x3, gain, scale, shift):
    ms = jnp.mean(x3 * x3, axis=-1, keepdims=True)
    y = x3 * jax.lax.rsqrt(ms + EPS)
    y = y * (gain * (1.0 + scale))[None] + shift[None]
    return y.reshape(-1, x3.shape[-1]).astype(BF16)


class _Anchor:
    def __init__(self, lhs_buf, never):
        self.lhs_buf, self.never, self.k = lhs_buf, never, 0
        self.lane_tiles = lhs_buf.shape[1] // LANES

    def tie(self, value):
        fold = value[0:BF16_SUBLANES]
        for r0 in range(BF16_SUBLANES, value.shape[0], BF16_SUBLANES):
            fold = fold + value[r0:r0 + BF16_SUBLANES]
        dep = fold[:, 0:LANES]
        for c0 in range(LANES, fold.shape[1], LANES):
            dep = dep + fold[:, c0:c0 + LANES]
        rg, lt = divmod(self.k, self.lane_tiles)
        self.k += 1
        tile = (slice(rg * BF16_SUBLANES, (rg + 1) * BF16_SUBLANES),
                slice(lt * LANES, (lt + 1) * LANES))
        self.lhs_buf[tile] = jnp.where(self.never, dep.astype(self.lhs_buf.dtype),
                                       self.lhs_buf[tile])


class _NormPipeline:
    PIECES = 8

    def __init__(self, hn_buf, lookahead, make_rows, mod, j, tT, nB):
        self.lookahead, self.tT, self.nB, self.mod, self.make_rows = lookahead, tT, nB, mod, make_rows
        self.rows = hn_buf
        if lookahead:
            @pl.when(j == 0)
            def _():
                self._fill(False, 0, tT)
        else:
            self._fill(False, 0, tT)

    def _fill(self, is_next, t0, t1):
        gain, scale, shift = self.mod
        rows = _modulated_norm(self.make_rows(is_next, t0, t1), gain, scale, shift)
        self.rows[t0 * self.nB:t1 * self.nB] = rows
        return rows

    def prepare_next(self, anchor):
        if self.lookahead:
            step = self.tT // self.PIECES
            for p in range(self.PIECES):
                anchor.tie(self._fill(True, p * step, (p + 1) * step))


def _load_rows(buf, st_ref, n_rows, nB):
    if st_ref is None:
        buf[0:n_rows * nB] = jnp.zeros((n_rows * nB, buf.shape[1]), buf.dtype)
    else:
        for k in range(n_rows):
            buf[k * nB:(k + 1) * nB] = st_ref[:, k, :]


def _store_rows(out_ref, tail, n_rows, nB):
    for k in range(n_rows):
        out_ref[:, k, :] = tail[k * nB:(k + 1) * nB]


def _rider_specs(riders, n_steps, step_of):
    in_specs, out_specs, out_shapes = [], [], []
    for src, layer in riders:
        _, K, N = src.shape
        hold = 1
        while (K * hold) % n_steps or (K * hold // n_steps) % BF16_SUBLANES:
            hold *= 2
            assert hold <= n_steps, (K, n_steps)
        rows = K * hold // n_steps
        in_specs.append(pl.BlockSpec(
            (None, rows, N), lambda *g, l=layer, h=hold: (l, step_of(*g) // h, 0)))
        out_specs.append(pl.BlockSpec((rows, N), lambda *g, h=hold: (step_of(*g) // h, 0)))
        out_shapes.append(jax.ShapeDtypeStruct((K, N), BF16))
    return in_specs, out_specs, out_shapes


def _cast_riders(ride_in, ride_out):
    for src, dst in zip(ride_in, ride_out):
        dst[...] = src[...].astype(BF16)


def _ada_kernel(*refs, n_ride):
    refs = list(refs)
    cp_ref, cs_ref, w_ref, b_ref = _take(refs, 4)
    ride_in = _take(refs, n_ride)
    op_ref, os_ref = _take(refs, 2)
    ride_out = _take(refs, n_ride)
    w = w_ref[...].astype(BF16)
    op_ref[...] = _dot(jax.nn.silu(cp_ref[...]).astype(BF16), w) + b_ref[...]
    os_ref[...] = _dot(jax.nn.silu(cs_ref[...]).astype(BF16), w) + b_ref[...]
    _cast_riders(ride_in, ride_out)


def _ada_call(c_prompt, c_sample, w_ada, b_ada, riders):
    n_p, n_s = c_prompt.shape[0], c_sample.shape[0]
    n_steps = DEPTH * ADA_COL_BLOCKS
    cols = N_MOD * D_MODEL // ADA_COL_BLOCKS
    lk = lambda s: (s // ADA_COL_BLOCKS, 0, s % ADA_COL_BLOCKS)
    c_spec = lambda n: pl.BlockSpec((n, D_MODEL), lambda s: (0, 0))
    o_spec = lambda n: pl.BlockSpec((None, n, cols), lk)
    r_in, r_out, r_shapes = _rider_specs(riders, n_steps, lambda s: s)
    outs = pl.pallas_call(
        functools.partial(_ada_kernel, n_ride=len(riders)),
        grid=(n_steps,),
        in_specs=[c_spec(n_p), c_spec(n_s),
                  pl.BlockSpec((None, D_MODEL, cols), lk),
                  pl.BlockSpec((None, 1, cols), lk),
                  *r_in],
        out_specs=[o_spec(n_p), o_spec(n_s), *r_out],
        out_shape=[jax.ShapeDtypeStruct((DEPTH, n_p, N_MOD * D_MODEL), F32),
                   jax.ShapeDtypeStruct((DEPTH, n_s, N_MOD * D_MODEL), F32),
                   *r_shapes],
        name="ada",
    )(c_prompt, c_sample, w_ada, b_ada.reshape(DEPTH, 1, N_MOD * D_MODEL),
      *[src for src, _ in riders])
    return outs[:2], outs[2:]


def _mix_kernel(*refs, tT, nB, x_bm, has_state, lookahead, n_ride):
    refs = list(refs)
    x_ref = refs.pop(0)
    xn_ref = refs.pop(0) if lookahead else None
    sh_ref, sc_ref, g_ref, ng_ref = _take(refs, 4)
    sth_ref, stlc_ref, stsc_ref = _take(refs, 3) if has_state else (None, None, None)
    win_ref, cw_ref, cb_ref, wg_ref, bg_ref, lam_ref, scw_ref, wout_ref = _take(refs, 8)
    ride_in = _take(refs, n_ride)
    y_ref, nh_ref, nlc_ref, nsc_ref = _take(refs, 4)
    ride_out = _take(refs, n_ride)
    hn_buf, xl_buf, z_buf, a_buf, b_buf, mix_buf, h_car = _take(refs, 7)
    xt_buf = refs[0] if x_bm else None

    R = tT * nB
    C = D_HALF
    j = pl.program_id(1)

    @pl.when(j == 0)
    def _():
        _load_rows(xl_buf, stlc_ref, LRU_CONV - 1, nB)
        _load_rows(z_buf, stsc_ref, SC_CONV - 1, nB)
        h_car[...] = sth_ref[...] if has_state else jnp.zeros((nB, C), F32)

    if x_bm:
        slot = j % 2 if lookahead else 0
        x_tm = xt_buf.at[slot]

        def make_rows(is_next, t0, t1):
            rows = jnp.swapaxes((xn_ref if is_next else x_ref)[:, t0:t1, :], 0, 1)
            (xt_buf.at[1 - slot] if is_next else x_tm)[t0:t1] = rows
            return rows
    else:
        x_tm = x_ref
        make_rows = lambda is_next, t0, t1: (xn_ref if is_next else x_ref)[t0:t1]
    norm = _NormPipeline(hn_buf, lookahead, make_rows,
                         (ng_ref[...], sc_ref[...], sh_ref[...]), j, tT, nB)

    in_proj = lambda k: _dot(norm.rows[...], win_ref[:, k * C:(k + 1) * C])

    xl_buf[3 * nB:3 * nB + R] = in_proj(0)
    xc = cb_ref[...] + cw_ref[0:1] * xl_buf[0:R]
    for k in range(1, LRU_CONV):
        xc = xc + cw_ref[k:k + 1] * xl_buf[k * nB:k * nB + R]

    neg_c_sp = -RG_C * jnp.logaddexp(-lam_ref[...], 0.0)
    xc_bf = xc.astype(BF16)
    for hf in range(C // MXU_COLS):
        lo, hi = hf * MXU_COLS, (hf + 1) * MXU_COLS
        gates = _dot(xc_bf[:, lo:hi], wg_ref[hf]) + bg_ref[hf]
        r = jax.nn.sigmoid(gates[:, :MXU_COLS])
        ig = jax.nn.sigmoid(gates[:, MXU_COLS:])
        log_a = r * neg_c_sp[:, lo:hi]
        a = jnp.exp(log_a)
        mult = jnp.sqrt(-jnp.tanh(log_a) * (1.0 + a * a))
        a_buf[:, lo:hi] = a
        b_buf[:, lo:hi] = mult * (ig * xc[:, lo:hi])

    h = h_car[...]
    for t in range(tT):
        rows = pl.ds(t * nB, nB)
        h = a_buf[rows] * h + b_buf[rows]
        b_buf[rows] = h
    h_car[...] = h
    nh_ref[...] = h

    mix_buf[:, 0:C] = (b_buf[...] * jax.nn.gelu(in_proj(1))).astype(BF16)

    z_buf[2 * nB:2 * nB + R] = in_proj(3) * in_proj(4)
    zc = scw_ref[0:1] * z_buf[0:R]
    for k in range(1, SC_CONV):
        zc = zc + scw_ref[k:k + 1] * z_buf[k * nB:k * nB + R]
    mix_buf[:, C:2 * C] = (in_proj(2) * zc).astype(BF16)

    norm.prepare_next(_Anchor(mix_buf, j < 0))
    mix = _dot(mix_buf[...], wout_ref[...])
    y_ref[...] = x_tm[...] + g_ref[...][None] * mix.reshape(tT, nB, D_MODEL)

    lc_tail = xl_buf[R:R + 3 * nB]
    sc_tail = z_buf[R:R + 2 * nB]
    _store_rows(nlc_ref, lc_tail, LRU_CONV - 1, nB)
    _store_rows(nsc_ref, sc_tail, SC_CONV - 1, nB)
    xl_buf[0:3 * nB] = lc_tail
    z_buf[0:2 * nB] = sc_tail

    _cast_riders(ride_in, ride_out)


def _mix_call(l, x, mod, state, small, big, riders, *, tT, nB, x_bm):
    if x_bm:
        B, T, _ = x.shape
        x_spec = lambda jmap: pl.BlockSpec((nB, tT, D_MODEL), lambda i, j: (i, jmap(j), 0))
    else:
        T, B, _ = x.shape
        x_spec = lambda jmap: pl.BlockSpec((tT, nB, D_MODEL), lambda i, j: (jmap(j), i, 0))
    R = tT * nB
    C = D_HALF
    n_steps = T // tT
    lookahead = n_steps > 1
    x_specs = [x_spec(lambda j: j)]
    if lookahead:
        x_specs.append(x_spec(lambda j: jnp.minimum(j + 1, n_steps - 1)))
    assert not riders or B == nB
    const2 = lambda i, j: (l, 0, 0)
    whole = lambda i, j: (0, 0)
    mod_spec = lambda k: pl.BlockSpec((None, nB, D_MODEL), lambda i, j: (l, i, k))
    once = pl.Buffered(1)
    state_specs, state_args = [], []
    if state is not None:
        state_specs = [
            pl.BlockSpec((None, nB, C), lambda i, j: (l, i, 0)),
            pl.BlockSpec((None, nB, LRU_CONV - 1, C), lambda i, j: (l, i, 0, 0)),
            pl.BlockSpec((None, nB, SC_CONV - 1, C), lambda i, j: (l, i, 0, 0)),
        ]
        state_args = list(state)
    r_in, r_out, r_shapes = _rider_specs(riders, n_steps, lambda i, j: j)
    n_slots = 2 if lookahead else 1
    scratch = [
        pltpu.VMEM((R, D_MODEL), BF16),
        pltpu.VMEM((R + 3 * nB, C), F32),
        pltpu.VMEM((R + 2 * nB, C), F32),
        pltpu.VMEM((R, C), F32),
        pltpu.VMEM((R, C), F32),
        pltpu.VMEM((R, D_MODEL), BF16),
        pltpu.VMEM((nB, C), F32),
    ]
    if x_bm:
        scratch.append(pltpu.VMEM((n_slots, tT, nB, D_MODEL), F32))
    outs = pl.pallas_call(
        functools.partial(_mix_kernel, tT=tT, nB=nB, x_bm=x_bm, has_state=state is not None,
                          lookahead=lookahead, n_ride=len(riders)),
        grid=(B // nB, n_steps),
        in_specs=[
            *x_specs, mod_spec(0), mod_spec(1), mod_spec(2),
            pl.BlockSpec((None, 1, D_MODEL), const2),
            *state_specs,
            pl.BlockSpec((D_MODEL, 5 * C), whole, pipeline_mode=once),
            pl.BlockSpec((None, LRU_CONV, C), const2),
            pl.BlockSpec((None, 1, C), const2),
            pl.BlockSpec((None, C // MXU_COLS, MXU_COLS, 2 * MXU_COLS),
                         lambda i, j: (l, 0, 0, 0), pipeline_mode=once),
            pl.BlockSpec((None, C // MXU_COLS, 1, 2 * MXU_COLS), lambda i, j: (l, 0, 0, 0)),
            pl.BlockSpec((None, 1, C), const2),
            pl.BlockSpec((None, SC_CONV, C), const2),
            pl.BlockSpec((D_MODEL, D_MODEL), whole, pipeline_mode=once),
            *r_in,
        ],
        out_specs=[
            pl.BlockSpec((tT, nB, D_MODEL), lambda i, j: (j, i, 0)),
            pl.BlockSpec((nB, C), lambda i, j: (i, 0)),
            pl.BlockSpec((nB, LRU_CONV - 1, C), lambda i, j: (i, 0, 0)),
            pl.BlockSpec((nB, SC_CONV - 1, C), lambda i, j: (i, 0, 0)),
            *r_out,
        ],
        out_shape=[
            jax.ShapeDtypeStruct((T, B, D_MODEL), F32),
            jax.ShapeDtypeStruct((B, C), F32),
            jax.ShapeDtypeStruct((B, LRU_CONV - 1, C), F32),
            jax.ShapeDtypeStruct((B, SC_CONV - 1, C), F32),
            *r_shapes,
        ],
        scratch_shapes=scratch,
        compiler_params=pltpu.CompilerParams(
            dimension_semantics=("arbitrary", "arbitrary"),
            vmem_limit_bytes=VMEM_LIMIT_BYTES),
        name="mix",
    )(*([x, x] if lookahead else [x]), mod, mod, mod, small["norm1_g"], *state_args,
      big["w_in"], small["lru_conv_w"], small["lru_conv_b"], small["w_gate"], small["b_gate"],
      small["lru_lambda"], small["sc_conv_w"], big["w_out"], *[src for src, _ in riders])
    return outs[:4], outs[4:]


def _ffn_kernel(*refs, tT, nB, final, has_state, lookahead, n_ride):
    refs = list(refs)
    x_ref = refs.pop(0)
    xn_ref = refs.pop(0) if lookahead else None
    sh_ref, sc_ref, g_ref, ng_ref, fg_ref = _take(refs, 5)
    stfc_ref = _take(refs, 1)[0] if has_state else None
    wup_ref, cw_ref, wdn_ref = _take(refs, 3)
    ride_in = _take(refs, n_ride)
    y_ref, nfc_ref = _take(refs, 2)
    ride_out = _take(refs, n_ride)
    hn_buf, u_buf, u_car, act_buf = refs

    R = tT * nB
    j = pl.program_id(1)

    @pl.when(j == 0)
    def _():
        _load_rows(u_car, stfc_ref, FFN_CONV - 1, nB)

    norm = _NormPipeline(
        hn_buf, lookahead, lambda is_next, t0, t1: (xn_ref if is_next else x_ref)[t0:t1],
        (ng_ref[...], sc_ref[...], sh_ref[...]), j, tT, nB)
    hn = norm.rows

    for ci, c0 in enumerate(range(0, D_FF, MXU_COLS)):
        cols = slice(c0, c0 + MXU_COLS)
        ub = u_buf.at[ci % 2]
        ub[0:2 * nB] = u_car[:, cols]
        ub[2 * nB:2 * nB + R] = _dot(hn[...], wup_ref[:, cols])
        v = _dot(hn[...], wup_ref[:, D_FF + c0:D_FF + c0 + MXU_COLS])
        uc = cw_ref[0:1, cols] * ub[0:R]
        for k in range(1, FFN_CONV):
            uc = uc + cw_ref[k:k + 1, cols] * ub[k * nB:k * nB + R]
        act_buf[:, cols] = (jax.nn.gelu(uc) * v).astype(BF16)
        u_car[:, cols] = ub[R:R + 2 * nB]

    norm.prepare_next(_Anchor(act_buf, j < 0))
    out = _dot(act_buf[...], wdn_ref[...])
    xn = x_ref[...] + g_ref[...][None] * out.reshape(tT, nB, D_MODEL)
    if final:
        ms = jnp.mean(xn * xn, axis=-1, keepdims=True)
        xn = xn * jax.lax.rsqrt(ms + EPS) * fg_ref[...][None]
        y_ref[...] = jnp.swapaxes(xn, 0, 1)
    else:
        y_ref[...] = xn

    _store_rows(nfc_ref, u_car[...], FFN_CONV - 1, nB)
    _cast_riders(ride_in, ride_out)


def _ffn_call(l, x, mod, st_fc, small, big, riders, *, tT, nB, final):
    T, B, _ = x.shape
    R = tT * nB
    n_steps = T // tT
    assert not riders or B == nB
    const2 = lambda i, j: (l, 0, 0)
    whole = lambda i, j: (0, 0)
    mod_spec = lambda k: pl.BlockSpec((None, nB, D_MODEL), lambda i, j: (l, i, k))
    once = pl.Buffered(1)
    state_specs, state_args = [], []
    if st_fc is not None:
        state_specs = [pl.BlockSpec((None, nB, FFN_CONV - 1, D_FF), lambda i, j: (l, i, 0, 0))]
        state_args = [st_fc]
    if final:
        y_spec = pl.BlockSpec((nB, tT, D_MODEL), lambda i, j: (i, j, 0))
        y_shape = jax.ShapeDtypeStruct((B, T, D_MODEL), F32)
    else:
        y_spec = pl.BlockSpec((tT, nB, D_MODEL), lambda i, j: (j, i, 0))
        y_shape = jax.ShapeDtypeStruct((T, B, D_MODEL), F32)
    r_in, r_out, r_shapes = _rider_specs(riders, n_steps, lambda i, j: j)
    lookahead = n_steps > 1
    x_specs = [pl.BlockSpec((tT, nB, D_MODEL), lambda i, j: (j, i, 0))]
    if lookahead:
        x_specs.append(pl.BlockSpec((tT, nB, D_MODEL),
                                    lambda i, j: (jnp.minimum(j + 1, n_steps - 1), i, 0)))
    outs = pl.pallas_call(
        functools.partial(_ffn_kernel, tT=tT, nB=nB, final=final, has_state=st_fc is not None,
                          lookahead=lookahead, n_ride=len(riders)),
        grid=(B // nB, n_steps),
        in_specs=[
            *x_specs,
            mod_spec(3), mod_spec(4), mod_spec(5),
            pl.BlockSpec((None, 1, D_MODEL), const2),
            pl.BlockSpec((1, D_MODEL), whole),
            *state_specs,
            pl.BlockSpec((D_MODEL, 2 * D_FF), whole, pipeline_mode=once),
            pl.BlockSpec((None, FFN_CONV, D_FF), const2),
            pl.BlockSpec((D_FF, D_MODEL), whole, pipeline_mode=once),
            *r_in,
        ],
        out_specs=[
            y_spec,
            pl.BlockSpec((nB, FFN_CONV - 1, D_FF), lambda i, j: (i, 0, 0)),
            *r_out,
        ],
        out_shape=[
            y_shape,
            jax.ShapeDtypeStruct((B, FFN_CONV - 1, D_FF), F32),
            *r_shapes,
        ],
        scratch_shapes=[
            pltpu.VMEM((R, D_MODEL), BF16),
            pltpu.VMEM((2, R + 2 * nB, MXU_COLS), F32),
            pltpu.VMEM((2 * nB, D_FF), F32),
            pltpu.VMEM((R, D_FF), BF16),
        ],
        compiler_params=pltpu.CompilerParams(
            dimension_semantics=("arbitrary", "arbitrary"),
            vmem_limit_bytes=VMEM_LIMIT_BYTES),
        name="ffn",
    )(*([x, x] if lookahead else [x]), mod, mod, mod, small["norm2_g"], small["final_g"],
      *state_args, big["w_up"], small["ffn_conv_w"], big["w_dn"], *[src for src, _ in riders])
    return outs[:2], outs[2:]


def _gate_weights(lru_wa, lru_wx, lru_ba, lru_bx):
    eye = jnp.eye(LRU_HEADS, dtype=lru_wa.dtype)
    dense = lambda w: jnp.einsum("lhij,hk->lhikj", w, eye).reshape(DEPTH, D_HALF, D_HALF)
    wa, wx = dense(lru_wa), dense(lru_wx)
    ws, bs = [], []
    for hf in range(D_HALF // MXU_COLS):
        s = slice(hf * MXU_COLS, (hf + 1) * MXU_COLS)
        ws.append(jnp.concatenate([wa[:, s, s], wx[:, s, s]], axis=-1))
        bs.append(jnp.concatenate([lru_ba[:, s], lru_bx[:, s]], axis=-1)[:, None, :])
    return jnp.stack(ws, axis=1).astype(BF16), jnp.stack(bs, axis=1)


def _run_trunk(x_bm, mod, state, small, big, f32w, *, tT, nB):
    hs, lcs, scs, fcs = [], [], [], []
    x = x_bm
    for l in range(DEPTH):
        riders = [(f32w["w_up"], l), (f32w["w_dn"], l)] if f32w else []
        (x, nh, nlc, nsc), cast = _mix_call(
            l, x, mod, None if state is None else state[:3], small, big[l], riders,
            tT=tT, nB=nB, x_bm=(l == 0))
        if cast:
            big[l]["w_up"], big[l]["w_dn"] = cast
        riders = [(f32w["w_in"], l + 1), (f32w["w_out"], l + 1)] if f32w and l + 1 < DEPTH else []
        (x, nfc), cast = _ffn_call(
            l, x, mod, None if state is None else state[3], small, big[l], riders,
            tT=tT, nB=nB, final=(l == DEPTH - 1))
        if cast:
            big[l + 1]["w_in"], big[l + 1]["w_out"] = cast
        hs.append(nh); lcs.append(nlc); scs.append(nsc); fcs.append(nfc)
    return (x, jnp.stack(hs), jnp.stack(lcs), jnp.stack(scs), jnp.stack(fcs))


def kernel(x_prompt, x_sample, c_prompt, c_sample, state_lru_h, state_lru_conv, state_sc_conv, state_ffn_conv, w_ada, b_ada, norm1_g, norm2_g, w_in, lru_conv_w, lru_conv_b, lru_wa, lru_ba, lru_wx, lru_bx, lru_lambda, sc_conv_w, w_out, ffn_w_up, ffn_conv_w, ffn_w_down, final_g):
    w_gate, b_gate = _gate_weights(lru_wa, lru_wx, lru_ba, lru_bx)
    small = dict(
        norm1_g=norm1_g[:, None, :], norm2_g=norm2_g[:, None, :], final_g=final_g[None, :],
        lru_conv_w=lru_conv_w, lru_conv_b=lru_conv_b[:, None, :],
        w_gate=w_gate, b_gate=b_gate, lru_lambda=lru_lambda[:, None, :],
        sc_conv_w=sc_conv_w, ffn_conv_w=ffn_conv_w)
    f32w = dict(w_in=w_in, w_out=w_out, w_up=ffn_w_up, w_dn=ffn_w_down)

    (mod_p, mod_s), cast = _ada_call(c_prompt, c_sample, w_ada, b_ada,
                                     [(w_in, 0), (w_out, 0)])
    big = [dict() for _ in range(DEPTH)]
    big[0]["w_in"], big[0]["w_out"] = cast

    out_p = _run_trunk(x_prompt, mod_p, None, small, big, f32w,
                       tT=64, nB=x_prompt.shape[0])
    out_s = _run_trunk(x_sample, mod_s,
                       (state_lru_h, state_lru_conv, state_sc_conv, state_ffn_conv),
                       small, big, None, tT=x_sample.shape[1], nB=64)
    return (out_p[0], out_s[0]) + out_p[1:] + out_s[1:]
```

```python
import functools

import jax
import jax.numpy as jnp
from jax.experimental import pallas as pl
from jax.experimental.pallas import tpu as pltpu

D_MODEL = 1024
DEPTH = 2
D_HALF = 512
LRU_HEADS = 8
LRU_HEAD_DIM = 64
LRU_CONV = 4
SC_CONV = 3
FFN_CONV = 3
RG_C = 8.0
D_FF = 2816
EPS = 1e-6
N_MOD = 6

MXU_COLS = 256
BF16_SUBLANES = 16
VMEM_LIMIT_BYTES = 56 * 1024 * 1024
ADA_COL_BLOCKS = 8

BF16 = jnp.bfloat16
F32 = jnp.float32

def _dot(a, b):
    return jnp.dot(a, b, preferred_element_type=F32)


def _take(refs, n):
    head = refs[:n]
    del refs[:n]
    return head


def _whole_spec(arr):
    return pl.BlockSpec(arr.shape, lambda *g: (0,) * arr.ndim)


def _modulated_norm(x3, gain, scale, shift):
    ms = jnp.mean(x3 * x3, axis=-1, keepdims=True)
    y = x3 * jax.lax.rsqrt(ms + EPS)
    return y * (gain * (1.0 + scale))[None] + shift[None]


def _load_rows(buf, st_ref, n_rows, nB):
    if st_ref is None:
        buf[0:n_rows * nB] = jnp.zeros((n_rows * nB, buf.shape[1]), buf.dtype)
    else:
        for k in range(n_rows):
            buf[k * nB:(k + 1) * nB] = st_ref[:, k, :]


def _store_rows(out_ref, layer, tail, n_rows, nB):
    for k in range(n_rows):
        out_ref[layer, :, k, :] = tail[k * nB:(k + 1) * nB]


def _keep_previous(prev_refs, out_refs):
    for prev, out in zip(prev_refs, out_refs):
        out[0:prev.shape[0]] = prev[...]


def _rider_specs(riders, n_steps, step_of):
    in_specs, out_specs, out_shapes = [], [], []
    for src, layer in riders:
        _, K, N = src.shape
        hold = 1
        while (K * hold) % n_steps or (K * hold // n_steps) % BF16_SUBLANES:
            hold *= 2
            assert hold <= n_steps, (K, n_steps)
        rows = K * hold // n_steps
        in_specs.append(pl.BlockSpec(
            (None, rows, N), lambda *g, l=layer, h=hold: (l, step_of(*g) // h, 0)))
        out_specs.append(pl.BlockSpec((rows, N), lambda *g, h=hold: (step_of(*g) // h, 0)))
        out_shapes.append(jax.ShapeDtypeStruct((K, N), BF16))
    return in_specs, out_specs, out_shapes


def _cast_riders(ride_in, ride_out):
    for src, dst in zip(ride_in, ride_out):
        dst[...] = src[...].astype(BF16)


def _ada_kernel(*refs, n_ride):
    refs = list(refs)
    cp_ref, cs_ref, w_ref, b_ref = _take(refs, 4)
    ride_in = _take(refs, n_ride)
    op_ref, os_ref = _take(refs, 2)
    ride_out = _take(refs, n_ride)
    w = w_ref[...].astype(BF16)
    b = b_ref[pl.ds(pl.program_id(0) // ADA_COL_BLOCKS, 1), :]
    op_ref[...] = _dot(jax.nn.silu(cp_ref[...]).astype(BF16), w) + b
    os_ref[...] = _dot(jax.nn.silu(cs_ref[...]).astype(BF16), w) + b
    _cast_riders(ride_in, ride_out)


def _ada_call(c_prompt, c_sample, w_ada, b_ada, riders):
    n_p, n_s = c_prompt.shape[0], c_sample.shape[0]
    n_steps = DEPTH * ADA_COL_BLOCKS
    cols = N_MOD * D_MODEL // ADA_COL_BLOCKS
    lk = lambda s: (s // ADA_COL_BLOCKS, 0, s % ADA_COL_BLOCKS)
    c_spec = lambda n: pl.BlockSpec((n, D_MODEL), lambda s: (0, 0))
    o_spec = lambda n: pl.BlockSpec((None, n, cols), lk)
    r_in, r_out, r_shapes = _rider_specs(riders, n_steps, lambda s: s)
    outs = pl.pallas_call(
        functools.partial(_ada_kernel, n_ride=len(riders)),
        grid=(n_steps,),
        in_specs=[c_spec(n_p), c_spec(n_s),
                  pl.BlockSpec((None, D_MODEL, cols), lk),
                  pl.BlockSpec((DEPTH, cols), lambda s: (0, s % ADA_COL_BLOCKS)),
                  *r_in],
        out_specs=[o_spec(n_p), o_spec(n_s), *r_out],
        out_shape=[jax.ShapeDtypeStruct((DEPTH, n_p, N_MOD * D_MODEL), F32),
                   jax.ShapeDtypeStruct((DEPTH, n_s, N_MOD * D_MODEL), F32),
                   *r_shapes],
        name="ada",
    )(c_prompt, c_sample, w_ada, b_ada, *[src for src, _ in riders])
    return outs[:2], outs[2:]


def _mix_kernel(*refs, layer, tT, nB, x_bm, has_state, n_ride):
    refs = list(refs)
    x_ref, sh_ref, sc_ref, g_ref = _take(refs, 4)
    ng_ref, cw_ref, cb_ref, ba_ref, bx_ref, lam_ref, scw_ref = _take(refs, 7)
    sth_ref, stlc_ref, stsc_ref = _take(refs, 3) if has_state else (None, None, None)
    prev = _take(refs, 3) if layer else []
    win_ref, wg_ref, wout_ref = _take(refs, 3)
    ride_in = _take(refs, n_ride)
    y_ref, nh_ref, nlc_ref, nsc_ref = _take(refs, 4)
    ride_out = _take(refs, n_ride)
    hn_buf, xl_buf, z_buf, a_buf, b_buf, mix_buf, h_car = _take(refs, 7)
    xt_buf = refs[0] if x_bm else None

    R = tT * nB
    C = D_HALF
    j = pl.program_id(1)
    row = lambda ref: ref[layer:layer + 1, :]
    tap = lambda ref, k: ref[layer, k:k + 1, :]

    @pl.when(j == 0)
    def _():
        if has_state:
            xl_buf[0:3 * nB] = stlc_ref[...].reshape(3 * nB, C)
            h_car[...] = sth_ref[...]
        else:
            xl_buf[0:3 * nB] = jnp.zeros((3 * nB, C), F32)
            h_car[...] = jnp.zeros((nB, C), F32)
        _load_rows(z_buf, stsc_ref, SC_CONV - 1, nB)

    if x_bm:
        xt_buf[...] = jnp.swapaxes(x_ref[...], 0, 1)
        x_tm = xt_buf
    else:
        x_tm = x_ref
    hn = _modulated_norm(x_tm[...], row(ng_ref), sc_ref[...], sh_ref[...])
    hn_buf[...] = hn.reshape(R, D_MODEL).astype(BF16)

    in_proj = lambda k: _dot(hn_buf[...], win_ref[:, k * C:(k + 1) * C])

    xl_buf[3 * nB:3 * nB + R] = in_proj(0)
    xc = row(cb_ref) + tap(cw_ref, 0) * xl_buf[0:R]
    for k in range(1, LRU_CONV):
        xc = xc + tap(cw_ref, k) * xl_buf[k * nB:k * nB + R]

    neg_c_sp = -RG_C * jnp.logaddexp(-row(lam_ref), 0.0)
    xc_bf = xc.astype(BF16)
    for hf in range(C // MXU_COLS):
        lo, hi = hf * MXU_COLS, (hf + 1) * MXU_COLS
        gates = _dot(xc_bf[:, lo:hi], wg_ref[hf])
        r = jax.nn.sigmoid(gates[:, :MXU_COLS] + row(ba_ref)[:, lo:hi])
        ig = jax.nn.sigmoid(gates[:, MXU_COLS:] + row(bx_ref)[:, lo:hi])
        log_a = r * neg_c_sp[:, lo:hi]
        a = jnp.exp(log_a)
        mult = jnp.sqrt(-jnp.tanh(log_a) * (1.0 + a * a))
        a_buf[:, lo:hi] = a
        b_buf[:, lo:hi] = mult * (ig * xc[:, lo:hi])

    h = h_car[...]
    for t in range(tT):
        rows = pl.ds(t * nB, nB)
        h = a_buf[rows] * h + b_buf[rows]
        b_buf[rows] = h
    h_car[...] = h

    mix_buf[:, 0:C] = (b_buf[...] * jax.nn.gelu(in_proj(1))).astype(BF16)

    z_buf[2 * nB:2 * nB + R] = in_proj(3) * in_proj(4)
    zc = tap(scw_ref, 0) * z_buf[0:R]
    for k in range(1, SC_CONV):
        zc = zc + tap(scw_ref, k) * z_buf[k * nB:k * nB + R]
    mix_buf[:, C:2 * C] = (in_proj(2) * zc).astype(BF16)

    mix = _dot(mix_buf[...], wout_ref[...])
    y_ref[...] = x_tm[...] + g_ref[...][None] * mix.reshape(tT, nB, D_MODEL)

    lc_tail = xl_buf[R:R + 3 * nB]
    sc_tail = z_buf[R:R + 2 * nB]
    _keep_previous(prev, (nh_ref, nlc_ref, nsc_ref))
    nh_ref[layer] = h
    nlc_ref[layer] = lc_tail.reshape(LRU_CONV - 1, nB, C)
    _store_rows(nsc_ref, layer, sc_tail, SC_CONV - 1, nB)
    xl_buf[0:3 * nB] = lc_tail
    z_buf[0:2 * nB] = sc_tail

    _cast_riders(ride_in, ride_out)


_MIX_SMALL = ("norm1_g", "lru_conv_w", "lru_conv_b", "lru_ba", "lru_bx", "lru_lambda", "sc_conv_w")


def _mix_call(l, x, mod, state, prev, small, w_gate, big, riders, *, tT, nB, x_bm):
    if x_bm:
        B, T, _ = x.shape
        x_spec = pl.BlockSpec((nB, tT, D_MODEL), lambda i, j: (i, j, 0))
    else:
        T, B, _ = x.shape
        x_spec = pl.BlockSpec((tT, nB, D_MODEL), lambda i, j: (j, i, 0))
    R = tT * nB
    C = D_HALF
    n_steps = T // tT
    assert not riders or B == nB
    whole = lambda i, j: (0, 0)
    mod_spec = lambda k: pl.BlockSpec((None, nB, D_MODEL), lambda i, j: (l, i, k))
    once = pl.Buffered(1)
    h_spec = lambda n: pl.BlockSpec((n, nB, C), lambda i, j: (0, i, 0))
    lc_spec = lambda n: pl.BlockSpec((n, LRU_CONV - 1, nB, C), lambda i, j: (0, 0, i, 0))
    sc_spec = lambda n: pl.BlockSpec((n, nB, SC_CONV - 1, C), lambda i, j: (0, i, 0, 0))
    state_specs, state_args = [], []
    if state is not None:
        state_specs = [
            pl.BlockSpec((None, nB, C), lambda i, j: (l, i, 0)),
            pl.BlockSpec((None, LRU_CONV - 1, nB, C), lambda i, j: (l, 0, i, 0)),
            pl.BlockSpec((None, nB, SC_CONV - 1, C), lambda i, j: (l, i, 0, 0)),
        ]
        state_args = list(state)
    prev_specs = [h_spec(l), lc_spec(l), sc_spec(l)] if l else []
    r_in, r_out, r_shapes = _rider_specs(riders, n_steps, lambda i, j: j)
    scratch = [
        pltpu.VMEM((R, D_MODEL), BF16),
        pltpu.VMEM((R + 3 * nB, C), F32),
        pltpu.VMEM((R + 2 * nB, C), F32),
        pltpu.VMEM((R, C), F32),
        pltpu.VMEM((R, C), F32),
        pltpu.VMEM((R, D_MODEL), BF16),
        pltpu.VMEM((nB, C), F32),
    ]
    if x_bm:
        scratch.append(pltpu.VMEM((tT, nB, D_MODEL), F32))
    outs = pl.pallas_call(
        functools.partial(_mix_kernel, layer=l, tT=tT, nB=nB, x_bm=x_bm,
                          has_state=state is not None, n_ride=len(riders)),
        grid=(B // nB, n_steps),
        in_specs=[
            x_spec, mod_spec(0), mod_spec(1), mod_spec(2),
            *[_whole_spec(small[name]) for name in _MIX_SMALL],
            *state_specs, *prev_specs,
            pl.BlockSpec((D_MODEL, 5 * C), whole, pipeline_mode=once),
            pl.BlockSpec((None, C // MXU_COLS, MXU_COLS, 2 * MXU_COLS),
                         lambda i, j: (l, 0, 0, 0), pipeline_mode=once),
            pl.BlockSpec((D_MODEL, D_MODEL), whole, pipeline_mode=once),
            *r_in,
        ],
        out_specs=[
            pl.BlockSpec((tT, nB, D_MODEL), lambda i, j: (j, i, 0)),
            h_spec(l + 1), lc_spec(l + 1), sc_spec(l + 1),
            *r_out,
        ],
        out_shape=[
            jax.ShapeDtypeStruct((T, B, D_MODEL), F32),
            jax.ShapeDtypeStruct((l + 1, B, C), F32),
            jax.ShapeDtypeStruct((l + 1, LRU_CONV - 1, B, C), F32),
            jax.ShapeDtypeStruct((l + 1, B, SC_CONV - 1, C), F32),
            *r_shapes,
        ],
        scratch_shapes=scratch,
        compiler_params=pltpu.CompilerParams(
            dimension_semantics=("arbitrary", "arbitrary"),
            vmem_limit_bytes=VMEM_LIMIT_BYTES),
        name="mix",
    )(x, mod, mod, mod, *[small[name] for name in _MIX_SMALL], *state_args, *prev,
      big["w_in"], w_gate, big["w_out"], *[src for src, _ in riders])
    return outs[:4], outs[4:]


def _ffn_kernel(*refs, layer, tT, nB, final, has_state, n_ride):
    refs = list(refs)
    x_ref, sh_ref, sc_ref, g_ref, ng_ref, cw_ref, fg_ref = _take(refs, 7)
    stfc_ref = _take(refs, 1)[0] if has_state else None
    prev = _take(refs, 1) if layer else []
    wup_ref, wdn_ref = _take(refs, 2)
    ride_in = _take(refs, n_ride)
    y_ref, nfc_ref = _take(refs, 2)
    ride_out = _take(refs, n_ride)
    hn_buf, u_buf, u_car, act_buf = refs

    R = tT * nB
    j = pl.program_id(1)

    @pl.when(j == 0)
    def _():
        _load_rows(u_car, stfc_ref, FFN_CONV - 1, nB)

    hn = _modulated_norm(x_ref[...], ng_ref[layer:layer + 1, :], sc_ref[...], sh_ref[...])
    hn_buf[...] = hn.reshape(R, D_MODEL).astype(BF16)

    for ci, c0 in enumerate(range(0, D_FF, MXU_COLS)):
        cols = slice(c0, c0 + MXU_COLS)
        tap = lambda k: cw_ref[layer, k:k + 1, cols]
        ub = u_buf.at[ci % 2]
        ub[0:2 * nB] = u_car[:, cols]
        ub[2 * nB:2 * nB + R] = _dot(hn_buf[...], wup_ref[:, cols])
        v = _dot(hn_buf[...], wup_ref[:, D_FF + c0:D_FF + c0 + MXU_COLS])
        uc = tap(0) * ub[0:R]
        for k in range(1, FFN_CONV):
            uc = uc + tap(k) * ub[k * nB:k * nB + R]
        act_buf[:, cols] = (jax.nn.gelu(uc) * v).astype(BF16)
        u_car[:, cols] = ub[R:R + 2 * nB]

    out = _dot(act_buf[...], wdn_ref[...])
    xn = x_ref[...] + g_ref[...][None] * out.reshape(tT, nB, D_MODEL)
    if final:
        ms = jnp.mean(xn * xn, axis=-1, keepdims=True)
        xn = xn * jax.lax.rsqrt(ms + EPS) * fg_ref[...][None]
        y_ref[...] = jnp.swapaxes(xn, 0, 1)
    else:
        y_ref[...] = xn

    _keep_previous(prev, (nfc_ref,))
    _store_rows(nfc_ref, layer, u_car[...], FFN_CONV - 1, nB)
    _cast_riders(ride_in, ride_out)


def _ffn_call(l, x, mod, st_fc, prev, small, big, riders, *, tT, nB, final):
    T, B, _ = x.shape
    R = tT * nB
    n_steps = T // tT
    assert not riders or B == nB
    whole = lambda i, j: (0, 0)
    mod_spec = lambda k: pl.BlockSpec((None, nB, D_MODEL), lambda i, j: (l, i, k))
    once = pl.Buffered(1)
    fc_spec = lambda n: pl.BlockSpec((n, nB, FFN_CONV - 1, D_FF), lambda i, j: (0, i, 0, 0))
    state_specs, state_args = [], []
    if st_fc is not None:
        state_specs = [pl.BlockSpec((None, nB, FFN_CONV - 1, D_FF), lambda i, j: (l, i, 0, 0))]
        state_args = [st_fc]
    prev_specs = [fc_spec(l)] if l else []
    if final:
        y_spec = pl.BlockSpec((nB, tT, D_MODEL), lambda i, j: (i, j, 0))
        y_shape = jax.ShapeDtypeStruct((B, T, D_MODEL), F32)
    else:
        y_spec = pl.BlockSpec((tT, nB, D_MODEL), lambda i, j: (j, i, 0))
        y_shape = jax.ShapeDtypeStruct((T, B, D_MODEL), F32)
    r_in, r_out, r_shapes = _rider_specs(riders, n_steps, lambda i, j: j)
    outs = pl.pallas_call(
        functools.partial(_ffn_kernel, layer=l, tT=tT, nB=nB, final=final,
                          has_state=st_fc is not None, n_ride=len(riders)),
        grid=(B // nB, n_steps),
        in_specs=[
            pl.BlockSpec((tT, nB, D_MODEL), lambda i, j: (j, i, 0)),
            mod_spec(3), mod_spec(4), mod_spec(5),
            _whole_spec(small["norm2_g"]), _whole_spec(small["ffn_conv_w"]),
            _whole_spec(small["final_g"]),
            *state_specs, *prev_specs,
            pl.BlockSpec((D_MODEL, 2 * D_FF), whole, pipeline_mode=once),
            pl.BlockSpec((D_FF, D_MODEL), whole, pipeline_mode=once),
            *r_in,
        ],
        out_specs=[y_spec, fc_spec(l + 1), *r_out],
        out_shape=[
            y_shape,
            jax.ShapeDtypeStruct((l + 1, B, FFN_CONV - 1, D_FF), F32),
            *r_shapes,
        ],
        scratch_shapes=[
            pltpu.VMEM((R, D_MODEL), BF16),
            pltpu.VMEM((2, R + 2 * nB, MXU_COLS), F32),
            pltpu.VMEM((2 * nB, D_FF), F32),
            pltpu.VMEM((R, D_FF), BF16),
        ],
        compiler_params=pltpu.CompilerParams(
            dimension_semantics=("arbitrary", "arbitrary"),
            vmem_limit_bytes=VMEM_LIMIT_BYTES),
        name="ffn",
    )(x, mod, mod, mod, small["norm2_g"], small["ffn_conv_w"], small["final_g"],
      *state_args, *prev, big["w_up"], big["w_dn"], *[src for src, _ in riders])
    return outs[:2], outs[2:]


def _gate_weights(lru_wa, lru_wx):
    eye = jnp.eye(LRU_HEADS, dtype=lru_wa.dtype)
    dense = lambda w: jnp.einsum("lhij,hk->lhikj", w, eye).reshape(DEPTH, D_HALF, D_HALF)
    wa, wx = dense(lru_wa), dense(lru_wx)
    ws = []
    for hf in range(D_HALF // MXU_COLS):
        s = slice(hf * MXU_COLS, (hf + 1) * MXU_COLS)
        ws.append(jnp.concatenate([wa[:, s, s], wx[:, s, s]], axis=-1))
    return jnp.stack(ws, axis=1).astype(BF16)


def _run_trunk(x_bm, mod, state, small, w_gate, big, f32w, *, tT, nB):
    x = x_bm
    mix_states, fc_states = [], []
    for l in range(DEPTH):
        riders = [(f32w["w_up"], l), (f32w["w_dn"], l)] if f32w else []
        (x, *mix_states), cast = _mix_call(
            l, x, mod, None if state is None else state[:3], mix_states, small, w_gate,
            big[l], riders, tT=tT, nB=nB, x_bm=(l == 0))
        if cast:
            big[l]["w_up"], big[l]["w_dn"] = cast
        riders = [(f32w["w_in"], l + 1), (f32w["w_out"], l + 1)] if f32w and l + 1 < DEPTH else []
        (x, *fc_states), cast = _ffn_call(
            l, x, mod, None if state is None else state[3], fc_states, small,
            big[l], riders, tT=tT, nB=nB, final=(l == DEPTH - 1))
        if cast:
            big[l + 1]["w_in"], big[l + 1]["w_out"] = cast
    nh, nlc, nsc = mix_states
    return (x, nh, jnp.swapaxes(nlc, 1, 2), nsc, fc_states[0])


def kernel(x_prompt, x_sample, c_prompt, c_sample, state_lru_h, state_lru_conv, state_sc_conv, state_ffn_conv, w_ada, b_ada, norm1_g, norm2_g, w_in, lru_conv_w, lru_conv_b, lru_wa, lru_ba, lru_wx, lru_bx, lru_lambda, sc_conv_w, w_out, ffn_w_up, ffn_conv_w, ffn_w_down, final_g):
    small = dict(
        norm1_g=norm1_g, norm2_g=norm2_g, final_g=final_g[None, :], lru_conv_w=lru_conv_w,
        lru_conv_b=lru_conv_b, lru_ba=lru_ba, lru_bx=lru_bx, lru_lambda=lru_lambda,
        sc_conv_w=sc_conv_w, ffn_conv_w=ffn_conv_w)
    w_gate = _gate_weights(lru_wa, lru_wx)
    f32w = dict(w_in=w_in, w_out=w_out, w_up=ffn_w_up, w_dn=ffn_w_down)

    (mod_p, mod_s), cast = _ada_call(c_prompt, c_sample, w_ada, b_ada,
                                     [(w_in, 0), (w_out, 0)])
    big = [dict() for _ in range(DEPTH)]
    big[0]["w_in"], big[0]["w_out"] = cast

    out_p = _run_trunk(x_prompt, mod_p, None, small, w_gate, big, f32w,
                       tT=64, nB=x_prompt.shape[0])
    out_s = _run_trunk(x_sample, mod_s,
                       (state_lru_h, jnp.swapaxes(state_lru_conv, 1, 2), state_sc_conv,
                        state_ffn_conv),
                       small, w_gate, big, None, tT=x_sample.shape[1], nB=64)
    return (out_p[0], out_s[0]) + out_p[1:] + out_s[1:]
```

```python
import functools
from typing import NamedTuple

import jax
import jax.numpy as jnp
from jax.experimental import pallas as pl
from jax.experimental.pallas import tpu as pltpu

D_MODEL = 1024
DEPTH = 2
D_HALF = 512
LRU_HEADS = 8
LRU_HEAD_DIM = 64
LRU_CONV = 4
SC_CONV = 3
FFN_CONV = 3
RG_C = 8.0
D_FF = 2816
EPS = 1e-6
N_MOD = 6

MXU_COLS = 256
BF16_SUBLANES = 16
VMEM_LIMIT_BYTES = 56 * 1024 * 1024
ADA_COL_BLOCKS = 8
TILE_ROWS = 512

BF16 = jnp.bfloat16
F32 = jnp.float32


class _Tiling(NamedTuple):
    tT: int
    nB: int
    n_tiles: int
    first: int
    by_time: bool

    def tile(self, s):
        return jnp.clip(s - self.first, 0, self.n_tiles - 1)

    def t(self, s):
        return self.tile(s) if self.by_time else 0

    def b(self, s):
        return 0 if self.by_time else self.tile(s)


def _dot(a, b):
    return jnp.dot(a, b, preferred_element_type=F32)


def _take(refs, n):
    head = refs[:n]
    del refs[:n]
    return head


def _whole_spec(arr, **kw):
    return pl.BlockSpec(arr.shape, lambda *g: (0,) * arr.ndim, **kw)


def _on_first_tile(j, fn):
    if isinstance(j, int):
        if j == 0:
            fn()
    else:
        pl.when(j == 0)(fn)


def _modulated_norm(x3, gain, scale, shift):
    ms = jnp.mean(x3 * x3, axis=-1, keepdims=True)
    y = x3 * jax.lax.rsqrt(ms + EPS)
    return y * (gain * (1.0 + scale))[None] + shift[None]


def _load_rows(buf, st_ref, n_rows, nB):
    if st_ref is None:
        buf[0:n_rows * nB] = jnp.zeros((n_rows * nB, buf.shape[1]), buf.dtype)
    else:
        for k in range(n_rows):
            buf[k * nB:(k + 1) * nB] = st_ref[:, k, :]


def _store_rows(out_ref, layer, tail, n_rows, nB):
    for k in range(n_rows):
        out_ref[layer, :, k, :] = tail[k * nB:(k + 1) * nB]


def _keep_previous(prev_refs, out_refs):
    for prev, out in zip(prev_refs, out_refs):
        out[0:prev.shape[0]] = prev[...]


def _rider_specs(riders, n_steps, step_of):
    in_specs, out_specs, out_shapes = [], [], []
    for src, layer in riders:
        _, K, N = src.shape
        hold = 1
        while (K * hold) % n_steps or (K * hold // n_steps) % BF16_SUBLANES:
            hold *= 2
            assert hold <= n_steps, (K, n_steps)
        rows = K * hold // n_steps
        in_specs.append(pl.BlockSpec(
            (None, rows, N), lambda *g, l=layer, h=hold: (l, step_of(*g) // h, 0)))
        out_specs.append(pl.BlockSpec((rows, N), lambda *g, h=hold: (step_of(*g) // h, 0)))
        out_shapes.append(jax.ShapeDtypeStruct((K, N), BF16))
    return in_specs, out_specs, out_shapes


def _cast_riders(ride_in, ride_out):
    for src, dst in zip(ride_in, ride_out):
        dst[...] = src[...].astype(BF16)


def _ada_kernel(*refs, n_ride):
    refs = list(refs)
    cp_ref, cs_ref, w_ref, b_ref = _take(refs, 4)
    ride_in = _take(refs, n_ride)
    op_ref, os_ref = _take(refs, 2)
    ride_out = _take(refs, n_ride)
    w = w_ref[...].astype(BF16)
    b = b_ref[pl.ds(pl.program_id(0) // ADA_COL_BLOCKS, 1), :]
    op_ref[...] = _dot(jax.nn.silu(cp_ref[...]).astype(BF16), w) + b
    os_ref[...] = _dot(jax.nn.silu(cs_ref[...]).astype(BF16), w) + b
    _cast_riders(ride_in, ride_out)


def _ada_call(c_prompt, c_sample, w_ada, b_ada, riders):
    n_p, n_s = c_prompt.shape[0], c_sample.shape[0]
    n_steps = DEPTH * ADA_COL_BLOCKS
    cols = N_MOD * D_MODEL // ADA_COL_BLOCKS
    lk = lambda s: (s // ADA_COL_BLOCKS, 0, s % ADA_COL_BLOCKS)
    c_spec = lambda n: pl.BlockSpec((n, D_MODEL), lambda s: (0, 0))
    o_spec = lambda n: pl.BlockSpec((None, n, cols), lk)
    r_in, r_out, r_shapes = _rider_specs(riders, n_steps, lambda s: s)
    outs = pl.pallas_call(
        functools.partial(_ada_kernel, n_ride=len(riders)),
        grid=(n_steps,),
        in_specs=[c_spec(n_p), c_spec(n_s),
                  pl.BlockSpec((None, D_MODEL, cols), lk),
                  pl.BlockSpec((DEPTH, cols), lambda s: (0, s % ADA_COL_BLOCKS)),
                  *r_in],
        out_specs=[o_spec(n_p), o_spec(n_s), *r_out],
        out_shape=[jax.ShapeDtypeStruct((DEPTH, n_p, N_MOD * D_MODEL), F32),
                   jax.ShapeDtypeStruct((DEPTH, n_s, N_MOD * D_MODEL), F32),
                   *r_shapes],
        name="ada",
    )(c_prompt, c_sample, w_ada, b_ada, *[src for src, _ in riders])
    return outs[:2], outs[2:]


_MIX_SMALL = ("norm1_g", "lru_conv_w", "lru_conv_b", "lru_ba", "lru_bx", "lru_lambda", "sc_conv_w")


def _mix_tile(j, tl, layer, x_bm, x_ref, sh_ref, sc_ref, g_ref, state, prev, small, weights,
              outs, scratch, xt_buf):
    tT, nB = tl.tT, tl.nB
    ng_ref, cw_ref, cb_ref, ba_ref, bx_ref, lam_ref, scw_ref = small
    win_ref, wg_ref, wout_ref = weights
    y_ref, nh_ref, nlc_ref, nsc_ref = outs
    hn_buf, xl_buf, z_buf, a_buf, b_buf, mix_buf, h_car = scratch
    sth_ref, stlc_ref, stsc_ref = state if state else (None, None, None)

    R = tT * nB
    C = D_HALF
    row = lambda ref: ref[layer:layer + 1, :]
    tap = lambda ref, k: ref[layer, k:k + 1, :]

    def init():
        if state:
            xl_buf[0:3 * nB] = stlc_ref[...].reshape(3 * nB, C)
            h_car[0:nB] = sth_ref[...]
        else:
            xl_buf[0:3 * nB] = jnp.zeros((3 * nB, C), F32)
            h_car[0:nB] = jnp.zeros((nB, C), F32)
        _load_rows(z_buf, stsc_ref, SC_CONV - 1, nB)
    _on_first_tile(j, init)

    if x_bm:
        xt_buf[...] = jnp.swapaxes(x_ref[...], 0, 1)
        x_tm = xt_buf
    else:
        x_tm = x_ref
    hn = _modulated_norm(x_tm[...], row(ng_ref), sc_ref[...], sh_ref[...])
    hn_buf[...] = hn.reshape(R, D_MODEL).astype(BF16)

    in_proj = lambda k: _dot(hn_buf[...], win_ref[:, k * C:(k + 1) * C])

    xl_buf[3 * nB:3 * nB + R] = in_proj(0)
    xc = row(cb_ref) + tap(cw_ref, 0) * xl_buf[0:R]
    for k in range(1, LRU_CONV):
        xc = xc + tap(cw_ref, k) * xl_buf[k * nB:k * nB + R]

    neg_c_sp = -RG_C * jnp.logaddexp(-row(lam_ref), 0.0)
    xc_bf = xc.astype(BF16)
    for hf in range(C // MXU_COLS):
        lo, hi = hf * MXU_COLS, (hf + 1) * MXU_COLS
        gates = _dot(xc_bf[:, lo:hi], wg_ref[hf])
        r = jax.nn.sigmoid(gates[:, :MXU_COLS] + row(ba_ref)[:, lo:hi])
        ig = jax.nn.sigmoid(gates[:, MXU_COLS:] + row(bx_ref)[:, lo:hi])
        log_a = r * neg_c_sp[:, lo:hi]
        a = jnp.exp(log_a)
        mult = jnp.sqrt(-jnp.tanh(log_a) * (1.0 + a * a))
        a_buf[:, lo:hi] = a
        b_buf[:, lo:hi] = mult * (ig * xc[:, lo:hi])

    h = h_car[0:nB]
    for t in range(tT):
        rows = pl.ds(t * nB, nB)
        h = a_buf[rows] * h + b_buf[rows]
        b_buf[rows] = h
    h_car[0:nB] = h

    mix_buf[:, 0:C] = (b_buf[...] * jax.nn.gelu(in_proj(1))).astype(BF16)

    z_buf[2 * nB:2 * nB + R] = in_proj(3) * in_proj(4)
    zc = tap(scw_ref, 0) * z_buf[0:R]
    for k in range(1, SC_CONV):
        zc = zc + tap(scw_ref, k) * z_buf[k * nB:k * nB + R]
    mix_buf[:, C:2 * C] = (in_proj(2) * zc).astype(BF16)

    mix = _dot(mix_buf[...], wout_ref[...])
    y_ref[...] = x_tm[...] + g_ref[...][None] * mix.reshape(tT, nB, D_MODEL)

    lc_tail = xl_buf[R:R + 3 * nB]
    sc_tail = z_buf[R:R + 2 * nB]
    _keep_previous(prev, (nh_ref, nlc_ref, nsc_ref))
    nh_ref[layer] = h
    nlc_ref[layer] = lc_tail.reshape(LRU_CONV - 1, nB, C)
    _store_rows(nsc_ref, layer, sc_tail, SC_CONV - 1, nB)
    xl_buf[0:3 * nB] = lc_tail
    z_buf[0:2 * nB] = sc_tail


def _mix_kernel(*refs, layer, tilings, x_bm, has_state, n_ride):
    refs = list(refs)
    small = _take(refs, len(_MIX_SMALL))
    weights = _take(refs, 3)
    ins = []
    for st in has_state:
        ins.append((_take(refs, 4), _take(refs, 3) if st else None,
                    _take(refs, 3) if layer else []))
    ride_in = _take(refs, n_ride)
    outs = [_take(refs, 4) for _ in tilings]
    ride_out = _take(refs, n_ride)
    scratch = _take(refs, 7)
    xt_bufs = [refs.pop(0) if x_bm else None for _ in tilings]

    s = pl.program_id(0)
    for k, tl in enumerate(tilings):
        (x_ref, sh_ref, sc_ref, g_ref), state, prev = ins[k]

        @pl.when((s >= tl.first) & (s < tl.first + tl.n_tiles))
        def _(k=k, tl=tl, x_ref=x_ref, sh_ref=sh_ref, sc_ref=sc_ref, g_ref=g_ref,
              state=state, prev=prev):
            _mix_tile(s - tl.first if tl.by_time else 0, tl, layer, x_bm, x_ref, sh_ref,
                      sc_ref, g_ref, state, prev, small, weights, outs[k], scratch, xt_bufs[k])
            if tl.by_time:
                _cast_riders(ride_in, ride_out)


def _trunk_specs(tl, l, x_bm_in, x_bm_out, state_dims, prev_dims):
    def act_spec(bm):
        if bm:
            return pl.BlockSpec((tl.nB, tl.tT, D_MODEL), lambda s: (tl.b(s), tl.t(s), 0))
        return pl.BlockSpec((tl.tT, tl.nB, D_MODEL), lambda s: (tl.t(s), tl.b(s), 0))

    mod_spec = lambda k: pl.BlockSpec((None, tl.nB, D_MODEL), lambda s: (l, tl.b(s), k))

    def state_spec(dims, lead):
        shape = tuple(tl.nB if d == "B" else d for d in dims)
        pos = dims.index("B")
        def index(s):
            idx = [0] * len(dims)
            idx[pos] = tl.b(s)
            return (l if lead is None else 0, *idx)
        return pl.BlockSpec((lead, *shape), index)

    return act_spec(x_bm_in), act_spec(x_bm_out), mod_spec, state_spec


_MIX_STATE_DIMS = (("B", D_HALF), (LRU_CONV - 1, "B", D_HALF), ("B", SC_CONV - 1, D_HALF))


def _state_shape(dims, lead, B):
    return (lead, *(B if d == "B" else d for d in dims))


def _mix_call(l, trunks, small, w_gate, big, riders):
    x_bm = l == 0
    C = D_HALF
    tilings = tuple(t[0] for t in trunks)
    n_grid = sum(tl.n_tiles for tl in tilings)
    once = pl.Buffered(1)
    in_specs = [*[_whole_spec(small[name]) for name in _MIX_SMALL],
                _whole_spec(big["w_in"], pipeline_mode=once),
                pl.BlockSpec((None, C // MXU_COLS, MXU_COLS, 2 * MXU_COLS),
                             lambda s: (l, 0, 0, 0), pipeline_mode=once),
                _whole_spec(big["w_out"], pipeline_mode=once)]
    args = [*[small[name] for name in _MIX_SMALL], big["w_in"], w_gate, big["w_out"]]
    out_specs, out_shapes = [], []
    for tl, x, mod, state, prev in trunks:
        B = tl.nB if tl.by_time else tl.nB * tl.n_tiles
        T = tl.tT * tl.n_tiles if tl.by_time else tl.tT
        x_spec, y_spec, mod_spec, state_spec = _trunk_specs(tl, l, x_bm, False, None, None)
        in_specs += [x_spec, mod_spec(0), mod_spec(1), mod_spec(2)]
        args += [x, mod, mod, mod]
        if state is not None:
            in_specs += [state_spec(d, None) for d in _MIX_STATE_DIMS]
            args += list(state)
        if l:
            in_specs += [state_spec(d, l) for d in _MIX_STATE_DIMS]
            args += list(prev)
        out_specs += [y_spec, *[state_spec(d, l + 1) for d in _MIX_STATE_DIMS]]
        out_shapes += [jax.ShapeDtypeStruct((T, B, D_MODEL), F32),
                       *[jax.ShapeDtypeStruct(_state_shape(d, l + 1, B), F32)
                         for d in _MIX_STATE_DIMS]]
    riding = tilings[0]
    assert riding.by_time and all(tl.tT * tl.nB == TILE_ROWS for tl in tilings)
    r_in, r_out, r_shapes = _rider_specs(riders, riding.n_tiles, riding.tile)
    R = TILE_ROWS
    max_nB = max(tl.nB for tl in tilings)
    scratch = [
        pltpu.VMEM((R, D_MODEL), BF16),
        pltpu.VMEM((R + (LRU_CONV - 1) * max_nB, C), F32),
        pltpu.VMEM((R + (SC_CONV - 1) * max_nB, C), F32),
        pltpu.VMEM((R, C), F32),
        pltpu.VMEM((R, C), F32),
        pltpu.VMEM((R, D_MODEL), BF16),
        pltpu.VMEM((max_nB, C), F32),
    ]
    if x_bm:
        scratch += [pltpu.VMEM((tl.tT, tl.nB, D_MODEL), F32) for tl in tilings]
    outs = pl.pallas_call(
        functools.partial(_mix_kernel, layer=l, tilings=tilings, x_bm=x_bm,
                          has_state=tuple(t[3] is not None for t in trunks),
                          n_ride=len(riders)),
        grid=(n_grid,),
        in_specs=[*in_specs, *r_in],
        out_specs=[*out_specs, *r_out],
        out_shape=[*out_shapes, *r_shapes],
        scratch_shapes=scratch,
        compiler_params=pltpu.CompilerParams(
            dimension_semantics=("arbitrary",), vmem_limit_bytes=VMEM_LIMIT_BYTES),
        name="mix",
    )(*args, *[src for src, _ in riders])
    n = 4 * len(trunks)
    return [outs[4 * k:4 * k + 4] for k in range(len(trunks))], outs[n:]


def _ffn_tile(j, tl, layer, final, x_ref, sh_ref, sc_ref, g_ref, stfc_ref, prev, small,
              weights, outs, scratch):
    tT, nB = tl.tT, tl.nB
    ng_ref, cw_ref, fg_ref = small
    wup_ref, wdn_ref = weights
    y_ref, nfc_ref = outs
    hn_buf, u_buf, u_car, act_buf = scratch
    R = tT * nB

    _on_first_tile(j, lambda: _load_rows(u_car, stfc_ref, FFN_CONV - 1, nB))

    hn = _modulated_norm(x_ref[...], ng_ref[layer:layer + 1, :], sc_ref[...], sh_ref[...])
    hn_buf[...] = hn.reshape(R, D_MODEL).astype(BF16)

    for ci, c0 in enumerate(range(0, D_FF, MXU_COLS)):
        cols = slice(c0, c0 + MXU_COLS)
        tap = lambda k: cw_ref[layer, k:k + 1, cols]
        ub = u_buf.at[ci % 2]
        ub[0:2 * nB] = u_car[0:2 * nB, cols]
        ub[2 * nB:2 * nB + R] = _dot(hn_buf[...], wup_ref[:, cols])
        v = _dot(hn_buf[...], wup_ref[:, D_FF + c0:D_FF + c0 + MXU_COLS])
        uc = tap(0) * ub[0:R]
        for k in range(1, FFN_CONV):
            uc = uc + tap(k) * ub[k * nB:k * nB + R]
        act_buf[:, cols] = (jax.nn.gelu(uc) * v).astype(BF16)
        u_car[0:2 * nB, cols] = ub[R:R + 2 * nB]

    out = _dot(act_buf[...], wdn_ref[...])
    xn = x_ref[...] + g_ref[...][None] * out.reshape(tT, nB, D_MODEL)
    if final:
        ms = jnp.mean(xn * xn, axis=-1, keepdims=True)
        xn = xn * jax.lax.rsqrt(ms + EPS) * fg_ref[...][None]
        y_ref[...] = jnp.swapaxes(xn, 0, 1)
    else:
        y_ref[...] = xn

    _keep_previous(prev, (nfc_ref,))
    _store_rows(nfc_ref, layer, u_car[0:2 * nB], FFN_CONV - 1, nB)


def _ffn_kernel(*refs, layer, tilings, final, has_state, n_ride):
    refs = list(refs)
    small = _take(refs, 3)
    weights = _take(refs, 2)
    ins = []
    for st in has_state:
        ins.append((_take(refs, 4), refs.pop(0) if st else None,
                    _take(refs, 1) if layer else []))
    ride_in = _take(refs, n_ride)
    outs = [_take(refs, 2) for _ in tilings]
    ride_out = _take(refs, n_ride)
    scratch = refs

    s = pl.program_id(0)
    for k, tl in enumerate(tilings):
        (x_ref, sh_ref, sc_ref, g_ref), stfc_ref, prev = ins[k]

        @pl.when((s >= tl.first) & (s < tl.first + tl.n_tiles))
        def _(k=k, tl=tl, x_ref=x_ref, sh_ref=sh_ref, sc_ref=sc_ref, g_ref=g_ref,
              stfc_ref=stfc_ref, prev=prev):
            _ffn_tile(s - tl.first if tl.by_time else 0, tl, layer, final, x_ref, sh_ref,
                      sc_ref, g_ref, stfc_ref, prev, small, weights, outs[k], scratch)
            if tl.by_time:
                _cast_riders(ride_in, ride_out)


_FFN_STATE_DIMS = ("B", FFN_CONV - 1, D_FF)


def _ffn_call(l, trunks, small, big, riders):
    final = l == DEPTH - 1
    tilings = tuple(t[0] for t in trunks)
    n_grid = sum(tl.n_tiles for tl in tilings)
    once = pl.Buffered(1)
    small_args = [small["norm2_g"], small["ffn_conv_w"], small["final_g"]]
    in_specs = [*[_whole_spec(a) for a in small_args],
                _whole_spec(big["w_up"], pipeline_mode=once),
                _whole_spec(big["w_dn"], pipeline_mode=once)]
    args = [*small_args, big["w_up"], big["w_dn"]]
    out_specs, out_shapes = [], []
    for tl, x, mod, st_fc, prev in trunks:
        B = tl.nB if tl.by_time else tl.nB * tl.n_tiles
        T = tl.tT * tl.n_tiles if tl.by_time else tl.tT
        x_spec, y_spec, mod_spec, state_spec = _trunk_specs(tl, l, False, final, None, None)
        in_specs += [x_spec, mod_spec(3), mod_spec(4), mod_spec(5)]
        args += [x, mod, mod, mod]
        if st_fc is not None:
            in_specs.append(state_spec(_FFN_STATE_DIMS, None))
            args.append(st_fc)
        if l:
            in_specs.append(state_spec(_FFN_STATE_DIMS, l))
            args += list(prev)
        out_specs += [y_spec, state_spec(_FFN_STATE_DIMS, l + 1)]
        out_shapes += [jax.ShapeDtypeStruct((B, T, D_MODEL) if final else (T, B, D_MODEL), F32),
                       jax.ShapeDtypeStruct(_state_shape(_FFN_STATE_DIMS, l + 1, B), F32)]
    riding = tilings[0]
    assert riding.by_time and all(tl.tT * tl.nB == TILE_ROWS for tl in tilings)
    r_in, r_out, r_shapes = _rider_specs(riders, riding.n_tiles, riding.tile)
    R = TILE_ROWS
    max_nB = max(tl.nB for tl in tilings)
    outs = pl.pallas_call(
        functools.partial(_ffn_kernel, layer=l, tilings=tilings, final=final,
                          has_state=tuple(t[3] is not None for t in trunks),
                          n_ride=len(riders)),
        grid=(n_grid,),
        in_specs=[*in_specs, *r_in],
        out_specs=[*out_specs, *r_out],
        out_shape=[*out_shapes, *r_shapes],
        scratch_shapes=[
            pltpu.VMEM((R, D_MODEL), BF16),
            pltpu.VMEM((2, R + (FFN_CONV - 1) * max_nB, MXU_COLS), F32),
            pltpu.VMEM(((FFN_CONV - 1) * max_nB, D_FF), F32),
            pltpu.VMEM((R, D_FF), BF16),
        ],
        compiler_params=pltpu.CompilerParams(
            dimension_semantics=("arbitrary",), vmem_limit_bytes=VMEM_LIMIT_BYTES),
        name="ffn",
    )(*args, *[src for src, _ in riders])
    n = 2 * len(trunks)
    return [outs[2 * k:2 * k + 2] for k in range(len(trunks))], outs[n:]


def _gate_weights(lru_wa, lru_wx):
    eye = jnp.eye(LRU_HEADS, dtype=lru_wa.dtype)
    dense = lambda w: jnp.einsum("lhij,hk->lhikj", w, eye).reshape(DEPTH, D_HALF, D_HALF)
    wa, wx = dense(lru_wa), dense(lru_wx)
    ws = []
    for hf in range(D_HALF // MXU_COLS):
        s = slice(hf * MXU_COLS, (hf + 1) * MXU_COLS)
        ws.append(jnp.concatenate([wa[:, s, s], wx[:, s, s]], axis=-1))
    return jnp.stack(ws, axis=1).astype(BF16)


def kernel(x_prompt, x_sample, c_prompt, c_sample, state_lru_h, state_lru_conv, state_sc_conv, state_ffn_conv, w_ada, b_ada, norm1_g, norm2_g, w_in, lru_conv_w, lru_conv_b, lru_wa, lru_ba, lru_wx, lru_bx, lru_lambda, sc_conv_w, w_out, ffn_w_up, ffn_conv_w, ffn_w_down, final_g):
    small = dict(
        norm1_g=norm1_g, norm2_g=norm2_g, final_g=final_g[None, :], lru_conv_w=lru_conv_w,
        lru_conv_b=lru_conv_b, lru_ba=lru_ba, lru_bx=lru_bx, lru_lambda=lru_lambda,
        sc_conv_w=sc_conv_w, ffn_conv_w=ffn_conv_w)
    w_gate = _gate_weights(lru_wa, lru_wx)

    (mod_p, mod_s), cast = _ada_call(c_prompt, c_sample, w_ada, b_ada,
                                     [(w_in, 0), (w_out, 0)])
    big = [dict() for _ in range(DEPTH)]
    big[0]["w_in"], big[0]["w_out"] = cast

    (b_p, t_p), (b_s, t_s) = x_prompt.shape[:2], x_sample.shape[:2]
    til_p = _Tiling(tT=TILE_ROWS // b_p, nB=b_p, n_tiles=t_p * b_p // TILE_ROWS, first=0,
                    by_time=True)
    til_s = _Tiling(tT=t_s, nB=TILE_ROWS // t_s, n_tiles=b_s * t_s // TILE_ROWS,
                    first=til_p.n_tiles, by_time=False)
    xs = [x_prompt, x_sample]
    mods = [mod_p, mod_s]
    mix_in = [None, (state_lru_h, jnp.swapaxes(state_lru_conv, 1, 2), state_sc_conv)]
    ffn_in = [None, state_ffn_conv]
    mix_new, ffn_new = [[], []], [[], []]
    for l in range(DEPTH):
        res, cast = _mix_call(
            l, [(tl, x, mod, st, prev) for tl, x, mod, st, prev
                in zip((til_p, til_s), xs, mods, mix_in, mix_new)],
            small, w_gate, big[l], [(ffn_w_up, l), (ffn_w_down, l)])
        xs = [r[0] for r in res]
        mix_new = [r[1:] for r in res]
        big[l]["w_up"], big[l]["w_dn"] = cast
        res, cast = _ffn_call(
            l, [(tl, x, mod, st, prev) for tl, x, mod, st, prev
                in zip((til_p, til_s), xs, mods, ffn_in, ffn_new)],
            small, big[l], [(w_in, l + 1), (w_out, l + 1)] if l + 1 < DEPTH else [])
        xs = [r[0] for r in res]
        ffn_new = [r[1:] for r in res]
        if cast:
            big[l + 1]["w_in"], big[l + 1]["w_out"] = cast
    states = [(nh, jnp.swapaxes(nlc, 1, 2), nsc, nfc)
              for (nh, nlc, nsc), (nfc,) in zip(mix_new, ffn_new)]
    return (xs[0], xs[1]) + states[0] + states[1]
```

```python
import functools

import jax
import jax.numpy as jnp
from jax.experimental import pallas as pl
from jax.experimental.pallas import tpu as pltpu

D_MODEL = 1024
DEPTH = 2
D_HALF = 512
LRU_HEADS = 8
LRU_HEAD_DIM = 64
LRU_CONV = 4
SC_CONV = 3
FFN_CONV = 3
RG_C = 8.0
D_FF = 2816
EPS = 1e-6
N_MOD = 6

MXU_COLS = 256
BF16_SUBLANES = 16
VMEM_LIMIT_BYTES = 56 * 1024 * 1024
ADA_COL_BLOCKS = 4

BF16 = jnp.bfloat16
F32 = jnp.float32


def _dot(a, b):
    return jnp.dot(a, b, preferred_element_type=F32)


def _take(refs, n):
    head = refs[:n]
    del refs[:n]
    return head


def _whole_spec(arr):
    return pl.BlockSpec(arr.shape, lambda *g: (0,) * arr.ndim)


def _modulated_norm(x3, gain, scale, shift):
    ms = jnp.mean(x3 * x3, axis=-1, keepdims=True)
    y = x3 * jax.lax.rsqrt(ms + EPS)
    return y * (gain * (1.0 + scale))[None] + shift[None]


def _load_rows(buf, st_ref, n_rows, nB):
    if st_ref is None:
        buf[0:n_rows * nB] = jnp.zeros((n_rows * nB, buf.shape[1]), buf.dtype)
    else:
        for k in range(n_rows):
            buf[k * nB:(k + 1) * nB] = st_ref[:, k, :]


def _store_rows(out_ref, layer, tail, n_rows, nB):
    for k in range(n_rows):
        out_ref[layer, :, k, :] = tail[k * nB:(k + 1) * nB]


def _keep_previous(prev_refs, out_refs):
    for prev, out in zip(prev_refs, out_refs):
        out[0:prev.shape[0]] = prev[...]


def _rider_specs(riders, n_steps, step_of):
    in_specs, out_specs, out_shapes = [], [], []
    for src, layer in riders:
        _, K, N = src.shape
        hold = 1
        while (K * hold) % n_steps or (K * hold // n_steps) % BF16_SUBLANES:
            hold *= 2
            assert hold <= n_steps, (K, n_steps)
        rows = K * hold // n_steps
        in_specs.append(pl.BlockSpec(
            (None, rows, N), lambda *g, l=layer, h=hold: (l, step_of(*g) // h, 0)))
        out_specs.append(pl.BlockSpec((rows, N), lambda *g, h=hold: (step_of(*g) // h, 0)))
        out_shapes.append(jax.ShapeDtypeStruct((K, N), BF16))
    return in_specs, out_specs, out_shapes


def _cast_riders(ride_in, ride_out):
    for src, dst in zip(ride_in, ride_out):
        dst[...] = src[...].astype(BF16)


def _ada_kernel(*refs, n_ride):
    refs = list(refs)
    cp_ref, cs_ref, w_ref, b_ref = _take(refs, 4)
    ride_in = _take(refs, n_ride)
    op_ref, os_ref = _take(refs, 2)
    ride_out = _take(refs, n_ride)
    w = w_ref[...].astype(BF16)
    b = b_ref[pl.ds(pl.program_id(0) // ADA_COL_BLOCKS, 1), :]
    op_ref[...] = _dot(jax.nn.silu(cp_ref[...]).astype(BF16), w) + b
    os_ref[...] = _dot(jax.nn.silu(cs_ref[...]).astype(BF16), w) + b
    _cast_riders(ride_in, ride_out)


def _ada_call(c_prompt, c_sample, w_ada, b_ada, riders):
    n_p, n_s = c_prompt.shape[0], c_sample.shape[0]
    n_steps = DEPTH * ADA_COL_BLOCKS
    cols = N_MOD * D_MODEL // ADA_COL_BLOCKS
    lk = lambda s: (s // ADA_COL_BLOCKS, 0, s % ADA_COL_BLOCKS)
    c_spec = lambda n: pl.BlockSpec((n, D_MODEL), lambda s: (0, 0))
    o_spec = lambda n: pl.BlockSpec((None, n, cols), lk)
    r_in, r_out, r_shapes = _rider_specs(riders, n_steps, lambda s: s)
    outs = pl.pallas_call(
        functools.partial(_ada_kernel, n_ride=len(riders)),
        grid=(n_steps,),
        in_specs=[c_spec(n_p), c_spec(n_s),
                  pl.BlockSpec((None, D_MODEL, cols), lk),
                  pl.BlockSpec((DEPTH, cols), lambda s: (0, s % ADA_COL_BLOCKS)),
                  *r_in],
        out_specs=[o_spec(n_p), o_spec(n_s), *r_out],
        out_shape=[jax.ShapeDtypeStruct((DEPTH, n_p, N_MOD * D_MODEL), F32),
                   jax.ShapeDtypeStruct((DEPTH, n_s, N_MOD * D_MODEL), F32),
                   *r_shapes],
        name="ada",
    )(c_prompt, c_sample, w_ada, b_ada, *[src for src, _ in riders])
    return outs[:2], outs[2:]


def _mix_kernel(*refs, layer, tT, nB, x_bm, has_state, n_ride):
    refs = list(refs)
    x_ref, sh_ref, sc_ref, g_ref = _take(refs, 4)
    ng_ref, cw_ref, cb_ref, ba_ref, bx_ref, lam_ref, scw_ref = _take(refs, 7)
    sth_ref, stlc_ref, stsc_ref = _take(refs, 3) if has_state else (None, None, None)
    prev = _take(refs, 3) if layer else []
    win_ref, wg_ref, wout_ref = _take(refs, 3)
    ride_in = _take(refs, n_ride)
    y_ref, nh_ref, nlc_ref, nsc_ref = _take(refs, 4)
    ride_out = _take(refs, n_ride)
    hn_buf, xl_buf, z_buf, a_buf, b_buf, mix_buf, h_car = _take(refs, 7)
    xt_buf = refs[0] if x_bm else None

    R = tT * nB
    C = D_HALF
    j = pl.program_id(1)
    row = lambda ref: ref[layer:layer + 1, :]
    tap = lambda ref, k: ref[layer, k:k + 1, :]

    @pl.when(j == 0)
    def _():
        if has_state:
            xl_buf[0:3 * nB] = stlc_ref[...].reshape(3 * nB, C)
            h_car[...] = sth_ref[...]
        else:
            xl_buf[0:3 * nB] = jnp.zeros((3 * nB, C), F32)
            h_car[...] = jnp.zeros((nB, C), F32)
        _load_rows(z_buf, stsc_ref, SC_CONV - 1, nB)

    if x_bm:
        xt_buf[...] = jnp.swapaxes(x_ref[...], 0, 1)
        x_tm = xt_buf
    else:
        x_tm = x_ref
    hn = _modulated_norm(x_tm[...], row(ng_ref), sc_ref[...], sh_ref[...])
    hn_buf[...] = hn.reshape(R, D_MODEL).astype(BF16)

    in_proj = lambda k: _dot(hn_buf[...], win_ref[:, k * C:(k + 1) * C])

    xl_buf[3 * nB:3 * nB + R] = in_proj(0)
    xc = row(cb_ref) + tap(cw_ref, 0) * xl_buf[0:R]
    for k in range(1, LRU_CONV):
        xc = xc + tap(cw_ref, k) * xl_buf[k * nB:k * nB + R]

    neg_c_sp = -RG_C * jnp.logaddexp(-row(lam_ref), 0.0)
    xc_bf = xc.astype(BF16)
    for hf in range(C // MXU_COLS):
        lo, hi = hf * MXU_COLS, (hf + 1) * MXU_COLS
        gates = _dot(xc_bf[:, lo:hi], wg_ref[hf])
        r = jax.nn.sigmoid(gates[:, :MXU_COLS] + row(ba_ref)[:, lo:hi])
        ig = jax.nn.sigmoid(gates[:, MXU_COLS:] + row(bx_ref)[:, lo:hi])
        log_a = r * neg_c_sp[:, lo:hi]
        a = jnp.exp(log_a)
        mult = jnp.sqrt(-jnp.tanh(log_a) * (1.0 + a * a))
        a_buf[:, lo:hi] = a
        b_buf[:, lo:hi] = mult * (ig * xc[:, lo:hi])

        h = h_car[:, lo:hi]
        for t in range(tT):
            rows = pl.ds(t * nB, nB)
            h = a_buf[rows, lo:hi] * h + b_buf[rows, lo:hi]
            b_buf[rows, lo:hi] = h
        h_car[:, lo:hi] = h

    mix_buf[:, 0:C] = (b_buf[...] * jax.nn.gelu(in_proj(1))).astype(BF16)

    z_buf[2 * nB:2 * nB + R] = in_proj(3) * in_proj(4)
    zc = tap(scw_ref, 0) * z_buf[0:R]
    for k in range(1, SC_CONV):
        zc = zc + tap(scw_ref, k) * z_buf[k * nB:k * nB + R]
    mix_buf[:, C:2 * C] = (in_proj(2) * zc).astype(BF16)

    mix = _dot(mix_buf[...], wout_ref[...])
    y_ref[...] = x_tm[...] + g_ref[...][None] * mix.reshape(tT, nB, D_MODEL)

    lc_tail = xl_buf[R:R + 3 * nB]
    sc_tail = z_buf[R:R + 2 * nB]
    _keep_previous(prev, (nh_ref, nlc_ref, nsc_ref))
    nh_ref[layer] = h_car[...]
    nlc_ref[layer] = lc_tail.reshape(LRU_CONV - 1, nB, C)
    _store_rows(nsc_ref, layer, sc_tail, SC_CONV - 1, nB)
    xl_buf[0:3 * nB] = lc_tail
    z_buf[0:2 * nB] = sc_tail

    _cast_riders(ride_in, ride_out)


_MIX_SMALL = ("norm1_g", "lru_conv_w", "lru_conv_b", "lru_ba", "lru_bx", "lru_lambda", "sc_conv_w")


def _mix_call(l, x, mod, state, prev, small, w_gate, big, riders, *, tT, nB, x_bm):
    if x_bm:
        B, T, _ = x.shape
        x_spec = pl.BlockSpec((nB, tT, D_MODEL), lambda i, j: (i, j, 0))
    else:
        T, B, _ = x.shape
        x_spec = pl.BlockSpec((tT, nB, D_MODEL), lambda i, j: (j, i, 0))
    R = tT * nB
    C = D_HALF
    n_steps = T // tT
    assert not riders or B == nB
    whole = lambda i, j: (0, 0)
    mod_spec = lambda k: pl.BlockSpec((None, nB, D_MODEL), lambda i, j: (l, i, k))
    once = pl.Buffered(1)
    h_spec = lambda n: pl.BlockSpec((n, nB, C), lambda i, j: (0, i, 0))
    lc_spec = lambda n: pl.BlockSpec((n, LRU_CONV - 1, nB, C), lambda i, j: (0, 0, i, 0))
    sc_spec = lambda n: pl.BlockSpec((n, nB, SC_CONV - 1, C), lambda i, j: (0, i, 0, 0))
    state_specs, state_args = [], []
    if state is not None:
        state_specs = [
            pl.BlockSpec((None, nB, C), lambda i, j: (l, i, 0)),
            pl.BlockSpec((None, LRU_CONV - 1, nB, C), lambda i, j: (l, 0, i, 0)),
            pl.BlockSpec((None, nB, SC_CONV - 1, C), lambda i, j: (l, i, 0, 0)),
        ]
        state_args = list(state)
    prev_specs = [h_spec(l), lc_spec(l), sc_spec(l)] if l else []
    r_in, r_out, r_shapes = _rider_specs(riders, n_steps, lambda i, j: j)
    scratch = [
        pltpu.VMEM((R, D_MODEL), BF16),
        pltpu.VMEM((R + 3 * nB, C), F32),
        pltpu.VMEM((R + 2 * nB, C), F32),
        pltpu.VMEM((R, C), F32),
        pltpu.VMEM((R, C), F32),
        pltpu.VMEM((R, D_MODEL), BF16),
        pltpu.VMEM((nB, C), F32),
    ]
    if x_bm:
        scratch.append(pltpu.VMEM((tT, nB, D_MODEL), F32))
    outs = pl.pallas_call(
        functools.partial(_mix_kernel, layer=l, tT=tT, nB=nB, x_bm=x_bm,
                          has_state=state is not None, n_ride=len(riders)),
        grid=(B // nB, n_steps),
        in_specs=[
            x_spec, mod_spec(0), mod_spec(1), mod_spec(2),
            *[_whole_spec(small[name]) for name in _MIX_SMALL],
            *state_specs, *prev_specs,
            pl.BlockSpec((D_MODEL, 5 * C), whole, pipeline_mode=once),
            pl.BlockSpec((None, C // MXU_COLS, MXU_COLS, 2 * MXU_COLS),
                         lambda i, j: (l, 0, 0, 0), pipeline_mode=once),
            pl.BlockSpec((D_MODEL, D_MODEL), whole, pipeline_mode=once),
            *r_in,
        ],
        out_specs=[
            pl.BlockSpec((tT, nB, D_MODEL), lambda i, j: (j, i, 0)),
            h_spec(l + 1), lc_spec(l + 1), sc_spec(l + 1),
            *r_out,
        ],
        out_shape=[
            jax.ShapeDtypeStruct((T, B, D_MODEL), F32),
            jax.ShapeDtypeStruct((l + 1, B, C), F32),
            jax.ShapeDtypeStruct((l + 1, LRU_CONV - 1, B, C), F32),
            jax.ShapeDtypeStruct((l + 1, B, SC_CONV - 1, C), F32),
            *r_shapes,
        ],
        scratch_shapes=scratch,
        compiler_params=pltpu.CompilerParams(
            dimension_semantics=("arbitrary", "arbitrary"),
            vmem_limit_bytes=VMEM_LIMIT_BYTES),
        name="mix",
    )(x, mod, mod, mod, *[small[name] for name in _MIX_SMALL], *state_args, *prev,
      big["w_in"], w_gate, big["w_out"], *[src for src, _ in riders])
    return outs[:4], outs[4:]


def _ffn_kernel(*refs, layer, tT, nB, final, has_state, n_ride):
    refs = list(refs)
    x_ref, sh_ref, sc_ref, g_ref, ng_ref, cw_ref, fg_ref = _take(refs, 7)
    stfc_ref = _take(refs, 1)[0] if has_state else None
    prev = _take(refs, 1) if layer else []
    wup_ref, wdn_ref = _take(refs, 2)
    ride_in = _take(refs, n_ride)
    y_ref, nfc_ref = _take(refs, 2)
    ride_out = _take(refs, n_ride)
    hn_buf, u_buf, u_car, act_buf = refs

    R = tT * nB
    j = pl.program_id(1)

    @pl.when(j == 0)
    def _():
        _load_rows(u_car, stfc_ref, FFN_CONV - 1, nB)

    hn = _modulated_norm(x_ref[...], ng_ref[layer:layer + 1, :], sc_ref[...], sh_ref[...])
    hn_buf[...] = hn.reshape(R, D_MODEL).astype(BF16)

    for ci, c0 in enumerate(range(0, D_FF, MXU_COLS)):
        cols = slice(c0, c0 + MXU_COLS)
        tap = lambda k: cw_ref[layer, k:k + 1, cols]
        ub = u_buf.at[ci % 2]
        ub[0:2 * nB] = u_car[:, cols]
        ub[2 * nB:2 * nB + R] = _dot(hn_buf[...], wup_ref[:, cols])
        v = _dot(hn_buf[...], wup_ref[:, D_FF + c0:D_FF + c0 + MXU_COLS])
        uc = tap(0) * ub[0:R]
        for k in range(1, FFN_CONV):
            uc = uc + tap(k) * ub[k * nB:k * nB + R]
        act_buf[:, cols] = (jax.nn.gelu(uc) * v).astype(BF16)
        u_car[:, cols] = ub[R:R + 2 * nB]

    out = _dot(act_buf[...], wdn_ref[...])
    xn = x_ref[...] + g_ref[...][None] * out.reshape(tT, nB, D_MODEL)
    if final:
        ms = jnp.mean(xn * xn, axis=-1, keepdims=True)
        xn = xn * jax.lax.rsqrt(ms + EPS) * fg_ref[...][None]
        y_ref[...] = jnp.swapaxes(xn, 0, 1)
    else:
        y_ref[...] = xn

    _keep_previous(prev, (nfc_ref,))
    _store_rows(nfc_ref, layer, u_car[...], FFN_CONV - 1, nB)
    _cast_riders(ride_in, ride_out)


def _ffn_call(l, x, mod, st_fc, prev, small, big, riders, *, tT, nB, final):
    T, B, _ = x.shape
    R = tT * nB
    n_steps = T // tT
    assert not riders or B == nB
    whole = lambda i, j: (0, 0)
    mod_spec = lambda k: pl.BlockSpec((None, nB, D_MODEL), lambda i, j: (l, i, k))
    once = pl.Buffered(1)
    fc_spec = lambda n: pl.BlockSpec((n, nB, FFN_CONV - 1, D_FF), lambda i, j: (0, i, 0, 0))
    state_specs, state_args = [], []
    if st_fc is not None:
        state_specs = [pl.BlockSpec((None, nB, FFN_CONV - 1, D_FF), lambda i, j: (l, i, 0, 0))]
        state_args = [st_fc]
    prev_specs = [fc_spec(l)] if l else []
    if final:
        y_spec = pl.BlockSpec((nB, tT, D_MODEL), lambda i, j: (i, j, 0))
        y_shape = jax.ShapeDtypeStruct((B, T, D_MODEL), F32)
    else:
        y_spec = pl.BlockSpec((tT, nB, D_MODEL), lambda i, j: (j, i, 0))
        y_shape = jax.ShapeDtypeStruct((T, B, D_MODEL), F32)
    r_in, r_out, r_shapes = _rider_specs(riders, n_steps, lambda i, j: j)
    outs = pl.pallas_call(
        functools.partial(_ffn_kernel, layer=l, tT=tT, nB=nB, final=final,
                          has_state=st_fc is not None, n_ride=len(riders)),
        grid=(B // nB, n_steps),
        in_specs=[
            pl.BlockSpec((tT, nB, D_MODEL), lambda i, j: (j, i, 0)),
            mod_spec(3), mod_spec(4), mod_spec(5),
            _whole_spec(small["norm2_g"]), _whole_spec(small["ffn_conv_w"]),
            _whole_spec(small["final_g"]),
            *state_specs, *prev_specs,
            pl.BlockSpec((D_MODEL, 2 * D_FF), whole, pipeline_mode=once),
            pl.BlockSpec((D_FF, D_MODEL), whole, pipeline_mode=once),
            *r_in,
        ],
        out_specs=[y_spec, fc_spec(l + 1), *r_out],
        out_shape=[
            y_shape,
            jax.ShapeDtypeStruct((l + 1, B, FFN_CONV - 1, D_FF), F32),
            *r_shapes,
        ],
        scratch_shapes=[
            pltpu.VMEM((R, D_MODEL), BF16),
            pltpu.VMEM((2, R + 2 * nB, MXU_COLS), F32),
            pltpu.VMEM((2 * nB, D_FF), F32),
            pltpu.VMEM((R, D_FF), BF16),
        ],
        compiler_params=pltpu.CompilerParams(
            dimension_semantics=("arbitrary", "arbitrary"),
            vmem_limit_bytes=VMEM_LIMIT_BYTES),
        name="ffn",
    )(x, mod, mod, mod, small["norm2_g"], small["ffn_conv_w"], small["final_g"],
      *state_args, *prev, big["w_up"], big["w_dn"], *[src for src, _ in riders])
    return outs[:2], outs[2:]


def _gate_weights(lru_wa, lru_wx):
    eye = jnp.eye(LRU_HEADS, dtype=lru_wa.dtype)
    dense = lambda w: jnp.einsum("lhij,hk->lhikj", w, eye).reshape(DEPTH, D_HALF, D_HALF)
    wa, wx = dense(lru_wa), dense(lru_wx)
    ws = []
    for hf in range(D_HALF // MXU_COLS):
        s = slice(hf * MXU_COLS, (hf + 1) * MXU_COLS)
        ws.append(jnp.concatenate([wa[:, s, s], wx[:, s, s]], axis=-1))
    return jnp.stack(ws, axis=1).astype(BF16)


def _run_trunk(x_bm, mod, state, small, w_gate, big, f32w, *, tT, nB):
    x = x_bm
    mix_states, fc_states = [], []
    for l in range(DEPTH):
        riders = [(f32w["w_up"], l), (f32w["w_dn"], l)] if f32w else []
        (x, *mix_states), cast = _mix_call(
            l, x, mod, None if state is None else state[:3], mix_states, small, w_gate,
            big[l], riders, tT=tT, nB=nB, x_bm=(l == 0))
        if cast:
            big[l]["w_up"], big[l]["w_dn"] = cast
        riders = [(f32w["w_in"], l + 1), (f32w["w_out"], l + 1)] if f32w and l + 1 < DEPTH else []
        (x, *fc_states), cast = _ffn_call(
            l, x, mod, None if state is None else state[3], fc_states, small,
            big[l], riders, tT=tT, nB=nB, final=(l == DEPTH - 1))
        if cast:
            big[l + 1]["w_in"], big[l + 1]["w_out"] = cast
    nh, nlc, nsc = mix_states
    return (x, nh, jnp.swapaxes(nlc, 1, 2), nsc, fc_states[0])


def kernel(x_prompt, x_sample, c_prompt, c_sample, state_lru_h, state_lru_conv, state_sc_conv, state_ffn_conv, w_ada, b_ada, norm1_g, norm2_g, w_in, lru_conv_w, lru_conv_b, lru_wa, lru_ba, lru_wx, lru_bx, lru_lambda, sc_conv_w, w_out, ffn_w_up, ffn_conv_w, ffn_w_down, final_g):
    small = dict(
        norm1_g=norm1_g, norm2_g=norm2_g, final_g=final_g[None, :], lru_conv_w=lru_conv_w,
        lru_conv_b=lru_conv_b, lru_ba=lru_ba, lru_bx=lru_bx, lru_lambda=lru_lambda,
        sc_conv_w=sc_conv_w, ffn_conv_w=ffn_conv_w)
    w_gate = _gate_weights(lru_wa, lru_wx)
    f32w = dict(w_in=w_in, w_out=w_out, w_up=ffn_w_up, w_dn=ffn_w_down)

    (mod_p, mod_s), cast = _ada_call(c_prompt, c_sample, w_ada, b_ada,
                                     [(w_in, 0), (w_out, 0)])
    big = [dict() for _ in range(DEPTH)]
    big[0]["w_in"], big[0]["w_out"] = cast

    out_p = _run_trunk(x_prompt, mod_p, None, small, w_gate, big, f32w,
                       tT=64, nB=x_prompt.shape[0])
    out_s = _run_trunk(x_sample, mod_s,
                       (state_lru_h, jnp.swapaxes(state_lru_conv, 1, 2), state_sc_conv,
                        state_ffn_conv),
                       small, w_gate, big, None, tT=x_sample.shape[1], nB=64)
    return (out_p[0], out_s[0]) + out_p[1:] + out_s[1:]
```

```python
import functools

import jax
import jax.numpy as jnp
from jax.experimental import pallas as pl
from jax.experimental.pallas import tpu as pltpu

D_MODEL = 1024
DEPTH = 2
D_HALF = 512
LRU_HEADS = 8
LRU_HEAD_DIM = 64
LRU_CONV = 4
SC_CONV = 3
FFN_CONV = 3
RG_C = 8.0
D_FF = 2816
EPS = 1e-6
N_MOD = 6

MXU_COLS = 256
BF16_SUBLANES = 16
VMEM_LIMIT_BYTES = 56 * 1024 * 1024
ADA_COL_BLOCKS = 4

BF16 = jnp.bfloat16
F32 = jnp.float32


def _dot(a, b):
    return jnp.dot(a, b, preferred_element_type=F32)


def _take(refs, n):
    head = refs[:n]
    del refs[:n]
    return head


def _whole_spec(arr):
    return pl.BlockSpec(arr.shape, lambda *g: (0,) * arr.ndim)


def _modulated_norm(x3, gain, scale, shift):
    ms = jnp.mean(x3 * x3, axis=-1, keepdims=True)
    y = x3 * jax.lax.rsqrt(ms + EPS)
    return y * (gain * (1.0 + scale))[None] + shift[None]


def _load_rows(buf, st_ref, n_rows, nB):
    if st_ref is None:
        buf[0:n_rows * nB] = jnp.zeros((n_rows * nB, buf.shape[1]), buf.dtype)
    else:
        for k in range(n_rows):
            buf[k * nB:(k + 1) * nB] = st_ref[:, k, :]


def _store_rows(out_ref, layer, tail, n_rows, nB):
    for k in range(n_rows):
        out_ref[layer, :, k, :] = tail[k * nB:(k + 1) * nB]


def _keep_previous(prev_refs, out_refs):
    for prev, out in zip(prev_refs, out_refs):
        out[0:prev.shape[0]] = prev[...]


def _rider_specs(riders, n_steps, step_of):
    in_specs, out_specs, out_shapes = [], [], []
    for src, layer in riders:
        _, K, N = src.shape
        hold = 1
        while (K * hold) % n_steps or (K * hold // n_steps) % BF16_SUBLANES:
            hold *= 2
            assert hold <= n_steps, (K, n_steps)
        rows = K * hold // n_steps
        in_specs.append(pl.BlockSpec(
            (None, rows, N), lambda *g, l=layer, h=hold: (l, step_of(*g) // h, 0)))
        out_specs.append(pl.BlockSpec((rows, N), lambda *g, h=hold: (step_of(*g) // h, 0)))
        out_shapes.append(jax.ShapeDtypeStruct((K, N), BF16))
    return in_specs, out_specs, out_shapes


def _cast_riders(ride_in, ride_out):
    for src, dst in zip(ride_in, ride_out):
        dst[...] = src[...].astype(BF16)


def _ada_kernel(*refs, n_ride):
    refs = list(refs)
    cp_ref, cs_ref, w_ref, b_ref = _take(refs, 4)
    ride_in = _take(refs, n_ride)
    op_ref, os_ref = _take(refs, 2)
    ride_out = _take(refs, n_ride)
    w = w_ref[...].astype(BF16)
    b = b_ref[pl.ds(pl.program_id(0) // ADA_COL_BLOCKS, 1), :]
    op_ref[...] = _dot(jax.nn.silu(cp_ref[...]).astype(BF16), w) + b
    os_ref[...] = _dot(jax.nn.silu(cs_ref[...]).astype(BF16), w) + b
    _cast_riders(ride_in, ride_out)


def _ada_call(c_prompt, c_sample, w_ada, b_ada, riders):
    n_p, n_s = c_prompt.shape[0], c_sample.shape[0]
    n_steps = DEPTH * ADA_COL_BLOCKS
    cols = N_MOD * D_MODEL // ADA_COL_BLOCKS
    lk = lambda s: (s // ADA_COL_BLOCKS, 0, s % ADA_COL_BLOCKS)
    c_spec = lambda n: pl.BlockSpec((n, D_MODEL), lambda s: (0, 0))
    o_spec = lambda n: pl.BlockSpec((None, n, cols), lk)
    r_in, r_out, r_shapes = _rider_specs(riders, n_steps, lambda s: s)
    outs = pl.pallas_call(
        functools.partial(_ada_kernel, n_ride=len(riders)),
        grid=(n_steps,),
        in_specs=[c_spec(n_p), c_spec(n_s),
                  pl.BlockSpec((None, D_MODEL, cols), lk),
                  pl.BlockSpec((DEPTH, cols), lambda s: (0, s % ADA_COL_BLOCKS)),
                  *r_in],
        out_specs=[o_spec(n_p), o_spec(n_s), *r_out],
        out_shape=[jax.ShapeDtypeStruct((DEPTH, n_p, N_MOD * D_MODEL), F32),
                   jax.ShapeDtypeStruct((DEPTH, n_s, N_MOD * D_MODEL), F32),
                   *r_shapes],
        name="ada",
    )(c_prompt, c_sample, w_ada, b_ada, *[src for src, _ in riders])
    return outs[:2], outs[2:]


def _mix_kernel(*refs, layer, tT, nB, x_bm, has_state, n_ride):
    refs = list(refs)
    x_ref, sh_ref, sc_ref, g_ref = _take(refs, 4)
    ng_ref, cw_ref, cb_ref, ba_ref, bx_ref, lam_ref, scw_ref = _take(refs, 7)
    sth_ref, stlc_ref, stsc_ref = _take(refs, 3) if has_state else (None, None, None)
    prev = _take(refs, 3) if layer else []
    win_ref, wg_ref, wout_ref = _take(refs, 3)
    ride_in = _take(refs, n_ride)
    y_ref, nh_ref, nlc_ref, nsc_ref = _take(refs, 4)
    ride_out = _take(refs, n_ride)
    hn_buf, xl_buf, z_buf, a_buf, b_buf, mix_buf, h_car = _take(refs, 7)
    xt_buf = refs[0] if x_bm else None

    R = tT * nB
    C = D_HALF
    j = pl.program_id(1)
    row = lambda ref: ref[layer:layer + 1, :]
    tap = lambda ref, k: ref[layer, k:k + 1, :]

    @pl.when(j == 0)
    def _():
        if has_state:
            xl_buf[0:3 * nB] = stlc_ref[...].reshape(3 * nB, C)
            h_car[...] = sth_ref[...]
        else:
            xl_buf[0:3 * nB] = jnp.zeros((3 * nB, C), F32)
            h_car[...] = jnp.zeros((nB, C), F32)
        _load_rows(z_buf, stsc_ref, SC_CONV - 1, nB)

    if x_bm:
        xt_buf[...] = jnp.swapaxes(x_ref[...], 0, 1)
        x_tm = xt_buf
    else:
        x_tm = x_ref
    hn = _modulated_norm(x_tm[...], row(ng_ref), sc_ref[...], sh_ref[...])
    hn_buf[...] = hn.reshape(R, D_MODEL).astype(BF16)

    in_proj = lambda k: _dot(hn_buf[...], win_ref[:, k * C:(k + 1) * C])

    xl_buf[3 * nB:3 * nB + R] = in_proj(0)
    xc = row(cb_ref) + tap(cw_ref, 0) * xl_buf[0:R]
    for k in range(1, LRU_CONV):
        xc = xc + tap(cw_ref, k) * xl_buf[k * nB:k * nB + R]

    neg_c_sp = -RG_C * jnp.logaddexp(-row(lam_ref), 0.0)
    xc_bf = xc.astype(BF16)
    for hf in range(C // MXU_COLS):
        lo, hi = hf * MXU_COLS, (hf + 1) * MXU_COLS
        gates = _dot(xc_bf[:, lo:hi], wg_ref[hf])
        r = jax.nn.sigmoid(gates[:, :MXU_COLS] + row(ba_ref)[:, lo:hi])
        ig = jax.nn.sigmoid(gates[:, MXU_COLS:] + row(bx_ref)[:, lo:hi])
        log_a = r * neg_c_sp[:, lo:hi]
        a = jnp.exp(log_a)
        mult = jnp.sqrt(-jnp.tanh(log_a) * (1.0 + a * a))
        a_buf[:, lo:hi] = a
        b_buf[:, lo:hi] = mult * (ig * xc[:, lo:hi])

        h = h_car[:, lo:hi]
        for t in range(tT):
            rows = pl.ds(t * nB, nB)
            h = a_buf[rows, lo:hi] * h + b_buf[rows, lo:hi]
            b_buf[rows, lo:hi] = h
        h_car[:, lo:hi] = h

    mix_buf[:, 0:C] = (b_buf[...] * jax.nn.gelu(in_proj(1))).astype(BF16)

    z_buf[2 * nB:2 * nB + R] = in_proj(3) * in_proj(4)
    zc = tap(scw_ref, 0) * z_buf[0:R]
    for k in range(1, SC_CONV):
        zc = zc + tap(scw_ref, k) * z_buf[k * nB:k * nB + R]
    mix_buf[:, C:2 * C] = (in_proj(2) * zc).astype(BF16)

    mix = _dot(mix_buf[...], wout_ref[...])
    y_ref[...] = x_tm[...] + g_ref[...][None] * mix.reshape(tT, nB, D_MODEL)

    lc_tail = xl_buf[R:R + 3 * nB]
    sc_tail = z_buf[R:R + 2 * nB]
    _keep_previous(prev, (nh_ref, nlc_ref, nsc_ref))
    nh_ref[layer] = h_car[...]
    nlc_ref[layer] = lc_tail.reshape(LRU_CONV - 1, nB, C)
    _store_rows(nsc_ref, layer, sc_tail, SC_CONV - 1, nB)
    xl_buf[0:3 * nB] = lc_tail
    z_buf[0:2 * nB] = sc_tail

    _cast_riders(ride_in, ride_out)


_MIX_SMALL = ("norm1_g", "lru_conv_w", "lru_conv_b", "lru_ba", "lru_bx", "lru_lambda", "sc_conv_w")


def _mix_call(l, x, mod, state, prev, small, w_gate, big, riders, *, tT, nB, x_bm):
    if x_bm:
        B, T, _ = x.shape
        x_spec = pl.BlockSpec((nB, tT, D_MODEL), lambda i, j: (i, j, 0))
    else:
        T, B, _ = x.shape
        x_spec = pl.BlockSpec((tT, nB, D_MODEL), lambda i, j: (j, i, 0))
    R = tT * nB
    C = D_HALF
    n_steps = T // tT
    assert not riders or B == nB
    whole = lambda i, j: (0, 0)
    mod_spec = lambda k: pl.BlockSpec((None, nB, D_MODEL), lambda i, j: (l, i, k))
    once = pl.Buffered(1)
    h_spec = lambda n: pl.BlockSpec((n, nB, C), lambda i, j: (0, i, 0))
    lc_spec = lambda n: pl.BlockSpec((n, LRU_CONV - 1, nB, C), lambda i, j: (0, 0, i, 0))
    sc_spec = lambda n: pl.BlockSpec((n, nB, SC_CONV - 1, C), lambda i, j: (0, i, 0, 0))
    state_specs, state_args = [], []
    if state is not None:
        state_specs = [
            pl.BlockSpec((None, nB, C), lambda i, j: (l, i, 0)),
            pl.BlockSpec((None, LRU_CONV - 1, nB, C), lambda i, j: (l, 0, i, 0)),
            pl.BlockSpec((None, nB, SC_CONV - 1, C), lambda i, j: (l, i, 0, 0)),
        ]
        state_args = list(state)
    prev_specs = [h_spec(l), lc_spec(l), sc_spec(l)] if l else []
    r_in, r_out, r_shapes = _rider_specs(riders, n_steps, lambda i, j: j)
    scratch = [
        pltpu.VMEM((R, D_MODEL), BF16),
        pltpu.VMEM((R + 3 * nB, C), F32),
        pltpu.VMEM((R + 2 * nB, C), F32),
        pltpu.VMEM((R, C), F32),
        pltpu.VMEM((R, C), F32),
        pltpu.VMEM((R, D_MODEL), BF16),
        pltpu.VMEM((nB, C), F32),
    ]
    if x_bm:
        scratch.append(pltpu.VMEM((tT, nB, D_MODEL), F32))
    outs = pl.pallas_call(
        functools.partial(_mix_kernel, layer=l, tT=tT, nB=nB, x_bm=x_bm,
                          has_state=state is not None, n_ride=len(riders)),
        grid=(B // nB, n_steps),
        in_specs=[
            x_spec, mod_spec(0), mod_spec(1), mod_spec(2),
            *[_whole_spec(small[name]) for name in _MIX_SMALL],
            *state_specs, *prev_specs,
            pl.BlockSpec((D_MODEL, 5 * C), whole, pipeline_mode=once),
            pl.BlockSpec((None, C // MXU_COLS, MXU_COLS, 2 * MXU_COLS),
                         lambda i, j: (l, 0, 0, 0), pipeline_mode=once),
            pl.BlockSpec((D_MODEL, D_MODEL), whole, pipeline_mode=once),
            *r_in,
        ],
        out_specs=[
            pl.BlockSpec((tT, nB, D_MODEL), lambda i, j: (j, i, 0)),
            h_spec(l + 1), lc_spec(l + 1), sc_spec(l + 1),
            *r_out,
        ],
        out_shape=[
            jax.ShapeDtypeStruct((T, B, D_MODEL), F32),
            jax.ShapeDtypeStruct((l + 1, B, C), F32),
            jax.ShapeDtypeStruct((l + 1, LRU_CONV - 1, B, C), F32),
            jax.ShapeDtypeStruct((l + 1, B, SC_CONV - 1, C), F32),
            *r_shapes,
        ],
        scratch_shapes=scratch,
        compiler_params=pltpu.CompilerParams(
            dimension_semantics=("arbitrary", "arbitrary"),
            vmem_limit_bytes=VMEM_LIMIT_BYTES),
        name="mix",
    )(x, mod, mod, mod, *[small[name] for name in _MIX_SMALL], *state_args, *prev,
      big["w_in"], w_gate, big["w_out"], *[src for src, _ in riders])
    return outs[:4], outs[4:]


def _ffn_kernel(*refs, layer, tT, nB, final, has_state, n_ride):
    refs = list(refs)
    x_ref, sh_ref, sc_ref, g_ref, ng_ref, cw_ref, fg_ref = _take(refs, 7)
    stfc_ref = _take(refs, 1)[0] if has_state else None
    prev = _take(refs, 1) if layer else []
    wup_ref, wdn_ref = _take(refs, 2)
    ride_in = _take(refs, n_ride)
    y_ref, nfc_ref = _take(refs, 2)
    ride_out = _take(refs, n_ride)
    hn_buf, u_buf, u_car, act_buf = refs

    R = tT * nB
    j = pl.program_id(1)

    @pl.when(j == 0)
    def _():
        _load_rows(u_car, stfc_ref, FFN_CONV - 1, nB)

    hn = _modulated_norm(x_ref[...], ng_ref[layer:layer + 1, :], sc_ref[...], sh_ref[...])
    hn_buf[...] = hn.reshape(R, D_MODEL).astype(BF16)

    for ci, c0 in enumerate(range(0, D_FF, MXU_COLS)):
        cols = slice(c0, c0 + MXU_COLS)
        tap = lambda k: cw_ref[layer, k:k + 1, cols]
        ub = u_buf.at[ci % 2]
        ub[0:2 * nB] = u_car[:, cols]
        ub[2 * nB:2 * nB + R] = _dot(hn_buf[...], wup_ref[:, cols])
        v = _dot(hn_buf[...], wup_ref[:, D_FF + c0:D_FF + c0 + MXU_COLS])
        uc = tap(0) * ub[0:R]
        for k in range(1, FFN_CONV):
            uc = uc + tap(k) * ub[k * nB:k * nB + R]
        act_buf[:, cols] = (jax.nn.gelu(uc) * v).astype(BF16)
        u_car[:, cols] = ub[R:R + 2 * nB]

    out = _dot(act_buf[...], wdn_ref[...])
    xn = x_ref[...] + g_ref[...][None] * out.reshape(tT, nB, D_MODEL)
    if final:
        ms = jnp.mean(xn * xn, axis=-1, keepdims=True)
        xn = xn * jax.lax.rsqrt(ms + EPS) * fg_ref[...][None]
        y_ref[...] = jnp.swapaxes(xn, 0, 1)
    else:
        y_ref[...] = xn

    _keep_previous(prev, (nfc_ref,))
    _store_rows(nfc_ref, layer, u_car[...], FFN_CONV - 1, nB)
    _cast_riders(ride_in, ride_out)


def _ffn_call(l, x, mod, st_fc, prev, small, big, riders, *, tT, nB, final):
    T, B, _ = x.shape
    R = tT * nB
    n_steps = T // tT
    assert not riders or B == nB
    whole = lambda i, j: (0, 0)
    mod_spec = lambda k: pl.BlockSpec((None, nB, D_MODEL), lambda i, j: (l, i, k))
    once = pl.Buffered(1)
    fc_spec = lambda n: pl.BlockSpec((n, nB, FFN_CONV - 1, D_FF), lambda i, j: (0, i, 0, 0))
    state_specs, state_args = [], []
    if st_fc is not None:
        state_specs = [pl.BlockSpec((None, nB, FFN_CONV - 1, D_FF), lambda i, j: (l, i, 0, 0))]
        state_args = [st_fc]
    prev_specs = [fc_spec(l)] if l else []
    if final:
        y_spec = pl.BlockSpec((nB, tT, D_MODEL), lambda i, j: (i, j, 0))
        y_shape = jax.ShapeDtypeStruct((B, T, D_MODEL), F32)
    else:
        y_spec = pl.BlockSpec((tT, nB, D_MODEL), lambda i, j: (j, i, 0))
        y_shape = jax.ShapeDtypeStruct((T, B, D_MODEL), F32)
    r_in, r_out, r_shapes = _rider_specs(riders, n_steps, lambda i, j: j)
    outs = pl.pallas_call(
        functools.partial(_ffn_kernel, layer=l, tT=tT, nB=nB, final=final,
                          has_state=st_fc is not None, n_ride=len(riders)),
        grid=(B // nB, n_steps),
        in_specs=[
            pl.BlockSpec((tT, nB, D_MODEL), lambda i, j: (j, i, 0)),
            mod_spec(3), mod_spec(4), mod_spec(5),
            _whole_spec(small["norm2_g"]), _whole_spec(small["ffn_conv_w"]),
            _whole_spec(small["final_g"]),
            *state_specs, *prev_specs,
            pl.BlockSpec((D_MODEL, 2 * D_FF), whole, pipeline_mode=once),
            pl.BlockSpec((D_FF, D_MODEL), whole, pipeline_mode=once),
            *r_in,
        ],
        out_specs=[y_spec, fc_spec(l + 1), *r_out],
        out_shape=[
            y_shape,
            jax.ShapeDtypeStruct((l + 1, B, FFN_CONV - 1, D_FF), F32),
            *r_shapes,
        ],
        scratch_shapes=[
            pltpu.VMEM((R, D_MODEL), BF16),
            pltpu.VMEM((2, R + 2 * nB, MXU_COLS), F32),
            pltpu.VMEM((2 * nB, D_FF), F32),
            pltpu.VMEM((R, D_FF), BF16),
        ],
        compiler_params=pltpu.CompilerParams(
            dimension_semantics=("arbitrary", "arbitrary"),
            vmem_limit_bytes=VMEM_LIMIT_BYTES),
        name="ffn",
    )(x, mod, mod, mod, small["norm2_g"], small["ffn_conv_w"], small["final_g"],
      *state_args, *prev, big["w_up"], big["w_dn"], *[src for src, _ in riders])
    return outs[:2], outs[2:]


def _gate_weights(lru_wa, lru_wx):
    eye = jnp.eye(LRU_HEADS, dtype=lru_wa.dtype)
    dense = lambda w: jnp.einsum("lhij,hk->lhikj", w, eye).reshape(DEPTH, D_HALF, D_HALF)
    wa, wx = dense(lru_wa), dense(lru_wx)
    ws = []
    for hf in range(D_HALF // MXU_COLS):
        s = slice(hf * MXU_COLS, (hf + 1) * MXU_COLS)
        ws.append(jnp.concatenate([wa[:, s, s], wx[:, s, s]], axis=-1))
    return jnp.stack(ws, axis=1).astype(BF16)


def _run_trunk(x_bm, mod, state, small, w_gate, big, f32w, *, tT, nB):
    x = x_bm
    mix_states, fc_states = [], []
    for l in range(DEPTH):
        riders = [(f32w["w_up"], l), (f32w["w_dn"], l)] if f32w else []
        (x, *mix_states), cast = _mix_call(
            l, x, mod, None if state is None else state[:3], mix_states, small, w_gate,
            big[l], riders, tT=tT, nB=nB, x_bm=(l == 0))
        if cast:
            big[l]["w_up"], big[l]["w_dn"] = cast
        riders = [(f32w["w_in"], l + 1), (f32w["w_out"], l + 1)] if f32w and l + 1 < DEPTH else []
        (x, *fc_states), cast = _ffn_call(
            l, x, mod, None if state is None else state[3], fc_states, small,
            big[l], riders, tT=tT, nB=nB, final=(l == DEPTH - 1))
        if cast:
            big[l + 1]["w_in"], big[l + 1]["w_out"] = cast
    nh, nlc, nsc = mix_states
    return (x, nh, jnp.swapaxes(nlc, 1, 2), nsc, fc_states[0])


def kernel(x_prompt, x_sample, c_prompt, c_sample, state_lru_h, state_lru_conv, state_sc_conv, state_ffn_conv, w_ada, b_ada, norm1_g, norm2_g, w_in, lru_conv_w, lru_conv_b, lru_wa, lru_ba, lru_wx, lru_bx, lru_lambda, sc_conv_w, w_out, ffn_w_up, ffn_conv_w, ffn_w_down, final_g):
    small = dict(
        norm1_g=norm1_g, norm2_g=norm2_g, final_g=final_g[None, :], lru_conv_w=lru_conv_w,
        lru_conv_b=lru_conv_b, lru_ba=lru_ba, lru_bx=lru_bx, lru_lambda=lru_lambda,
        sc_conv_w=sc_conv_w, ffn_conv_w=ffn_conv_w)
    w_gate = _gate_weights(lru_wa, lru_wx)
    f32w = dict(w_in=w_in, w_out=w_out, w_up=ffn_w_up, w_dn=ffn_w_down)

    (mod_p, mod_s), cast = _ada_call(c_prompt, c_sample, w_ada, b_ada,
                                     [(w_in, 0), (w_out, 0)])
    big = [dict() for _ in range(DEPTH)]
    big[0]["w_in"], big[0]["w_out"] = cast

    out_p = _run_trunk(x_prompt, mod_p, None, small, w_gate, big, f32w,
                       tT=128, nB=x_prompt.shape[0])
    out_s = _run_trunk(x_sample, mod_s,
                       (state_lru_h, jnp.swapaxes(state_lru_conv, 1, 2), state_sc_conv,
                        state_ffn_conv),
                       small, w_gate, big, None, tT=x_sample.shape[1], nB=64)
    return (out_p[0], out_s[0]) + out_p[1:] + out_s[1:]
```

```python
import functools

import jax
import jax.numpy as jnp
from jax.experimental import pallas as pl
from jax.experimental.pallas import tpu as pltpu

D_MODEL = 1024
DEPTH = 2
D_HALF = 512
LRU_HEADS = 8
LRU_HEAD_DIM = 64
LRU_CONV = 4
SC_CONV = 3
FFN_CONV = 3
RG_C = 8.0
D_FF = 2816
EPS = 1e-6
N_MOD = 6

MXU_COLS = 256
BF16_SUBLANES = 16
VMEM_LIMIT_BYTES = 56 * 1024 * 1024
ADA_COL_BLOCKS = 4

BF16 = jnp.bfloat16
F32 = jnp.float32


def _dot(a, b):
    return jnp.dot(a, b, preferred_element_type=F32)


def _take(refs, n):
    head = refs[:n]
    del refs[:n]
    return head


def _whole_spec(arr):
    return pl.BlockSpec(arr.shape, lambda *g: (0,) * arr.ndim)


def _modulated_norm(x3, gain, scale, shift):
    ms = jnp.mean(x3 * x3, axis=-1, keepdims=True)
    y = x3 * jax.lax.rsqrt(ms + EPS)
    return y * (gain * (1.0 + scale))[None] + shift[None]


def _load_rows(buf, st_ref, n_rows, nB):
    if st_ref is None:
        buf[0:n_rows * nB] = jnp.zeros((n_rows * nB, buf.shape[1]), buf.dtype)
    else:
        for k in range(n_rows):
            buf[k * nB:(k + 1) * nB] = st_ref[:, k, :]


def _store_rows(out_ref, layer, tail, n_rows, nB):
    for k in range(n_rows):
        out_ref[layer, :, k, :] = tail[k * nB:(k + 1) * nB]


def _keep_previous(prev_refs, out_refs):
    for prev, out in zip(prev_refs, out_refs):
        out[0:prev.shape[0]] = prev[...]


def _rider_specs(riders, n_steps, step_of):
    in_specs, out_specs, out_shapes = [], [], []
    for src, layer in riders:
        _, K, N = src.shape
        hold = 1
        while (K * hold) % n_steps or (K * hold // n_steps) % BF16_SUBLANES:
            hold *= 2
            assert hold <= n_steps, (K, n_steps)
        rows = K * hold // n_steps
        in_specs.append(pl.BlockSpec(
            (None, rows, N), lambda *g, l=layer, h=hold: (l, step_of(*g) // h, 0)))
        out_specs.append(pl.BlockSpec((rows, N), lambda *g, h=hold: (step_of(*g) // h, 0)))
        out_shapes.append(jax.ShapeDtypeStruct((K, N), BF16))
    return in_specs, out_specs, out_shapes


def _cast_riders(ride_in, ride_out):
    for src, dst in zip(ride_in, ride_out):
        dst[...] = src[...].astype(BF16)


def _ada_kernel(*refs, n_ride):
    refs = list(refs)
    cp_ref, cs_ref, w_ref, b_ref = _take(refs, 4)
    ride_in = _take(refs, n_ride)
    op_ref, os_ref = _take(refs, 2)
    ride_out = _take(refs, n_ride)
    w = w_ref[...].astype(BF16)
    b = b_ref[pl.ds(pl.program_id(0) // ADA_COL_BLOCKS, 1), :]
    op_ref[...] = _dot(jax.nn.silu(cp_ref[...]).astype(BF16), w) + b
    os_ref[...] = _dot(jax.nn.silu(cs_ref[...]).astype(BF16), w) + b
    _cast_riders(ride_in, ride_out)


def _ada_call(c_prompt, c_sample, w_ada, b_ada, riders):
    n_p, n_s = c_prompt.shape[0], c_sample.shape[0]
    n_steps = DEPTH * ADA_COL_BLOCKS
    cols = N_MOD * D_MODEL // ADA_COL_BLOCKS
    lk = lambda s: (s // ADA_COL_BLOCKS, 0, s % ADA_COL_BLOCKS)
    c_spec = lambda n: pl.BlockSpec((n, D_MODEL), lambda s: (0, 0))
    o_spec = lambda n: pl.BlockSpec((None, n, cols), lk)
    r_in, r_out, r_shapes = _rider_specs(riders, n_steps, lambda s: s)
    outs = pl.pallas_call(
        functools.partial(_ada_kernel, n_ride=len(riders)),
        grid=(n_steps,),
        in_specs=[c_spec(n_p), c_spec(n_s),
                  pl.BlockSpec((None, D_MODEL, cols), lk),
                  pl.BlockSpec((DEPTH, cols), lambda s: (0, s % ADA_COL_BLOCKS)),
                  *r_in],
        out_specs=[o_spec(n_p), o_spec(n_s), *r_out],
        out_shape=[jax.ShapeDtypeStruct((DEPTH, n_p, N_MOD * D_MODEL), F32),
                   jax.ShapeDtypeStruct((DEPTH, n_s, N_MOD * D_MODEL), F32),
                   *r_shapes],
        name="ada",
    )(c_prompt, c_sample, w_ada, b_ada, *[src for src, _ in riders])
    return outs[:2], outs[2:]


def _mix_kernel(*refs, layer, tT, nB, x_bm, has_state, n_ride):
    refs = list(refs)
    x_ref, sh_ref, sc_ref, g_ref = _take(refs, 4)
    ng_ref, cw_ref, cb_ref, ba_ref, bx_ref, lam_ref, scw_ref = _take(refs, 7)
    sth_ref, stlc_ref, stsc_ref = _take(refs, 3) if has_state else (None, None, None)
    prev = _take(refs, 3) if layer else []
    win_ref, wg_ref, wout_ref = _take(refs, 3)
    ride_in = _take(refs, n_ride)
    y_ref, nh_ref, nlc_ref, nsc_ref = _take(refs, 4)
    ride_out = _take(refs, n_ride)
    hn_buf, xl_buf, z_buf, a_buf, b_buf, mix_buf, h_car = _take(refs, 7)
    xt_buf = refs[0] if x_bm else None

    R = tT * nB
    C = D_HALF
    j = pl.program_id(1)
    row = lambda ref: ref[layer:layer + 1, :]
    tap = lambda ref, k: ref[layer, k:k + 1, :]

    @pl.when(j == 0)
    def _():
        if has_state:
            xl_buf[0:3 * nB] = stlc_ref[...].reshape(3 * nB, C)
            h_car[...] = sth_ref[...]
        else:
            xl_buf[0:3 * nB] = jnp.zeros((3 * nB, C), F32)
            h_car[...] = jnp.zeros((nB, C), F32)
        _load_rows(z_buf, stsc_ref, SC_CONV - 1, nB)

    if x_bm:
        xt_buf[...] = jnp.swapaxes(x_ref[...], 0, 1)
        x_tm = xt_buf
    else:
        x_tm = x_ref
    hn = _modulated_norm(x_tm[...], row(ng_ref), sc_ref[...], sh_ref[...])
    hn_buf[...] = hn.reshape(R, D_MODEL).astype(BF16)

    in_proj = lambda k: _dot(hn_buf[...], win_ref[:, k * C:(k + 1) * C])

    xl_buf[3 * nB:3 * nB + R] = in_proj(0)
    xc = row(cb_ref) + tap(cw_ref, 0) * xl_buf[0:R]
    for k in range(1, LRU_CONV):
        xc = xc + tap(cw_ref, k) * xl_buf[k * nB:k * nB + R]

    neg_c_sp = -RG_C * jnp.logaddexp(-row(lam_ref), 0.0)
    xc_bf = xc.astype(BF16)
    for hf in range(C // MXU_COLS):
        lo, hi = hf * MXU_COLS, (hf + 1) * MXU_COLS
        gates = _dot(xc_bf[:, lo:hi], wg_ref[hf])
        r = jax.nn.sigmoid(gates[:, :MXU_COLS] + row(ba_ref)[:, lo:hi])
        ig = jax.nn.sigmoid(gates[:, MXU_COLS:] + row(bx_ref)[:, lo:hi])
        log_a = r * neg_c_sp[:, lo:hi]
        a = jnp.exp(log_a)
        mult = jnp.sqrt(-jnp.tanh(log_a) * (1.0 + a * a))
        a_buf[:, lo:hi] = a
        b_buf[:, lo:hi] = mult * (ig * xc[:, lo:hi])

        h = h_car[:, lo:hi]
        for t in range(tT):
            rows = pl.ds(t * nB, nB)
            h = a_buf[rows, lo:hi] * h + b_buf[rows, lo:hi]
            b_buf[rows, lo:hi] = h
        h_car[:, lo:hi] = h

    mix_buf[:, 0:C] = (b_buf[...] * jax.nn.gelu(in_proj(1))).astype(BF16)

    z_buf[2 * nB:2 * nB + R] = in_proj(3) * in_proj(4)
    zc = tap(scw_ref, 0) * z_buf[0:R]
    for k in range(1, SC_CONV):
        zc = zc + tap(scw_ref, k) * z_buf[k * nB:k * nB + R]
    mix_buf[:, C:2 * C] = (in_proj(2) * zc).astype(BF16)

    mix = _dot(mix_buf[...], wout_ref[...])
    y_ref[...] = x_tm[...] + g_ref[...][None] * mix.reshape(tT, nB, D_MODEL)

    lc_tail = xl_buf[R:R + 3 * nB]
    sc_tail = z_buf[R:R + 2 * nB]
    _keep_previous(prev, (nh_ref, nlc_ref, nsc_ref))
    nh_ref[layer] = h_car[...]
    nlc_ref[layer] = lc_tail.reshape(LRU_CONV - 1, nB, C)
    _store_rows(nsc_ref, layer, sc_tail, SC_CONV - 1, nB)
    xl_buf[0:3 * nB] = lc_tail
    z_buf[0:2 * nB] = sc_tail

    _cast_riders(ride_in, ride_out)


_MIX_SMALL = ("norm1_g", "lru_conv_w", "lru_conv_b", "lru_ba", "lru_bx", "lru_lambda", "sc_conv_w")


def _mix_call(l, x, mod, state, prev, small, w_gate, big, riders, *, tT, nB, x_bm):
    if x_bm:
        B, T, _ = x.shape
        x_spec = pl.BlockSpec((nB, tT, D_MODEL), lambda i, j: (i, j, 0))
    else:
        T, B, _ = x.shape
        x_spec = pl.BlockSpec((tT, nB, D_MODEL), lambda i, j: (j, i, 0))
    R = tT * nB
    C = D_HALF
    n_steps = T // tT
    assert not riders or B == nB
    whole = lambda i, j: (0, 0)
    mod_spec = lambda k: pl.BlockSpec((None, nB, D_MODEL), lambda i, j: (l, i, k))
    once = pl.Buffered(1)
    h_spec = lambda n: pl.BlockSpec((n, nB, C), lambda i, j: (0, i, 0))
    lc_spec = lambda n: pl.BlockSpec((n, LRU_CONV - 1, nB, C), lambda i, j: (0, 0, i, 0))
    sc_spec = lambda n: pl.BlockSpec((n, nB, SC_CONV - 1, C), lambda i, j: (0, i, 0, 0))
    state_specs, state_args = [], []
    if state is not None:
        state_specs = [
            pl.BlockSpec((None, nB, C), lambda i, j: (l, i, 0)),
            pl.BlockSpec((None, LRU_CONV - 1, nB, C), lambda i, j: (l, 0, i, 0)),
            pl.BlockSpec((None, nB, SC_CONV - 1, C), lambda i, j: (l, i, 0, 0)),
        ]
        state_args = list(state)
    prev_specs = [h_spec(l), lc_spec(l), sc_spec(l)] if l else []
    r_in, r_out, r_shapes = _rider_specs(riders, n_steps, lambda i, j: j)
    scratch = [
        pltpu.VMEM((R, D_MODEL), BF16),
        pltpu.VMEM((R + 3 * nB, C), F32),
        pltpu.VMEM((R + 2 * nB, C), F32),
        pltpu.VMEM((R, C), F32),
        pltpu.VMEM((R, C), F32),
        pltpu.VMEM((R, D_MODEL), BF16),
        pltpu.VMEM((nB, C), F32),
    ]
    if x_bm:
        scratch.append(pltpu.VMEM((tT, nB, D_MODEL), F32))
    outs = pl.pallas_call(
        functools.partial(_mix_kernel, layer=l, tT=tT, nB=nB, x_bm=x_bm,
                          has_state=state is not None, n_ride=len(riders)),
        grid=(B // nB, n_steps),
        in_specs=[
            x_spec, mod_spec(0), mod_spec(1), mod_spec(2),
            *[_whole_spec(small[name]) for name in _MIX_SMALL],
            *state_specs, *prev_specs,
            pl.BlockSpec((D_MODEL, 5 * C), whole, pipeline_mode=once),
            pl.BlockSpec((None, C // MXU_COLS, MXU_COLS, 2 * MXU_COLS),
                         lambda i, j: (l, 0, 0, 0), pipeline_mode=once),
            pl.BlockSpec((D_MODEL, D_MODEL), whole, pipeline_mode=once),
            *r_in,
        ],
        out_specs=[
            pl.BlockSpec((tT, nB, D_MODEL), lambda i, j: (j, i, 0)),
            h_spec(l + 1), lc_spec(l + 1), sc_spec(l + 1),
            *r_out,
        ],
        out_shape=[
            jax.ShapeDtypeStruct((T, B, D_MODEL), F32),
            jax.ShapeDtypeStruct((l + 1, B, C), F32),
            jax.ShapeDtypeStruct((l + 1, LRU_CONV - 1, B, C), F32),
            jax.ShapeDtypeStruct((l + 1, B, SC_CONV - 1, C), F32),
            *r_shapes,
        ],
        scratch_shapes=scratch,
        compiler_params=pltpu.CompilerParams(
            dimension_semantics=("arbitrary", "arbitrary"),
            vmem_limit_bytes=VMEM_LIMIT_BYTES),
        name="mix",
    )(x, mod, mod, mod, *[small[name] for name in _MIX_SMALL], *state_args, *prev,
      big["w_in"], w_gate, big["w_out"], *[src for src, _ in riders])
    return outs[:4], outs[4:]


def _ffn_kernel(*refs, layer, tT, nB, final, has_state, guest, n_ride):
    refs = list(refs)
    x_ref, sh_ref, sc_ref, g_ref, ng_ref, cw_ref, fg_ref = _take(refs, 7)
    stfc_ref = _take(refs, 1)[0] if has_state else None
    prev = _take(refs, 1) if layer else []
    if guest:
        xg_ref, shg_ref, scg_ref, gg_ref, stg_ref = _take(refs, 5)
        prev_g = _take(refs, 1) if layer else []
    wup_ref, wdn_ref = _take(refs, 2)
    ride_in = _take(refs, n_ride)
    y_ref, nfc_ref = _take(refs, 2)
    if guest:
        yg_ref, nfcg_ref = _take(refs, 2)
    ride_out = _take(refs, n_ride)
    hn_buf, u_buf, u_car, act_buf = _take(refs, 4)

    R = tT * nB
    tTg, nBg = guest if guest else (0, 0)
    Rg = tTg * nBg
    j = pl.program_id(1)
    gain = ng_ref[layer:layer + 1, :]

    @pl.when(j == 0)
    def _():
        _load_rows(u_car, stfc_ref, FFN_CONV - 1, nB)

    hn = _modulated_norm(x_ref[...], gain, sc_ref[...], sh_ref[...])
    hn_buf[0:R] = hn.reshape(R, D_MODEL).astype(BF16)
    if guest:
        ug_buf = refs[0]
        hn_g = _modulated_norm(xg_ref[...], gain, scg_ref[...], shg_ref[...])
        hn_buf[R:R + Rg] = hn_g.reshape(Rg, D_MODEL).astype(BF16)

    def conv_act(ub, n, rows, v, tap):
        uc = tap(0) * ub[0:rows]
        for k in range(1, FFN_CONV):
            uc = uc + tap(k) * ub[k * n:k * n + rows]
        return (jax.nn.gelu(uc) * v).astype(BF16)

    for ci, c0 in enumerate(range(0, D_FF, MXU_COLS)):
        cols = slice(c0, c0 + MXU_COLS)
        tap = lambda k: cw_ref[layer, k:k + 1, cols]
        u = _dot(hn_buf[...], wup_ref[:, cols])
        v = _dot(hn_buf[...], wup_ref[:, D_FF + c0:D_FF + c0 + MXU_COLS])
        ub = u_buf.at[ci % 2]
        ub[0:2 * nB] = u_car[:, cols]
        ub[2 * nB:2 * nB + R] = u[0:R]
        act_buf[0:R, cols] = conv_act(ub, nB, R, v[0:R], tap)
        u_car[:, cols] = ub[R:R + 2 * nB]
        if guest:
            ubg = ug_buf.at[ci % 2]
            for k in range(FFN_CONV - 1):
                ubg[k * nBg:(k + 1) * nBg] = stg_ref[:, k, cols]
            ubg[2 * nBg:2 * nBg + Rg] = u[R:R + Rg]
            act_buf[R:R + Rg, cols] = conv_act(ubg, nBg, Rg, v[R:R + Rg], tap)
            for k in range(FFN_CONV - 1):
                nfcg_ref[layer, :, k, cols] = ubg[Rg + k * nBg:Rg + (k + 1) * nBg]

    out = _dot(act_buf[...], wdn_ref[...])

    def finish(x_t, g_t, out_t, t_n, b_n, dst):
        xn = x_t + g_t[None] * out_t.reshape(t_n, b_n, D_MODEL)
        if final:
            ms = jnp.mean(xn * xn, axis=-1, keepdims=True)
            xn = xn * jax.lax.rsqrt(ms + EPS) * fg_ref[...][None]
            dst[...] = jnp.swapaxes(xn, 0, 1)
        else:
            dst[...] = xn

    finish(x_ref[...], g_ref[...], out[0:R], tT, nB, y_ref)
    if guest:
        finish(xg_ref[...], gg_ref[...], out[R:R + Rg], tTg, nBg, yg_ref)
        _keep_previous(prev_g, (nfcg_ref,))

    _keep_previous(prev, (nfc_ref,))
    _store_rows(nfc_ref, layer, u_car[...], FFN_CONV - 1, nB)
    _cast_riders(ride_in, ride_out)


def _ffn_call(l, x, mod, st_fc, prev, small, big, riders, *, tT, nB, final, guest=None):
    T, B, _ = x.shape
    R = tT * nB
    n_steps = T // tT
    assert not riders or B == nB
    guest_specs, guest_args, guest_out_specs, guest_out_shapes, guest_dims = [], [], [], [], None
    Rg = 0
    if guest is not None:
        xg, modg, stg, prevg = guest
        tTg, Bg, _ = xg.shape
        assert B == nB and Bg % n_steps == 0
        nBg = Bg // n_steps
        guest_dims, Rg = (tTg, nBg), tTg * nBg
        gmod = lambda k: pl.BlockSpec((None, nBg, D_MODEL), lambda i, j: (l, j, k))
        gfc = lambda n: pl.BlockSpec((n, nBg, FFN_CONV - 1, D_FF), lambda i, j: (0, j, 0, 0))
        guest_specs = [
            pl.BlockSpec((tTg, nBg, D_MODEL), lambda i, j: (0, j, 0)), gmod(3), gmod(4), gmod(5),
            pl.BlockSpec((None, nBg, FFN_CONV - 1, D_FF), lambda i, j: (l, j, 0, 0)),
            *([gfc(l)] if l else [])]
        guest_args = [xg, modg, modg, modg, stg, *prevg]
        if final:
            guest_out_specs = [pl.BlockSpec((nBg, tTg, D_MODEL), lambda i, j: (j, 0, 0))]
            guest_out_shapes = [jax.ShapeDtypeStruct((Bg, tTg, D_MODEL), F32)]
        else:
            guest_out_specs = [pl.BlockSpec((tTg, nBg, D_MODEL), lambda i, j: (0, j, 0))]
            guest_out_shapes = [jax.ShapeDtypeStruct((tTg, Bg, D_MODEL), F32)]
        guest_out_specs.append(gfc(l + 1))
        guest_out_shapes.append(jax.ShapeDtypeStruct((l + 1, Bg, FFN_CONV - 1, D_FF), F32))
    whole = lambda i, j: (0, 0)
    mod_spec = lambda k: pl.BlockSpec((None, nB, D_MODEL), lambda i, j: (l, i, k))
    once = pl.Buffered(1)
    fc_spec = lambda n: pl.BlockSpec((n, nB, FFN_CONV - 1, D_FF), lambda i, j: (0, i, 0, 0))
    state_specs, state_args = [], []
    if st_fc is not None:
        state_specs = [pl.BlockSpec((None, nB, FFN_CONV - 1, D_FF), lambda i, j: (l, i, 0, 0))]
        state_args = [st_fc]
    prev_specs = [fc_spec(l)] if l else []
    if final:
        y_spec = pl.BlockSpec((nB, tT, D_MODEL), lambda i, j: (i, j, 0))
        y_shape = jax.ShapeDtypeStruct((B, T, D_MODEL), F32)
    else:
        y_spec = pl.BlockSpec((tT, nB, D_MODEL), lambda i, j: (j, i, 0))
        y_shape = jax.ShapeDtypeStruct((T, B, D_MODEL), F32)
    r_in, r_out, r_shapes = _rider_specs(riders, n_steps, lambda i, j: j)
    outs = pl.pallas_call(
        functools.partial(_ffn_kernel, layer=l, tT=tT, nB=nB, final=final,
                          has_state=st_fc is not None, guest=guest_dims, n_ride=len(riders)),
        grid=(B // nB, n_steps),
        in_specs=[
            pl.BlockSpec((tT, nB, D_MODEL), lambda i, j: (j, i, 0)),
            mod_spec(3), mod_spec(4), mod_spec(5),
            _whole_spec(small["norm2_g"]), _whole_spec(small["ffn_conv_w"]),
            _whole_spec(small["final_g"]),
            *state_specs, *prev_specs, *guest_specs,
            pl.BlockSpec((D_MODEL, 2 * D_FF), whole, pipeline_mode=once),
            pl.BlockSpec((D_FF, D_MODEL), whole, pipeline_mode=once),
            *r_in,
        ],
        out_specs=[y_spec, fc_spec(l + 1), *guest_out_specs, *r_out],
        out_shape=[
            y_shape,
            jax.ShapeDtypeStruct((l + 1, B, FFN_CONV - 1, D_FF), F32),
            *guest_out_shapes,
            *r_shapes,
        ],
        scratch_shapes=[
            pltpu.VMEM((R + Rg, D_MODEL), BF16),
            pltpu.VMEM((2, R + 2 * nB, MXU_COLS), F32),
            pltpu.VMEM((2 * nB, D_FF), F32),
            pltpu.VMEM((R + Rg, D_FF), BF16),
            *([pltpu.VMEM((2, Rg + 2 * guest_dims[1], MXU_COLS), F32)] if guest else []),
        ],
        compiler_params=pltpu.CompilerParams(
            dimension_semantics=("arbitrary", "arbitrary"),
            vmem_limit_bytes=VMEM_LIMIT_BYTES),
        name="ffn",
    )(x, mod, mod, mod, small["norm2_g"], small["ffn_conv_w"], small["final_g"],
      *state_args, *prev, *guest_args, big["w_up"], big["w_dn"], *[src for src, _ in riders])
    n_out = 2 + len(guest_out_specs)
    return outs[:n_out], outs[n_out:]


def _gate_weights(lru_wa, lru_wx):
    eye = jnp.eye(LRU_HEADS, dtype=lru_wa.dtype)
    dense = lambda w: jnp.einsum("lhij,hk->lhikj", w, eye).reshape(DEPTH, D_HALF, D_HALF)
    wa, wx = dense(lru_wa), dense(lru_wx)
    ws = []
    for hf in range(D_HALF // MXU_COLS):
        s = slice(hf * MXU_COLS, (hf + 1) * MXU_COLS)
        ws.append(jnp.concatenate([wa[:, s, s], wx[:, s, s]], axis=-1))
    return jnp.stack(ws, axis=1).astype(BF16)


def kernel(x_prompt, x_sample, c_prompt, c_sample, state_lru_h, state_lru_conv, state_sc_conv, state_ffn_conv, w_ada, b_ada, norm1_g, norm2_g, w_in, lru_conv_w, lru_conv_b, lru_wa, lru_ba, lru_wx, lru_bx, lru_lambda, sc_conv_w, w_out, ffn_w_up, ffn_conv_w, ffn_w_down, final_g):
    small = dict(
        norm1_g=norm1_g, norm2_g=norm2_g, final_g=final_g[None, :], lru_conv_w=lru_conv_w,
        lru_conv_b=lru_conv_b, lru_ba=lru_ba, lru_bx=lru_bx, lru_lambda=lru_lambda,
        sc_conv_w=sc_conv_w, ffn_conv_w=ffn_conv_w)
    w_gate = _gate_weights(lru_wa, lru_wx)

    (mod_p, mod_s), cast = _ada_call(c_prompt, c_sample, w_ada, b_ada,
                                     [(w_in, 0), (w_out, 0)])
    w_bf = dict(w_in=cast[0], w_out=cast[1])

    tile_p = dict(tT=128, nB=x_prompt.shape[0])
    tile_s = dict(tT=x_sample.shape[1], nB=64)
    x_p, x_s = x_prompt, x_sample
    mix_state_s = (state_lru_h, jnp.swapaxes(state_lru_conv, 1, 2), state_sc_conv)
    mix_p, mix_s, fc_p, fc_s = [], [], [], []
    for l in range(DEPTH):
        (x_p, *mix_p), cast = _mix_call(l, x_p, mod_p, None, mix_p, small, w_gate, w_bf,
                                        [(ffn_w_up, l), (ffn_w_down, l)], x_bm=(l == 0), **tile_p)
        (x_s, *mix_s), _ = _mix_call(l, x_s, mod_s, mix_state_s, mix_s, small, w_gate, w_bf,
                                     [], x_bm=(l == 0), **tile_s)
        w_bf = dict(w_up=cast[0], w_dn=cast[1])
        (x_p, nfc_p, x_s, nfc_s), cast = _ffn_call(
            l, x_p, mod_p, None, fc_p, small, w_bf,
            [(w_in, l + 1), (w_out, l + 1)] if l + 1 < DEPTH else [],
            final=(l == DEPTH - 1), guest=(x_s, mod_s, state_ffn_conv, fc_s), **tile_p)
        fc_p, fc_s = [nfc_p], [nfc_s]
        if cast:
            w_bf = dict(w_in=cast[0], w_out=cast[1])
    states = [(nh, jnp.swapaxes(nlc, 1, 2), nsc, nfc)
              for (nh, nlc, nsc), (nfc,) in ((mix_p, fc_p), (mix_s, fc_s))]
    return (x_p, x_s) + states[0] + states[1]
```

```python
import functools

import jax
import jax.numpy as jnp
from jax.experimental import pallas as pl
from jax.experimental.pallas import tpu as pltpu

D_MODEL = 1024
DEPTH = 2
D_HALF = 512
LRU_HEADS = 8
LRU_HEAD_DIM = 64
LRU_CONV = 4
SC_CONV = 3
FFN_CONV = 3
RG_C = 8.0
D_FF = 2816
EPS = 1e-6
N_MOD = 6

MXU_COLS = 256
BF16_SUBLANES = 16
VMEM_LIMIT_BYTES = 56 * 1024 * 1024
ADA_COL_BLOCKS = 4

BF16 = jnp.bfloat16
F32 = jnp.float32


def _dot(a, b):
    return jnp.dot(a, b, preferred_element_type=F32)


def _take(refs, n):
    head = refs[:n]
    del refs[:n]
    return head


def _whole_spec(arr):
    return pl.BlockSpec(arr.shape, lambda *g: (0,) * arr.ndim)


def _modulated_norm(x3, gain, scale, shift):
    ms = jnp.mean(x3 * x3, axis=-1, keepdims=True)
    y = x3 * jax.lax.rsqrt(ms + EPS)
    return y * (gain * (1.0 + scale))[None] + shift[None]


def _load_rows(buf, st_ref, n_rows, nB):
    if st_ref is None:
        buf[0:n_rows * nB] = jnp.zeros((n_rows * nB, buf.shape[1]), buf.dtype)
    else:
        for k in range(n_rows):
            buf[k * nB:(k + 1) * nB] = st_ref[:, k, :]


def _store_rows(out_ref, layer, tail, n_rows, nB):
    for k in range(n_rows):
        out_ref[layer, :, k, :] = tail[k * nB:(k + 1) * nB]


def _keep_previous(prev_refs, out_refs):
    for prev, out in zip(prev_refs, out_refs):
        out[0:prev.shape[0]] = prev[...]


def _rider_specs(riders, n_steps, step_of):
    in_specs, out_specs, out_shapes = [], [], []
    for src, layer in riders:
        _, K, N = src.shape
        hold = 1
        while (K * hold) % n_steps or (K * hold // n_steps) % BF16_SUBLANES:
            hold *= 2
            assert hold <= n_steps, (K, n_steps)
        rows = K * hold // n_steps
        in_specs.append(pl.BlockSpec(
            (None, rows, N), lambda *g, l=layer, h=hold: (l, step_of(*g) // h, 0)))
        out_specs.append(pl.BlockSpec((rows, N), lambda *g, h=hold: (step_of(*g) // h, 0)))
        out_shapes.append(jax.ShapeDtypeStruct((K, N), BF16))
    return in_specs, out_specs, out_shapes


def _cast_riders(ride_in, ride_out):
    for src, dst in zip(ride_in, ride_out):
        dst[...] = src[...].astype(BF16)


def _ada_kernel(*refs, n_ride):
    refs = list(refs)
    cp_ref, cs_ref, w_ref, b_ref = _take(refs, 4)
    ride_in = _take(refs, n_ride)
    op_ref, os_ref = _take(refs, 2)
    ride_out = _take(refs, n_ride)
    w = w_ref[...].astype(BF16)
    b = b_ref[pl.ds(pl.program_id(0) // ADA_COL_BLOCKS, 1), :]
    op_ref[...] = _dot(jax.nn.silu(cp_ref[...]).astype(BF16), w) + b
    os_ref[...] = _dot(jax.nn.silu(cs_ref[...]).astype(BF16), w) + b
    _cast_riders(ride_in, ride_out)


def _ada_call(c_prompt, c_sample, w_ada, b_ada, riders):
    n_p, n_s = c_prompt.shape[0], c_sample.shape[0]
    n_steps = DEPTH * ADA_COL_BLOCKS
    cols = N_MOD * D_MODEL // ADA_COL_BLOCKS
    lk = lambda s: (s // ADA_COL_BLOCKS, 0, s % ADA_COL_BLOCKS)
    c_spec = lambda n: pl.BlockSpec((n, D_MODEL), lambda s: (0, 0))
    o_spec = lambda n: pl.BlockSpec((None, n, cols), lk)
    r_in, r_out, r_shapes = _rider_specs(riders, n_steps, lambda s: s)
    outs = pl.pallas_call(
        functools.partial(_ada_kernel, n_ride=len(riders)),
        grid=(n_steps,),
        in_specs=[c_spec(n_p), c_spec(n_s),
                  pl.BlockSpec((None, D_MODEL, cols), lk),
                  pl.BlockSpec((DEPTH, cols), lambda s: (0, s % ADA_COL_BLOCKS)),
                  *r_in],
        out_specs=[o_spec(n_p), o_spec(n_s), *r_out],
        out_shape=[jax.ShapeDtypeStruct((DEPTH, n_p, N_MOD * D_MODEL), F32),
                   jax.ShapeDtypeStruct((DEPTH, n_s, N_MOD * D_MODEL), F32),
                   *r_shapes],
        name="ada",
    )(c_prompt, c_sample, w_ada, b_ada, *[src for src, _ in riders])
    return outs[:2], outs[2:]


def _mix_kernel(*refs, layer, tT, nB, x_bm, has_state, guest, n_ride):
    refs = list(refs)
    x_ref, sh_ref, sc_ref, g_ref = _take(refs, 4)
    ng_ref, cw_ref, cb_ref, ba_ref, bx_ref, lam_ref, scw_ref = _take(refs, 7)
    sth_ref, stlc_ref, stsc_ref = _take(refs, 3) if has_state else (None, None, None)
    prev = _take(refs, 3) if layer else []
    if guest:
        xg_ref, shg_ref, scg_ref, gg_ref, sthg_ref, stlcg_ref, stscg_ref = _take(refs, 7)
        prev_g = _take(refs, 3) if layer else []
    win_ref, wg_ref, wout_ref = _take(refs, 3)
    ride_in = _take(refs, n_ride)
    y_ref, nh_ref, nlc_ref, nsc_ref = _take(refs, 4)
    if guest:
        yg_ref, nhg_ref, nlcg_ref, nscg_ref = _take(refs, 4)
    ride_out = _take(refs, n_ride)
    hn_buf, xl_buf, z_buf, a_buf, b_buf, mix_buf, h_car = _take(refs, 7)
    xt_buf = refs.pop(0) if x_bm else None
    xlg_buf, zg_buf = refs if guest else (None, None)

    R = tT * nB
    tTg, nBg = guest if guest else (0, 0)
    Rg = tTg * nBg
    Rt = R + Rg
    C = D_HALF
    j = pl.program_id(1)
    row = lambda ref: ref[layer:layer + 1, :]
    tap = lambda ref, k: ref[layer, k:k + 1, :]

    def conv(buf, w_ref, taps, n, rows, bias=None):
        acc = tap(w_ref, 0) * buf[0:rows]
        if bias is not None:
            acc = bias + acc
        for k in range(1, taps):
            acc = acc + tap(w_ref, k) * buf[k * n:k * n + rows]
        return acc

    @pl.when(j == 0)
    def _():
        if has_state:
            xl_buf[0:3 * nB] = stlc_ref[...].reshape(3 * nB, C)
            h_car[...] = sth_ref[...]
        else:
            xl_buf[0:3 * nB] = jnp.zeros((3 * nB, C), F32)
            h_car[...] = jnp.zeros((nB, C), F32)
        _load_rows(z_buf, stsc_ref, SC_CONV - 1, nB)

    if x_bm:
        xt_buf[...] = jnp.swapaxes(x_ref[...], 0, 1)
        x_tm = xt_buf
    else:
        x_tm = x_ref
    hn = _modulated_norm(x_tm[...], row(ng_ref), sc_ref[...], sh_ref[...])
    hn_buf[0:R] = hn.reshape(R, D_MODEL).astype(BF16)
    if guest:
        xg_tm = jnp.swapaxes(xg_ref[...], 0, 1) if x_bm else xg_ref[...]
        hn_g = _modulated_norm(xg_tm, row(ng_ref), scg_ref[...], shg_ref[...])
        hn_buf[R:Rt] = hn_g.reshape(Rg, D_MODEL).astype(BF16)

    in_proj = lambda k: _dot(hn_buf[...], win_ref[:, k * C:(k + 1) * C])

    xl = in_proj(0)
    xl_buf[3 * nB:3 * nB + R] = xl[0:R]
    b_buf[0:R] = conv(xl_buf, cw_ref, LRU_CONV, nB, R, row(cb_ref))
    if guest:
        xlg_buf[0:3 * nBg] = stlcg_ref[...].reshape(3 * nBg, C)
        xlg_buf[3 * nBg:3 * nBg + Rg] = xl[R:Rt]
        b_buf[R:Rt] = conv(xlg_buf, cw_ref, LRU_CONV, nBg, Rg, row(cb_ref))

    neg_c_sp = -RG_C * jnp.logaddexp(-row(lam_ref), 0.0)
    for hf in range(C // MXU_COLS):
        lo, hi = hf * MXU_COLS, (hf + 1) * MXU_COLS
        xc = b_buf[:, lo:hi]
        gates = _dot(xc.astype(BF16), wg_ref[hf])
        r = jax.nn.sigmoid(gates[:, :MXU_COLS] + row(ba_ref)[:, lo:hi])
        ig = jax.nn.sigmoid(gates[:, MXU_COLS:] + row(bx_ref)[:, lo:hi])
        log_a = r * neg_c_sp[:, lo:hi]
        a = jnp.exp(log_a)
        mult = jnp.sqrt(-jnp.tanh(log_a) * (1.0 + a * a))
        a_buf[:, lo:hi] = a
        b_buf[:, lo:hi] = mult * (ig * xc)

        def recur(h, first, steps, n):
            for t in range(steps):
                rows = pl.ds(first + t * n, n)
                h = a_buf[rows, lo:hi] * h + b_buf[rows, lo:hi]
                b_buf[rows, lo:hi] = h
            return h
        h_car[:, lo:hi] = recur(h_car[:, lo:hi], 0, tT, nB)
        if guest:
            nhg_ref[layer, :, lo:hi] = recur(sthg_ref[:, lo:hi], R, tTg, nBg)

    mix_buf[:, 0:C] = (b_buf[...] * jax.nn.gelu(in_proj(1))).astype(BF16)

    z = in_proj(3) * in_proj(4)
    bs = in_proj(2)
    z_buf[2 * nB:2 * nB + R] = z[0:R]
    mix_buf[0:R, C:2 * C] = (bs[0:R] * conv(z_buf, scw_ref, SC_CONV, nB, R)).astype(BF16)
    if guest:
        _load_rows(zg_buf, stscg_ref, SC_CONV - 1, nBg)
        zg_buf[2 * nBg:2 * nBg + Rg] = z[R:Rt]
        mix_buf[R:Rt, C:2 * C] = (bs[R:Rt] * conv(zg_buf, scw_ref, SC_CONV, nBg, Rg)).astype(BF16)

    mix = _dot(mix_buf[...], wout_ref[...])
    y_ref[...] = x_tm[...] + g_ref[...][None] * mix[0:R].reshape(tT, nB, D_MODEL)
    if guest:
        yg_ref[...] = xg_tm + gg_ref[...][None] * mix[R:Rt].reshape(tTg, nBg, D_MODEL)
        _keep_previous(prev_g, (nhg_ref, nlcg_ref, nscg_ref))
        nlcg_ref[layer] = xlg_buf[Rg:Rg + 3 * nBg].reshape(LRU_CONV - 1, nBg, C)
        _store_rows(nscg_ref, layer, zg_buf[Rg:Rg + 2 * nBg], SC_CONV - 1, nBg)

    lc_tail = xl_buf[R:R + 3 * nB]
    sc_tail = z_buf[R:R + 2 * nB]
    _keep_previous(prev, (nh_ref, nlc_ref, nsc_ref))
    nh_ref[layer] = h_car[...]
    nlc_ref[layer] = lc_tail.reshape(LRU_CONV - 1, nB, C)
    _store_rows(nsc_ref, layer, sc_tail, SC_CONV - 1, nB)
    xl_buf[0:3 * nB] = lc_tail
    z_buf[0:2 * nB] = sc_tail

    _cast_riders(ride_in, ride_out)


_MIX_SMALL = ("norm1_g", "lru_conv_w", "lru_conv_b", "lru_ba", "lru_bx", "lru_lambda", "sc_conv_w")


def _mix_call(l, x, mod, state, prev, small, w_gate, big, riders, *, tT, nB, x_bm, guest=None):
    if x_bm:
        B, T, _ = x.shape
        x_spec = pl.BlockSpec((nB, tT, D_MODEL), lambda i, j: (i, j, 0))
    else:
        T, B, _ = x.shape
        x_spec = pl.BlockSpec((tT, nB, D_MODEL), lambda i, j: (j, i, 0))
    R = tT * nB
    C = D_HALF
    n_steps = T // tT
    assert not riders or B == nB
    guest_specs, guest_args, guest_out_specs, guest_out_shapes, guest_dims = [], [], [], [], None
    Rg = 0
    if guest is not None:
        xg, modg, stateg, prevg = guest
        Bg, tTg = (xg.shape[0], xg.shape[1]) if x_bm else (xg.shape[1], xg.shape[0])
        assert B == nB and Bg % n_steps == 0
        nBg = Bg // n_steps
        guest_dims, Rg = (tTg, nBg), tTg * nBg
        gx = (pl.BlockSpec((nBg, tTg, D_MODEL), lambda i, j: (j, 0, 0)) if x_bm else
              pl.BlockSpec((tTg, nBg, D_MODEL), lambda i, j: (0, j, 0)))
        gmod = lambda k: pl.BlockSpec((None, nBg, D_MODEL), lambda i, j: (l, j, k))
        gh = lambda n, ll: pl.BlockSpec((n, nBg, C), lambda i, j: (ll, j, 0))
        glc = lambda n, ll: pl.BlockSpec((n, LRU_CONV - 1, nBg, C), lambda i, j: (ll, 0, j, 0))
        gsc = lambda n, ll: pl.BlockSpec((n, nBg, SC_CONV - 1, C), lambda i, j: (ll, j, 0, 0))
        guest_specs = [gx, gmod(0), gmod(1), gmod(2), gh(None, l), glc(None, l), gsc(None, l),
                       *([gh(l, 0), glc(l, 0), gsc(l, 0)] if l else [])]
        guest_args = [xg, modg, modg, modg, *stateg, *prevg]
        guest_out_specs = [pl.BlockSpec((tTg, nBg, D_MODEL), lambda i, j: (0, j, 0)),
                           gh(l + 1, 0), glc(l + 1, 0), gsc(l + 1, 0)]
        guest_out_shapes = [jax.ShapeDtypeStruct((tTg, Bg, D_MODEL), F32),
                            jax.ShapeDtypeStruct((l + 1, Bg, C), F32),
                            jax.ShapeDtypeStruct((l + 1, LRU_CONV - 1, Bg, C), F32),
                            jax.ShapeDtypeStruct((l + 1, Bg, SC_CONV - 1, C), F32)]
    whole = lambda i, j: (0, 0)
    mod_spec = lambda k: pl.BlockSpec((None, nB, D_MODEL), lambda i, j: (l, i, k))
    once = pl.Buffered(1)
    h_spec = lambda n: pl.BlockSpec((n, nB, C), lambda i, j: (0, i, 0))
    lc_spec = lambda n: pl.BlockSpec((n, LRU_CONV - 1, nB, C), lambda i, j: (0, 0, i, 0))
    sc_spec = lambda n: pl.BlockSpec((n, nB, SC_CONV - 1, C), lambda i, j: (0, i, 0, 0))
    state_specs, state_args = [], []
    if state is not None:
        state_specs = [
            pl.BlockSpec((None, nB, C), lambda i, j: (l, i, 0)),
            pl.BlockSpec((None, LRU_CONV - 1, nB, C), lambda i, j: (l, 0, i, 0)),
            pl.BlockSpec((None, nB, SC_CONV - 1, C), lambda i, j: (l, i, 0, 0)),
        ]
        state_args = list(state)
    prev_specs = [h_spec(l), lc_spec(l), sc_spec(l)] if l else []
    r_in, r_out, r_shapes = _rider_specs(riders, n_steps, lambda i, j: j)
    scratch = [
        pltpu.VMEM((R + Rg, D_MODEL), BF16),
        pltpu.VMEM((R + 3 * nB, C), F32),
        pltpu.VMEM((R + 2 * nB, C), F32),
        pltpu.VMEM((R + Rg, C), F32),
        pltpu.VMEM((R + Rg, C), F32),
        pltpu.VMEM((R + Rg, D_MODEL), BF16),
        pltpu.VMEM((nB, C), F32),
    ]
    if x_bm:
        scratch.append(pltpu.VMEM((tT, nB, D_MODEL), F32))
    if guest is not None:
        scratch += [pltpu.VMEM((Rg + (LRU_CONV - 1) * guest_dims[1], C), F32),
                    pltpu.VMEM((Rg + (SC_CONV - 1) * guest_dims[1], C), F32)]
    outs = pl.pallas_call(
        functools.partial(_mix_kernel, layer=l, tT=tT, nB=nB, x_bm=x_bm,
                          has_state=state is not None, guest=guest_dims, n_ride=len(riders)),
        grid=(B // nB, n_steps),
        in_specs=[
            x_spec, mod_spec(0), mod_spec(1), mod_spec(2),
            *[_whole_spec(small[name]) for name in _MIX_SMALL],
            *state_specs, *prev_specs, *guest_specs,
            pl.BlockSpec((D_MODEL, 5 * C), whole, pipeline_mode=once),
            pl.BlockSpec((None, C // MXU_COLS, MXU_COLS, 2 * MXU_COLS),
                         lambda i, j: (l, 0, 0, 0), pipeline_mode=once),
            pl.BlockSpec((D_MODEL, D_MODEL), whole, pipeline_mode=once),
            *r_in,
        ],
        out_specs=[
            pl.BlockSpec((tT, nB, D_MODEL), lambda i, j: (j, i, 0)),
            h_spec(l + 1), lc_spec(l + 1), sc_spec(l + 1),
            *guest_out_specs, *r_out,
        ],
        out_shape=[
            jax.ShapeDtypeStruct((T, B, D_MODEL), F32),
            jax.ShapeDtypeStruct((l + 1, B, C), F32),
            jax.ShapeDtypeStruct((l + 1, LRU_CONV - 1, B, C), F32),
            jax.ShapeDtypeStruct((l + 1, B, SC_CONV - 1, C), F32),
            *guest_out_shapes, *r_shapes,
        ],
        scratch_shapes=scratch,
        compiler_params=pltpu.CompilerParams(
            dimension_semantics=("arbitrary", "arbitrary"),
            vmem_limit_bytes=VMEM_LIMIT_BYTES),
        name="mix",
    )(x, mod, mod, mod, *[small[name] for name in _MIX_SMALL], *state_args, *prev, *guest_args,
      big["w_in"], w_gate, big["w_out"], *[src for src, _ in riders])
    n_out = 4 + len(guest_out_specs)
    return outs[:n_out], outs[n_out:]


def _ffn_kernel(*refs, layer, tT, nB, final, has_state, guest, n_ride):
    refs = list(refs)
    x_ref, sh_ref, sc_ref, g_ref, ng_ref, cw_ref, fg_ref = _take(refs, 7)
    stfc_ref = _take(refs, 1)[0] if has_state else None
    prev = _take(refs, 1) if layer else []
    if guest:
        xg_ref, shg_ref, scg_ref, gg_ref, stg_ref = _take(refs, 5)
        prev_g = _take(refs, 1) if layer else []
    wup_ref, wdn_ref = _take(refs, 2)
    ride_in = _take(refs, n_ride)
    y_ref, nfc_ref = _take(refs, 2)
    if guest:
        yg_ref, nfcg_ref = _take(refs, 2)
    ride_out = _take(refs, n_ride)
    hn_buf, u_buf, u_car, act_buf = _take(refs, 4)

    R = tT * nB
    tTg, nBg = guest if guest else (0, 0)
    Rg = tTg * nBg
    j = pl.program_id(1)
    gain = ng_ref[layer:layer + 1, :]

    @pl.when(j == 0)
    def _():
        _load_rows(u_car, stfc_ref, FFN_CONV - 1, nB)

    hn = _modulated_norm(x_ref[...], gain, sc_ref[...], sh_ref[...])
    hn_buf[0:R] = hn.reshape(R, D_MODEL).astype(BF16)
    if guest:
        ug_buf = refs[0]
        hn_g = _modulated_norm(xg_ref[...], gain, scg_ref[...], shg_ref[...])
        hn_buf[R:R + Rg] = hn_g.reshape(Rg, D_MODEL).astype(BF16)

    def conv_act(ub, n, rows, v, tap):
        uc = tap(0) * ub[0:rows]
        for k in range(1, FFN_CONV):
            uc = uc + tap(k) * ub[k * n:k * n + rows]
        return (jax.nn.gelu(uc) * v).astype(BF16)

    for ci, c0 in enumerate(range(0, D_FF, MXU_COLS)):
        cols = slice(c0, c0 + MXU_COLS)
        tap = lambda k: cw_ref[layer, k:k + 1, cols]
        u = _dot(hn_buf[...], wup_ref[:, cols])
        v = _dot(hn_buf[...], wup_ref[:, D_FF + c0:D_FF + c0 + MXU_COLS])
        ub = u_buf.at[ci % 2]
        ub[0:2 * nB] = u_car[:, cols]
        ub[2 * nB:2 * nB + R] = u[0:R]
        act_buf[0:R, cols] = conv_act(ub, nB, R, v[0:R], tap)
        u_car[:, cols] = ub[R:R + 2 * nB]
        if guest:
            ubg = ug_buf.at[ci % 2]
            for k in range(FFN_CONV - 1):
                ubg[k * nBg:(k + 1) * nBg] = stg_ref[:, k, cols]
            ubg[2 * nBg:2 * nBg + Rg] = u[R:R + Rg]
            act_buf[R:R + Rg, cols] = conv_act(ubg, nBg, Rg, v[R:R + Rg], tap)
            for k in range(FFN_CONV - 1):
                nfcg_ref[layer, :, k, cols] = ubg[Rg + k * nBg:Rg + (k + 1) * nBg]

    out = _dot(act_buf[...], wdn_ref[...])

    def finish(x_t, g_t, out_t, t_n, b_n, dst):
        xn = x_t + g_t[None] * out_t.reshape(t_n, b_n, D_MODEL)
        if final:
            ms = jnp.mean(xn * xn, axis=-1, keepdims=True)
            xn = xn * jax.lax.rsqrt(ms + EPS) * fg_ref[...][None]
            dst[...] = jnp.swapaxes(xn, 0, 1)
        else:
            dst[...] = xn

    finish(x_ref[...], g_ref[...], out[0:R], tT, nB, y_ref)
    if guest:
        finish(xg_ref[...], gg_ref[...], out[R:R + Rg], tTg, nBg, yg_ref)
        _keep_previous(prev_g, (nfcg_ref,))

    _keep_previous(prev, (nfc_ref,))
    _store_rows(nfc_ref, layer, u_car[...], FFN_CONV - 1, nB)
    _cast_riders(ride_in, ride_out)


def _ffn_call(l, x, mod, st_fc, prev, small, big, riders, *, tT, nB, final, guest=None):
    T, B, _ = x.shape
    R = tT * nB
    n_steps = T // tT
    assert not riders or B == nB
    guest_specs, guest_args, guest_out_specs, guest_out_shapes, guest_dims = [], [], [], [], None
    Rg = 0
    if guest is not None:
        xg, modg, stg, prevg = guest
        tTg, Bg, _ = xg.shape
        assert B == nB and Bg % n_steps == 0
        nBg = Bg // n_steps
        guest_dims, Rg = (tTg, nBg), tTg * nBg
        gmod = lambda k: pl.BlockSpec((None, nBg, D_MODEL), lambda i, j: (l, j, k))
        gfc = lambda n: pl.BlockSpec((n, nBg, FFN_CONV - 1, D_FF), lambda i, j: (0, j, 0, 0))
        guest_specs = [
            pl.BlockSpec((tTg, nBg, D_MODEL), lambda i, j: (0, j, 0)), gmod(3), gmod(4), gmod(5),
            pl.BlockSpec((None, nBg, FFN_CONV - 1, D_FF), lambda i, j: (l, j, 0, 0)),
            *([gfc(l)] if l else [])]
        guest_args = [xg, modg, modg, modg, stg, *prevg]
        if final:
            guest_out_specs = [pl.BlockSpec((nBg, tTg, D_MODEL), lambda i, j: (j, 0, 0))]
            guest_out_shapes = [jax.ShapeDtypeStruct((Bg, tTg, D_MODEL), F32)]
        else:
            guest_out_specs = [pl.BlockSpec((tTg, nBg, D_MODEL), lambda i, j: (0, j, 0))]
            guest_out_shapes = [jax.ShapeDtypeStruct((tTg, Bg, D_MODEL), F32)]
        guest_out_specs.append(gfc(l + 1))
        guest_out_shapes.append(jax.ShapeDtypeStruct((l + 1, Bg, FFN_CONV - 1, D_FF), F32))
    whole = lambda i, j: (0, 0)
    mod_spec = lambda k: pl.BlockSpec((None, nB, D_MODEL), lambda i, j: (l, i, k))
    once = pl.Buffered(1)
    fc_spec = lambda n: pl.BlockSpec((n, nB, FFN_CONV - 1, D_FF), lambda i, j: (0, i, 0, 0))
    state_specs, state_args = [], []
    if st_fc is not None:
        state_specs = [pl.BlockSpec((None, nB, FFN_CONV - 1, D_FF), lambda i, j: (l, i, 0, 0))]
        state_args = [st_fc]
    prev_specs = [fc_spec(l)] if l else []
    if final:
        y_spec = pl.BlockSpec((nB, tT, D_MODEL), lambda i, j: (i, j, 0))
        y_shape = jax.ShapeDtypeStruct((B, T, D_MODEL), F32)
    else:
        y_spec = pl.BlockSpec((tT, nB, D_MODEL), lambda i, j: (j, i, 0))
        y_shape = jax.ShapeDtypeStruct((T, B, D_MODEL), F32)
    r_in, r_out, r_shapes = _rider_specs(riders, n_steps, lambda i, j: j)
    outs = pl.pallas_call(
        functools.partial(_ffn_kernel, layer=l, tT=tT, nB=nB, final=final,
                          has_state=st_fc is not None, guest=guest_dims, n_ride=len(riders)),
        grid=(B // nB, n_steps),
        in_specs=[
            pl.BlockSpec((tT, nB, D_MODEL), lambda i, j: (j, i, 0)),
            mod_spec(3), mod_spec(4), mod_spec(5),
            _whole_spec(small["norm2_g"]), _whole_spec(small["ffn_conv_w"]),
            _whole_spec(small["final_g"]),
            *state_specs, *prev_specs, *guest_specs,
            pl.BlockSpec((D_MODEL, 2 * D_FF), whole, pipeline_mode=once),
            pl.BlockSpec((D_FF, D_MODEL), whole, pipeline_mode=once),
            *r_in,
        ],
        out_specs=[y_spec, fc_spec(l + 1), *guest_out_specs, *r_out],
        out_shape=[
            y_shape,
            jax.ShapeDtypeStruct((l + 1, B, FFN_CONV - 1, D_FF), F32),
            *guest_out_shapes,
            *r_shapes,
        ],
        scratch_shapes=[
            pltpu.VMEM((R + Rg, D_MODEL), BF16),
            pltpu.VMEM((2, R + 2 * nB, MXU_COLS), F32),
            pltpu.VMEM((2 * nB, D_FF), F32),
            pltpu.VMEM((R + Rg, D_FF), BF16),
            *([pltpu.VMEM((2, Rg + 2 * guest_dims[1], MXU_COLS), F32)] if guest else []),
        ],
        compiler_params=pltpu.CompilerParams(
            dimension_semantics=("arbitrary", "arbitrary"),
            vmem_limit_bytes=VMEM_LIMIT_BYTES),
        name="ffn",
    )(x, mod, mod, mod, small["norm2_g"], small["ffn_conv_w"], small["final_g"],
      *state_args, *prev, *guest_args, big["w_up"], big["w_dn"], *[src for src, _ in riders])
    n_out = 2 + len(guest_out_specs)
    return outs[:n_out], outs[n_out:]


def _gate_weights(lru_wa, lru_wx):
    eye = jnp.eye(LRU_HEADS, dtype=lru_wa.dtype)
    dense = lambda w: jnp.einsum("lhij,hk->lhikj", w, eye).reshape(DEPTH, D_HALF, D_HALF)
    wa, wx = dense(lru_wa), dense(lru_wx)
    ws = []
    for hf in range(D_HALF // MXU_COLS):
        s = slice(hf * MXU_COLS, (hf + 1) * MXU_COLS)
        ws.append(jnp.concatenate([wa[:, s, s], wx[:, s, s]], axis=-1))
    return jnp.stack(ws, axis=1).astype(BF16)


def kernel(x_prompt, x_sample, c_prompt, c_sample, state_lru_h, state_lru_conv, state_sc_conv, state_ffn_conv, w_ada, b_ada, norm1_g, norm2_g, w_in, lru_conv_w, lru_conv_b, lru_wa, lru_ba, lru_wx, lru_bx, lru_lambda, sc_conv_w, w_out, ffn_w_up, ffn_conv_w, ffn_w_down, final_g):
    small = dict(
        norm1_g=norm1_g, norm2_g=norm2_g, final_g=final_g[None, :], lru_conv_w=lru_conv_w,
        lru_conv_b=lru_conv_b, lru_ba=lru_ba, lru_bx=lru_bx, lru_lambda=lru_lambda,
        sc_conv_w=sc_conv_w, ffn_conv_w=ffn_conv_w)
    w_gate = _gate_weights(lru_wa, lru_wx)

    (mod_p, mod_s), cast = _ada_call(c_prompt, c_sample, w_ada, b_ada,
                                     [(w_in, 0), (w_out, 0)])
    w_bf = dict(w_in=cast[0], w_out=cast[1])

    tile_p = dict(tT=128, nB=x_prompt.shape[0])
    x_p, x_s = x_prompt, x_sample
    mix_state_s = (state_lru_h, jnp.swapaxes(state_lru_conv, 1, 2), state_sc_conv)
    mix_p, mix_s, fc_p, fc_s = [], [], [], []
    for l in range(DEPTH):
        (x_p, *mix_p, x_s, nh_s, nlc_s, nsc_s), cast = _mix_call(
            l, x_p, mod_p, None, mix_p, small, w_gate, w_bf, [(ffn_w_up, l), (ffn_w_down, l)],
            x_bm=(l == 0), guest=(x_s, mod_s, mix_state_s, mix_s), **tile_p)
        mix_s = [nh_s, nlc_s, nsc_s]
        w_bf = dict(w_up=cast[0], w_dn=cast[1])
        (x_p, nfc_p, x_s, nfc_s), cast = _ffn_call(
            l, x_p, mod_p, None, fc_p, small, w_bf,
            [(w_in, l + 1), (w_out, l + 1)] if l + 1 < DEPTH else [],
            final=(l == DEPTH - 1), guest=(x_s, mod_s, state_ffn_conv, fc_s), **tile_p)
        fc_p, fc_s = [nfc_p], [nfc_s]
        if cast:
            w_bf = dict(w_in=cast[0], w_out=cast[1])
    states = [(nh, jnp.swapaxes(nlc, 1, 2), nsc, nfc)
              for (nh, nlc, nsc), (nfc,) in ((mix_p, fc_p), (mix_s, fc_s))]
    return (x_p, x_s) + states[0] + states[1]
```

```python
import functools

import jax
import jax.numpy as jnp
from jax.experimental import pallas as pl
from jax.experimental.pallas import tpu as pltpu

D_MODEL = 1024
DEPTH = 2
D_HALF = 512
LRU_HEADS = 8
LRU_HEAD_DIM = 64
LRU_CONV = 4
SC_CONV = 3
FFN_CONV = 3
RG_C = 8.0
D_FF = 2816
EPS = 1e-6
N_MOD = 6

MXU_COLS = 256
BF16_SUBLANES = 16
VMEM_LIMIT_BYTES = 56 * 1024 * 1024
ADA_COL_BLOCKS = 4

BF16 = jnp.bfloat16
F32 = jnp.float32


def _dot(a, b):
    return jnp.dot(a, b, preferred_element_type=F32)


def _take(refs, n):
    head = refs[:n]
    del refs[:n]
    return head


def _whole_spec(arr):
    return pl.BlockSpec(arr.shape, lambda *g: (0,) * arr.ndim)


def _modulated_norm(x3, gain, scale, shift):
    ms = jnp.mean(x3 * x3, axis=-1, keepdims=True)
    y = x3 * jax.lax.rsqrt(ms + EPS)
    return y * (gain * (1.0 + scale))[None] + shift[None]


def _tie(buf, col0, value, never):
    rows, lanes = BF16_SUBLANES, 128
    fold = value[0:rows]
    for r0 in range(rows, value.shape[0], rows):
        fold = fold + value[r0:r0 + rows]
    dep = fold[:, 0:lanes]
    for c0 in range(lanes, fold.shape[1], lanes):
        dep = dep + fold[:, c0:c0 + lanes]
    tile = (slice(0, rows), slice(col0, col0 + lanes))
    buf[tile] = jnp.where(never, dep.astype(buf.dtype), buf[tile])


def _load_rows(buf, st_ref, n_rows, nB):
    if st_ref is None:
        buf[0:n_rows * nB] = jnp.zeros((n_rows * nB, buf.shape[1]), buf.dtype)
    else:
        for k in range(n_rows):
            buf[k * nB:(k + 1) * nB] = st_ref[:, k, :]


def _store_rows(out_ref, layer, tail, n_rows, nB):
    for k in range(n_rows):
        out_ref[layer, :, k, :] = tail[k * nB:(k + 1) * nB]


def _keep_previous(prev_refs, out_refs):
    for prev, out in zip(prev_refs, out_refs):
        out[0:prev.shape[0]] = prev[...]


def _rider_specs(riders, n_steps, step_of):
    in_specs, out_specs, out_shapes = [], [], []
    for src, layer in riders:
        _, K, N = src.shape
        hold = 1
        while (K * hold) % n_steps or (K * hold // n_steps) % BF16_SUBLANES:
            hold *= 2
            assert hold <= n_steps, (K, n_steps)
        rows = K * hold // n_steps
        in_specs.append(pl.BlockSpec(
            (None, rows, N), lambda *g, l=layer, h=hold: (l, step_of(*g) // h, 0)))
        out_specs.append(pl.BlockSpec((rows, N), lambda *g, h=hold: (step_of(*g) // h, 0)))
        out_shapes.append(jax.ShapeDtypeStruct((K, N), BF16))
    return in_specs, out_specs, out_shapes


def _cast_riders(ride_in, ride_out):
    for src, dst in zip(ride_in, ride_out):
        dst[...] = src[...].astype(BF16)


def _ada_kernel(*refs, n_ride):
    refs = list(refs)
    cp_ref, cs_ref, w_ref, b_ref = _take(refs, 4)
    ride_in = _take(refs, n_ride)
    op_ref, os_ref = _take(refs, 2)
    ride_out = _take(refs, n_ride)
    w = w_ref[...].astype(BF16)
    b = b_ref[pl.ds(pl.program_id(0) // ADA_COL_BLOCKS, 1), :]
    op_ref[...] = _dot(jax.nn.silu(cp_ref[...]).astype(BF16), w) + b
    os_ref[...] = _dot(jax.nn.silu(cs_ref[...]).astype(BF16), w) + b
    _cast_riders(ride_in, ride_out)


def _ada_call(c_prompt, c_sample, w_ada, b_ada, riders):
    n_p, n_s = c_prompt.shape[0], c_sample.shape[0]
    n_steps = DEPTH * ADA_COL_BLOCKS
    cols = N_MOD * D_MODEL // ADA_COL_BLOCKS
    lk = lambda s: (s // ADA_COL_BLOCKS, 0, s % ADA_COL_BLOCKS)
    c_spec = lambda n: pl.BlockSpec((n, D_MODEL), lambda s: (0, 0))
    o_spec = lambda n: pl.BlockSpec((None, n, cols), lk)
    r_in, r_out, r_shapes = _rider_specs(riders, n_steps, lambda s: s)
    outs = pl.pallas_call(
        functools.partial(_ada_kernel, n_ride=len(riders)),
        grid=(n_steps,),
        in_specs=[c_spec(n_p), c_spec(n_s),
                  pl.BlockSpec((None, D_MODEL, cols), lk),
                  pl.BlockSpec((DEPTH, cols), lambda s: (0, s % ADA_COL_BLOCKS)),
                  *r_in],
        out_specs=[o_spec(n_p), o_spec(n_s), *r_out],
        out_shape=[jax.ShapeDtypeStruct((DEPTH, n_p, N_MOD * D_MODEL), F32),
                   jax.ShapeDtypeStruct((DEPTH, n_s, N_MOD * D_MODEL), F32),
                   *r_shapes],
        name="ada",
    )(c_prompt, c_sample, w_ada, b_ada, *[src for src, _ in riders])
    return outs[:2], outs[2:]


def _mix_kernel(*refs, layer, tT, nB, x_bm, has_state, guest, n_ride):
    refs = list(refs)
    x_ref, sh_ref, sc_ref, g_ref = _take(refs, 4)
    ng_ref, cw_ref, cb_ref, ba_ref, bx_ref, lam_ref, scw_ref = _take(refs, 7)
    sth_ref, stlc_ref, stsc_ref = _take(refs, 3) if has_state else (None, None, None)
    prev = _take(refs, 3) if layer else []
    if guest:
        xg_ref, shg_ref, scg_ref, gg_ref, sthg_ref, stlcg_ref, stscg_ref = _take(refs, 7)
        prev_g = _take(refs, 3) if layer else []
    win_ref, wg_ref, wout_ref = _take(refs, 3)
    ride_in = _take(refs, n_ride)
    y_ref, nh_ref, nlc_ref, nsc_ref = _take(refs, 4)
    if guest:
        yg_ref, nhg_ref, nlcg_ref, nscg_ref = _take(refs, 4)
    ride_out = _take(refs, n_ride)
    hn_buf, xl_buf, z_buf, a_buf, b_buf, mix_buf, h_car = _take(refs, 7)
    xt_buf = refs.pop(0) if x_bm else None
    xlg_buf, zg_buf = refs if guest else (None, None)

    R = tT * nB
    tTg, nBg = guest if guest else (0, 0)
    Rg = tTg * nBg
    Rt = R + Rg
    C = D_HALF
    j = pl.program_id(1)
    row = lambda ref: ref[layer:layer + 1, :]
    tap = lambda ref, k: ref[layer, k:k + 1, :]

    def conv(buf, w_ref, taps, n, rows, bias=None):
        acc = tap(w_ref, 0) * buf[0:rows]
        if bias is not None:
            acc = bias + acc
        for k in range(1, taps):
            acc = acc + tap(w_ref, k) * buf[k * n:k * n + rows]
        return acc

    @pl.when(j == 0)
    def _():
        if has_state:
            xl_buf[0:3 * nB] = stlc_ref[...].reshape(3 * nB, C)
            h_car[...] = sth_ref[...]
        else:
            xl_buf[0:3 * nB] = jnp.zeros((3 * nB, C), F32)
            h_car[...] = jnp.zeros((nB, C), F32)
        _load_rows(z_buf, stsc_ref, SC_CONV - 1, nB)

    if x_bm:
        xt_buf[...] = jnp.swapaxes(x_ref[...], 0, 1)
        x_tm = xt_buf
    else:
        x_tm = x_ref
    hn = _modulated_norm(x_tm[...], row(ng_ref), sc_ref[...], sh_ref[...])
    hn_buf[0:R] = hn.reshape(R, D_MODEL).astype(BF16)
    if guest:
        xg_tm = jnp.swapaxes(xg_ref[...], 0, 1) if x_bm else xg_ref[...]
        hn_g = _modulated_norm(xg_tm, row(ng_ref), scg_ref[...], shg_ref[...])
        hn_buf[R:Rt] = hn_g.reshape(Rg, D_MODEL).astype(BF16)

    in_proj = lambda k: _dot(hn_buf[...], win_ref[:, k * C:(k + 1) * C])

    xl = in_proj(0)
    xl_buf[3 * nB:3 * nB + R] = xl[0:R]
    b_buf[0:R] = conv(xl_buf, cw_ref, LRU_CONV, nB, R, row(cb_ref))
    if guest:
        xlg_buf[0:3 * nBg] = stlcg_ref[...].reshape(3 * nBg, C)
        xlg_buf[3 * nBg:3 * nBg + Rg] = xl[R:Rt]
        b_buf[R:Rt] = conv(xlg_buf, cw_ref, LRU_CONV, nBg, Rg, row(cb_ref))

    neg_c_sp = -RG_C * jnp.logaddexp(-row(lam_ref), 0.0)
    short = []
    for hf in range(C // MXU_COLS):
        lo, hi = hf * MXU_COLS, (hf + 1) * MXU_COLS
        xc = b_buf[:, lo:hi]
        gates = _dot(xc.astype(BF16), wg_ref[hf])
        r = jax.nn.sigmoid(gates[:, :MXU_COLS] + row(ba_ref)[:, lo:hi])
        ig = jax.nn.sigmoid(gates[:, MXU_COLS:] + row(bx_ref)[:, lo:hi])
        log_a = r * neg_c_sp[:, lo:hi]
        a = jnp.exp(log_a)
        mult = jnp.sqrt(-jnp.tanh(log_a) * (1.0 + a * a))
        a_buf[:, lo:hi] = a
        b_buf[:, lo:hi] = mult * (ig * xc)

        short.append(in_proj(3 + hf))
        _tie(b_buf, lo, short[hf], j < 0)

        def recur(h, first, steps, n):
            for t in range(steps):
                rows = pl.ds(first + t * n, n)
                h = a_buf[rows, lo:hi] * h + b_buf[rows, lo:hi]
                b_buf[rows, lo:hi] = h
            return h
        h_car[:, lo:hi] = recur(h_car[:, lo:hi], 0, tT, nB)
        if guest:
            nhg_ref[layer, :, lo:hi] = recur(sthg_ref[:, lo:hi], R, tTg, nBg)

    mix_buf[:, 0:C] = (b_buf[...] * jax.nn.gelu(in_proj(1))).astype(BF16)

    z = short[0] * short[1]
    bs = in_proj(2)
    z_buf[2 * nB:2 * nB + R] = z[0:R]
    mix_buf[0:R, C:2 * C] = (bs[0:R] * conv(z_buf, scw_ref, SC_CONV, nB, R)).astype(BF16)
    if guest:
        _load_rows(zg_buf, stscg_ref, SC_CONV - 1, nBg)
        zg_buf[2 * nBg:2 * nBg + Rg] = z[R:Rt]
        mix_buf[R:Rt, C:2 * C] = (bs[R:Rt] * conv(zg_buf, scw_ref, SC_CONV, nBg, Rg)).astype(BF16)

    mix = _dot(mix_buf[...], wout_ref[...])
    y_ref[...] = x_tm[...] + g_ref[...][None] * mix[0:R].reshape(tT, nB, D_MODEL)
    if guest:
        yg_ref[...] = xg_tm + gg_ref[...][None] * mix[R:Rt].reshape(tTg, nBg, D_MODEL)
        _keep_previous(prev_g, (nhg_ref, nlcg_ref, nscg_ref))
        nlcg_ref[layer] = xlg_buf[Rg:Rg + 3 * nBg].reshape(LRU_CONV - 1, nBg, C)
        _store_rows(nscg_ref, layer, zg_buf[Rg:Rg + 2 * nBg], SC_CONV - 1, nBg)

    lc_tail = xl_buf[R:R + 3 * nB]
    sc_tail = z_buf[R:R + 2 * nB]
    _keep_previous(prev, (nh_ref, nlc_ref, nsc_ref))
    nh_ref[layer] = h_car[...]
    nlc_ref[layer] = lc_tail.reshape(LRU_CONV - 1, nB, C)
    _store_rows(nsc_ref, layer, sc_tail, SC_CONV - 1, nB)
    xl_buf[0:3 * nB] = lc_tail
    z_buf[0:2 * nB] = sc_tail

    _cast_riders(ride_in, ride_out)


_MIX_SMALL = ("norm1_g", "lru_conv_w", "lru_conv_b", "lru_ba", "lru_bx", "lru_lambda", "sc_conv_w")


def _mix_call(l, x, mod, state, prev, small, w_gate, big, riders, *, tT, nB, x_bm, guest=None):
    if x_bm:
        B, T, _ = x.shape
        x_spec = pl.BlockSpec((nB, tT, D_MODEL), lambda i, j: (i, j, 0))
    else:
        T, B, _ = x.shape
        x_spec = pl.BlockSpec((tT, nB, D_MODEL), lambda i, j: (j, i, 0))
    R = tT * nB
    C = D_HALF
    n_steps = T // tT
    assert not riders or B == nB
    guest_specs, guest_args, guest_out_specs, guest_out_shapes, guest_dims = [], [], [], [], None
    Rg = 0
    if guest is not None:
        xg, modg, stateg, prevg = guest
        Bg, tTg = (xg.shape[0], xg.shape[1]) if x_bm else (xg.shape[1], xg.shape[0])
        assert B == nB and Bg % n_steps == 0
        nBg = Bg // n_steps
        guest_dims, Rg = (tTg, nBg), tTg * nBg
        gx = (pl.BlockSpec((nBg, tTg, D_MODEL), lambda i, j: (j, 0, 0)) if x_bm else
              pl.BlockSpec((tTg, nBg, D_MODEL), lambda i, j: (0, j, 0)))
        gmod = lambda k: pl.BlockSpec((None, nBg, D_MODEL), lambda i, j: (l, j, k))
        gh = lambda n, ll: pl.BlockSpec((n, nBg, C), lambda i, j: (ll, j, 0))
        glc = lambda n, ll: pl.BlockSpec((n, LRU_CONV - 1, nBg, C), lambda i, j: (ll, 0, j, 0))
        gsc = lambda n, ll: pl.BlockSpec((n, nBg, SC_CONV - 1, C), lambda i, j: (ll, j, 0, 0))
        guest_specs = [gx, gmod(0), gmod(1), gmod(2), gh(None, l), glc(None, l), gsc(None, l),
                       *([gh(l, 0), glc(l, 0), gsc(l, 0)] if l else [])]
        guest_args = [xg, modg, modg, modg, *stateg, *prevg]
        guest_out_specs = [pl.BlockSpec((tTg, nBg, D_MODEL), lambda i, j: (0, j, 0)),
                           gh(l + 1, 0), glc(l + 1, 0), gsc(l + 1, 0)]
        guest_out_shapes = [jax.ShapeDtypeStruct((tTg, Bg, D_MODEL), F32),
                            jax.ShapeDtypeStruct((l + 1, Bg, C), F32),
                            jax.ShapeDtypeStruct((l + 1, LRU_CONV - 1, Bg, C), F32),
                            jax.ShapeDtypeStruct((l + 1, Bg, SC_CONV - 1, C), F32)]
    whole = lambda i, j: (0, 0)
    mod_spec = lambda k: pl.BlockSpec((None, nB, D_MODEL), lambda i, j: (l, i, k))
    once = pl.Buffered(1)
    h_spec = lambda n: pl.BlockSpec((n, nB, C), lambda i, j: (0, i, 0))
    lc_spec = lambda n: pl.BlockSpec((n, LRU_CONV - 1, nB, C), lambda i, j: (0, 0, i, 0))
    sc_spec = lambda n: pl.BlockSpec((n, nB, SC_CONV - 1, C), lambda i, j: (0, i, 0, 0))
    state_specs, state_args = [], []
    if state is not None:
        state_specs = [
            pl.BlockSpec((None, nB, C), lambda i, j: (l, i, 0)),
            pl.BlockSpec((None, LRU_CONV - 1, nB, C), lambda i, j: (l, 0, i, 0)),
            pl.BlockSpec((None, nB, SC_CONV - 1, C), lambda i, j: (l, i, 0, 0)),
        ]
        state_args = list(state)
    prev_specs = [h_spec(l), lc_spec(l), sc_spec(l)] if l else []
    r_in, r_out, r_shapes = _rider_specs(riders, n_steps, lambda i, j: j)
    scratch = [
        pltpu.VMEM((R + Rg, D_MODEL), BF16),
        pltpu.VMEM((R + 3 * nB, C), F32),
        pltpu.VMEM((R + 2 * nB, C), F32),
        pltpu.VMEM((R + Rg, C), F32),
        pltpu.VMEM((R + Rg, C), F32),
        pltpu.VMEM((R + Rg, D_MODEL), BF16),
        pltpu.VMEM((nB, C), F32),
    ]
    if x_bm:
        scratch.append(pltpu.VMEM((tT, nB, D_MODEL), F32))
    if guest is not None:
        scratch += [pltpu.VMEM((Rg + (LRU_CONV - 1) * guest_dims[1], C), F32),
                    pltpu.VMEM((Rg + (SC_CONV - 1) * guest_dims[1], C), F32)]
    outs = pl.pallas_call(
        functools.partial(_mix_kernel, layer=l, tT=tT, nB=nB, x_bm=x_bm,
                          has_state=state is not None, guest=guest_dims, n_ride=len(riders)),
        grid=(B // nB, n_steps),
        in_specs=[
            x_spec, mod_spec(0), mod_spec(1), mod_spec(2),
            *[_whole_spec(small[name]) for name in _MIX_SMALL],
            *state_specs, *prev_specs, *guest_specs,
            pl.BlockSpec((D_MODEL, 5 * C), whole, pipeline_mode=once),
            pl.BlockSpec((None, C // MXU_COLS, MXU_COLS, 2 * MXU_COLS),
                         lambda i, j: (l, 0, 0, 0), pipeline_mode=once),
            pl.BlockSpec((D_MODEL, D_MODEL), whole, pipeline_mode=once),
            *r_in,
        ],
        out_specs=[
            pl.BlockSpec((tT, nB, D_MODEL), lambda i, j: (j, i, 0)),
            h_spec(l + 1), lc_spec(l + 1), sc_spec(l + 1),
            *guest_out_specs, *r_out,
        ],
        out_shape=[
            jax.ShapeDtypeStruct((T, B, D_MODEL), F32),
            jax.ShapeDtypeStruct((l + 1, B, C), F32),
            jax.ShapeDtypeStruct((l + 1, LRU_CONV - 1, B, C), F32),
            jax.ShapeDtypeStruct((l + 1, B, SC_CONV - 1, C), F32),
            *guest_out_shapes, *r_shapes,
        ],
        scratch_shapes=scratch,
        compiler_params=pltpu.CompilerParams(
            dimension_semantics=("arbitrary", "arbitrary"),
            vmem_limit_bytes=VMEM_LIMIT_BYTES),
        name="mix",
    )(x, mod, mod, mod, *[small[name] for name in _MIX_SMALL], *state_args, *prev, *guest_args,
      big["w_in"], w_gate, big["w_out"], *[src for src, _ in riders])
    n_out = 4 + len(guest_out_specs)
    return outs[:n_out], outs[n_out:]


def _ffn_kernel(*refs, layer, tT, nB, final, has_state, guest, n_ride):
    refs = list(refs)
    x_ref, sh_ref, sc_ref, g_ref, ng_ref, cw_ref, fg_ref = _take(refs, 7)
    stfc_ref = _take(refs, 1)[0] if has_state else None
    prev = _take(refs, 1) if layer else []
    if guest:
        xg_ref, shg_ref, scg_ref, gg_ref, stg_ref = _take(refs, 5)
        prev_g = _take(refs, 1) if layer else []
    wup_ref, wdn_ref = _take(refs, 2)
    ride_in = _take(refs, n_ride)
    y_ref, nfc_ref = _take(refs, 2)
    if guest:
        yg_ref, nfcg_ref = _take(refs, 2)
    ride_out = _take(refs, n_ride)
    hn_buf, u_buf, u_car, act_buf = _take(refs, 4)

    R = tT * nB
    tTg, nBg = guest if guest else (0, 0)
    Rg = tTg * nBg
    j = pl.program_id(1)
    gain = ng_ref[layer:layer + 1, :]

    @pl.when(j == 0)
    def _():
        _load_rows(u_car, stfc_ref, FFN_CONV - 1, nB)

    hn = _modulated_norm(x_ref[...], gain, sc_ref[...], sh_ref[...])
    hn_buf[0:R] = hn.reshape(R, D_MODEL).astype(BF16)
    if guest:
        ug_buf = refs[0]
        hn_g = _modulated_norm(xg_ref[...], gain, scg_ref[...], shg_ref[...])
        hn_buf[R:R + Rg] = hn_g.reshape(Rg, D_MODEL).astype(BF16)

    def conv_act(ub, n, rows, v, tap):
        uc = tap(0) * ub[0:rows]
        for k in range(1, FFN_CONV):
            uc = uc + tap(k) * ub[k * n:k * n + rows]
        return (jax.nn.gelu(uc) * v).astype(BF16)

    for ci, c0 in enumerate(range(0, D_FF, MXU_COLS)):
        cols = slice(c0, c0 + MXU_COLS)
        tap = lambda k: cw_ref[layer, k:k + 1, cols]
        u = _dot(hn_buf[...], wup_ref[:, cols])
        v = _dot(hn_buf[...], wup_ref[:, D_FF + c0:D_FF + c0 + MXU_COLS])
        ub = u_buf.at[ci % 2]
        ub[0:2 * nB] = u_car[:, cols]
        ub[2 * nB:2 * nB + R] = u[0:R]
        act_buf[0:R, cols] = conv_act(ub, nB, R, v[0:R], tap)
        u_car[:, cols] = ub[R:R + 2 * nB]
        if guest:
            ubg = ug_buf.at[ci % 2]
            for k in range(FFN_CONV - 1):
                ubg[k * nBg:(k + 1) * nBg] = stg_ref[:, k, cols]
            ubg[2 * nBg:2 * nBg + Rg] = u[R:R + Rg]
            act_buf[R:R + Rg, cols] = conv_act(ubg, nBg, Rg, v[R:R + Rg], tap)
            for k in range(FFN_CONV - 1):
                nfcg_ref[layer, :, k, cols] = ubg[Rg + k * nBg:Rg + (k + 1) * nBg]

    out = _dot(act_buf[...], wdn_ref[...])

    def finish(x_t, g_t, out_t, t_n, b_n, dst):
        xn = x_t + g_t[None] * out_t.reshape(t_n, b_n, D_MODEL)
        if final:
            ms = jnp.mean(xn * xn, axis=-1, keepdims=True)
            xn = xn * jax.lax.rsqrt(ms + EPS) * fg_ref[...][None]
            dst[...] = jnp.swapaxes(xn, 0, 1)
        else:
            dst[...] = xn

    finish(x_ref[...], g_ref[...], out[0:R], tT, nB, y_ref)
    if guest:
        finish(xg_ref[...], gg_ref[...], out[R:R + Rg], tTg, nBg, yg_ref)
        _keep_previous(prev_g, (nfcg_ref,))

    _keep_previous(prev, (nfc_ref,))
    _store_rows(nfc_ref, layer, u_car[...], FFN_CONV - 1, nB)
    _cast_riders(ride_in, ride_out)


def _ffn_call(l, x, mod, st_fc, prev, small, big, riders, *, tT, nB, final, guest=None):
    T, B, _ = x.shape
    R = tT * nB
    n_steps = T // tT
    assert not riders or B == nB
    guest_specs, guest_args, guest_out_specs, guest_out_shapes, guest_dims = [], [], [], [], None
    Rg = 0
    if guest is not None:
        xg, modg, stg, prevg = guest
        tTg, Bg, _ = xg.shape
        assert B == nB and Bg % n_steps == 0
        nBg = Bg // n_steps
        guest_dims, Rg = (tTg, nBg), tTg * nBg
        gmod = lambda k: pl.BlockSpec((None, nBg, D_MODEL), lambda i, j: (l, j, k))
        gfc = lambda n: pl.BlockSpec((n, nBg, FFN_CONV - 1, D_FF), lambda i, j: (0, j, 0, 0))
        guest_specs = [
            pl.BlockSpec((tTg, nBg, D_MODEL), lambda i, j: (0, j, 0)), gmod(3), gmod(4), gmod(5),
            pl.BlockSpec((None, nBg, FFN_CONV - 1, D_FF), lambda i, j: (l, j, 0, 0)),
            *([gfc(l)] if l else [])]
        guest_args = [xg, modg, modg, modg, stg, *prevg]
        if final:
            guest_out_specs = [pl.BlockSpec((nBg, tTg, D_MODEL), lambda i, j: (j, 0, 0))]
            guest_out_shapes = [jax.ShapeDtypeStruct((Bg, tTg, D_MODEL), F32)]
        else:
            guest_out_specs = [pl.BlockSpec((tTg, nBg, D_MODEL), lambda i, j: (0, j, 0))]
            guest_out_shapes = [jax.ShapeDtypeStruct((tTg, Bg, D_MODEL), F32)]
        guest_out_specs.append(gfc(l + 1))
        guest_out_shapes.append(jax.ShapeDtypeStruct((l + 1, Bg, FFN_CONV - 1, D_FF), F32))
    whole = lambda i, j: (0, 0)
    mod_spec = lambda k: pl.BlockSpec((None, nB, D_MODEL), lambda i, j: (l, i, k))
    once = pl.Buffered(1)
    fc_spec = lambda n: pl.BlockSpec((n, nB, FFN_CONV - 1, D_FF), lambda i, j: (0, i, 0, 0))
    state_specs, state_args = [], []
    if st_fc is not None:
        state_specs = [pl.BlockSpec((None, nB, FFN_CONV - 1, D_FF), lambda i, j: (l, i, 0, 0))]
        state_args = [st_fc]
    prev_specs = [fc_spec(l)] if l else []
    if final:
        y_spec = pl.BlockSpec((nB, tT, D_MODEL), lambda i, j: (i, j, 0))
        y_shape = jax.ShapeDtypeStruct((B, T, D_MODEL), F32)
    else:
        y_spec = pl.BlockSpec((tT, nB, D_MODEL), lambda i, j: (j, i, 0))
        y_shape = jax.ShapeDtypeStruct((T, B, D_MODEL), F32)
    r_in, r_out, r_shapes = _rider_specs(riders, n_steps, lambda i, j: j)
    outs = pl.pallas_call(
        functools.partial(_ffn_kernel, layer=l, tT=tT, nB=nB, final=final,
                          has_state=st_fc is not None, guest=guest_dims, n_ride=len(riders)),
        grid=(B // nB, n_steps),
        in_specs=[
            pl.BlockSpec((tT, nB, D_MODEL), lambda i, j: (j, i, 0)),
            mod_spec(3), mod_spec(4), mod_spec(5),
            _whole_spec(small["norm2_g"]), _whole_spec(small["ffn_conv_w"]),
            _whole_spec(small["final_g"]),
            *state_specs, *prev_specs, *guest_specs,
            pl.BlockSpec((D_MODEL, 2 * D_FF), whole, pipeline_mode=once),
            pl.BlockSpec((D_FF, D_MODEL), whole, pipeline_mode=once),
            *r_in,
        ],
        out_specs=[y_spec, fc_spec(l + 1), *guest_out_specs, *r_out],
        out_shape=[
            y_shape,
            jax.ShapeDtypeStruct((l + 1, B, FFN_CONV - 1, D_FF), F32),
            *guest_out_shapes,
            *r_shapes,
        ],
        scratch_shapes=[
            pltpu.VMEM((R + Rg, D_MODEL), BF16),
            pltpu.VMEM((2, R + 2 * nB, MXU_COLS), F32),
            pltpu.VMEM((2 * nB, D_FF), F32),
            pltpu.VMEM((R + Rg, D_FF), BF16),
            *([pltpu.VMEM((2, Rg + 2 * guest_dims[1], MXU_COLS), F32)] if guest else []),
        ],
        compiler_params=pltpu.CompilerParams(
            dimension_semantics=("arbitrary", "arbitrary"),
            vmem_limit_bytes=VMEM_LIMIT_BYTES),
        name="ffn",
    )(x, mod, mod, mod, small["norm2_g"], small["ffn_conv_w"], small["final_g"],
      *state_args, *prev, *guest_args, big["w_up"], big["w_dn"], *[src for src, _ in riders])
    n_out = 2 + len(guest_out_specs)
    return outs[:n_out], outs[n_out:]


def _gate_weights(lru_wa, lru_wx):
    eye = jnp.eye(LRU_HEADS, dtype=lru_wa.dtype)
    dense = lambda w: jnp.einsum("lhij,hk->lhikj", w, eye).reshape(DEPTH, D_HALF, D_HALF)
    wa, wx = dense(lru_wa), dense(lru_wx)
    ws = []
    for hf in range(D_HALF // MXU_COLS):
        s = slice(hf * MXU_COLS, (hf + 1) * MXU_COLS)
        ws.append(jnp.concatenate([wa[:, s, s], wx[:, s, s]], axis=-1))
    return jnp.stack(ws, axis=1).astype(BF16)


def kernel(x_prompt, x_sample, c_prompt, c_sample, state_lru_h, state_lru_conv, state_sc_conv, state_ffn_conv, w_ada, b_ada, norm1_g, norm2_g, w_in, lru_conv_w, lru_conv_b, lru_wa, lru_ba, lru_wx, lru_bx, lru_lambda, sc_conv_w, w_out, ffn_w_up, ffn_conv_w, ffn_w_down, final_g):
    small = dict(
        norm1_g=norm1_g, norm2_g=norm2_g, final_g=final_g[None, :], lru_conv_w=lru_conv_w,
        lru_conv_b=lru_conv_b, lru_ba=lru_ba, lru_bx=lru_bx, lru_lambda=lru_lambda,
        sc_conv_w=sc_conv_w, ffn_conv_w=ffn_conv_w)
    w_gate = _gate_weights(lru_wa, lru_wx)

    (mod_p, mod_s), cast = _ada_call(c_prompt, c_sample, w_ada, b_ada,
                                     [(w_in, 0), (w_out, 0)])
    w_bf = dict(w_in=cast[0], w_out=cast[1])

    tile_p = dict(tT=128, nB=x_prompt.shape[0])
    x_p, x_s = x_prompt, x_sample
    mix_state_s = (state_lru_h, jnp.swapaxes(state_lru_conv, 1, 2), state_sc_conv)
    mix_p, mix_s, fc_p, fc_s = [], [], [], []
    for l in range(DEPTH):
        (x_p, *mix_p, x_s, nh_s, nlc_s, nsc_s), cast = _mix_call(
            l, x_p, mod_p, None, mix_p, small, w_gate, w_bf, [(ffn_w_up, l), (ffn_w_down, l)],
            x_bm=(l == 0), guest=(x_s, mod_s, mix_state_s, mix_s), **tile_p)
        mix_s = [nh_s, nlc_s, nsc_s]
        w_bf = dict(w_up=cast[0], w_dn=cast[1])
        (x_p, nfc_p, x_s, nfc_s), cast = _ffn_call(
            l, x_p, mod_p, None, fc_p, small, w_bf,
            [(w_in, l + 1), (w_out, l + 1)] if l + 1 < DEPTH else [],
            final=(l == DEPTH - 1), guest=(x_s, mod_s, state_ffn_conv, fc_s), **tile_p)
        fc_p, fc_s = [nfc_p], [nfc_s]
        if cast:
            w_bf = dict(w_in=cast[0], w_out=cast[1])
    states = [(nh, jnp.swapaxes(nlc, 1, 2), nsc, nfc)
              for (nh, nlc, nsc), (nfc,) in ((mix_p, fc_p), (mix_s, fc_s))]
    return (x_p, x_s) + states[0] + states[1]
```

```python
import functools

import jax
import jax.numpy as jnp
from jax.experimental import pallas as pl
from jax.experimental.pallas import tpu as pltpu

D_MODEL = 1024
DEPTH = 2
D_HALF = 512
LRU_HEADS = 8
LRU_HEAD_DIM = 64
LRU_CONV = 4
SC_CONV = 3
FFN_CONV = 3
RG_C = 8.0
D_FF = 2816
EPS = 1e-6
N_MOD = 6

MXU_COLS = 256
BF16_SUBLANES = 16
VMEM_LIMIT_BYTES = 56 * 1024 * 1024
ADA_COL_BLOCKS = 4

BF16 = jnp.bfloat16
F32 = jnp.float32


def _dot(a, b):
    return jnp.dot(a, b, preferred_element_type=F32)


def _take(refs, n):
    head = refs[:n]
    del refs[:n]
    return head


def _whole_spec(arr):
    return pl.BlockSpec(arr.shape, lambda *g: (0,) * arr.ndim)


def _modulated_norm(x3, gain, scale, shift):
    ms = jnp.mean(x3 * x3, axis=-1, keepdims=True)
    y = x3 * jax.lax.rsqrt(ms + EPS)
    return y * (gain * (1.0 + scale))[None] + shift[None]


def _tie(buf, col0, value, never):
    rows, lanes = BF16_SUBLANES, 128
    fold = value[0:rows]
    for r0 in range(rows, value.shape[0], rows):
        fold = fold + value[r0:r0 + rows]
    dep = fold[:, 0:lanes]
    for c0 in range(lanes, fold.shape[1], lanes):
        dep = dep + fold[:, c0:c0 + lanes]
    tile = (slice(0, rows), slice(col0, col0 + lanes))
    buf[tile] = jnp.where(never, dep.astype(buf.dtype), buf[tile])


def _load_rows(buf, st_ref, n_rows, nB):
    if st_ref is None:
        buf[0:n_rows * nB] = jnp.zeros((n_rows * nB, buf.shape[1]), buf.dtype)
    else:
        for k in range(n_rows):
            buf[k * nB:(k + 1) * nB] = st_ref[:, k, :]


def _store_rows(out_ref, layer, tail, n_rows, nB):
    for k in range(n_rows):
        out_ref[layer, :, k, :] = tail[k * nB:(k + 1) * nB]


def _keep_previous(prev_refs, out_refs):
    for prev, out in zip(prev_refs, out_refs):
        out[0:prev.shape[0]] = prev[...]


def _rider_specs(riders, n_steps, step_of):
    in_specs, out_specs, out_shapes = [], [], []
    for src, layer in riders:
        _, K, N = src.shape
        hold = 1
        while (K * hold) % n_steps or (K * hold // n_steps) % BF16_SUBLANES:
            hold *= 2
            assert hold <= n_steps, (K, n_steps)
        rows = K * hold // n_steps
        in_specs.append(pl.BlockSpec(
            (None, rows, N), lambda *g, l=layer, h=hold: (l, step_of(*g) // h, 0)))
        out_specs.append(pl.BlockSpec((rows, N), lambda *g, h=hold: (step_of(*g) // h, 0)))
        out_shapes.append(jax.ShapeDtypeStruct((K, N), BF16))
    return in_specs, out_specs, out_shapes


def _cast_riders(ride_in, ride_out):
    for src, dst in zip(ride_in, ride_out):
        dst[...] = src[...].astype(BF16)


def _ada_kernel(*refs, n_ride):
    refs = list(refs)
    cp_ref, cs_ref, w_ref, b_ref = _take(refs, 4)
    ride_in = _take(refs, n_ride)
    op_ref, os_ref = _take(refs, 2)
    ride_out = _take(refs, n_ride)
    w = w_ref[...].astype(BF16)
    b = b_ref[pl.ds(pl.program_id(0) // ADA_COL_BLOCKS, 1), :]
    op_ref[...] = _dot(jax.nn.silu(cp_ref[...]).astype(BF16), w) + b
    os_ref[...] = _dot(jax.nn.silu(cs_ref[...]).astype(BF16), w) + b
    _cast_riders(ride_in, ride_out)


def _ada_call(c_prompt, c_sample, w_ada, b_ada, riders):
    n_p, n_s = c_prompt.shape[0], c_sample.shape[0]
    n_steps = DEPTH * ADA_COL_BLOCKS
    cols = N_MOD * D_MODEL // ADA_COL_BLOCKS
    lk = lambda s: (s // ADA_COL_BLOCKS, 0, s % ADA_COL_BLOCKS)
    c_spec = lambda n: pl.BlockSpec((n, D_MODEL), lambda s: (0, 0))
    o_spec = lambda n: pl.BlockSpec((None, n, cols), lk)
    r_in, r_out, r_shapes = _rider_specs(riders, n_steps, lambda s: s)
    outs = pl.pallas_call(
        functools.partial(_ada_kernel, n_ride=len(riders)),
        grid=(n_steps,),
        in_specs=[c_spec(n_p), c_spec(n_s),
                  pl.BlockSpec((None, D_MODEL, cols), lk),
                  pl.BlockSpec((DEPTH, cols), lambda s: (0, s % ADA_COL_BLOCKS)),
                  *r_in],
        out_specs=[o_spec(n_p), o_spec(n_s), *r_out],
        out_shape=[jax.ShapeDtypeStruct((DEPTH, n_p, N_MOD * D_MODEL), F32),
                   jax.ShapeDtypeStruct((DEPTH, n_s, N_MOD * D_MODEL), F32),
                   *r_shapes],
        name="ada",
    )(c_prompt, c_sample, w_ada, b_ada, *[src for src, _ in riders])
    return outs[:2], outs[2:]


def _mix_kernel(*refs, layer, tT, nB, n_tiles, x_bm, has_state, guest, n_ride):
    refs = list(refs)
    x_ref, sh_ref, sc_ref, g_ref = _take(refs, 4)
    ng_ref, cw_ref, cb_ref, ba_ref, bx_ref, lam_ref, scw_ref = _take(refs, 7)
    sth_ref, stlc_ref, stsc_ref = _take(refs, 3) if has_state else (None, None, None)
    prev = _take(refs, 3) if layer else []
    if guest:
        xg_ref, shg_ref, scg_ref, gg_ref, sthg_ref, stlcg_ref, stscg_ref = _take(refs, 7)
        prev_g = _take(refs, 3) if layer else []
    win_ref, wg_ref, wout_ref = _take(refs, 3)
    ride_in = _take(refs, n_ride)
    y_ref, nh_ref, nlc_ref, nsc_ref = _take(refs, 4)
    if guest:
        yg_ref, nhg_ref, nlcg_ref, nscg_ref = _take(refs, 4)
    ride_out = _take(refs, n_ride)
    hn_buf, xl_buf, z_buf, a_buf, b_buf, mix_buf, h_car = _take(refs, 7)
    xt_buf, xt_sem = _take(refs, 2) if x_bm else (None, None)
    xlg_buf, zg_buf = refs if guest else (None, None)

    R = tT * nB
    tTg, nBg = guest if guest else (0, 0)
    Rg = tTg * nBg
    Rt = R + Rg
    C = D_HALF
    j = pl.program_id(1)
    row = lambda ref: ref[layer:layer + 1, :]
    tap = lambda ref, k: ref[layer, k:k + 1, :]

    def conv(buf, w_ref, taps, n, rows, bias=None):
        acc = tap(w_ref, 0) * buf[0:rows]
        if bias is not None:
            acc = bias + acc
        for k in range(1, taps):
            acc = acc + tap(w_ref, k) * buf[k * n:k * n + rows]
        return acc

    @pl.when(j == 0)
    def _():
        if has_state:
            xl_buf[0:3 * nB] = stlc_ref[...].reshape(3 * nB, C)
            h_car[...] = sth_ref[...]
        else:
            xl_buf[0:3 * nB] = jnp.zeros((3 * nB, C), F32)
            h_car[...] = jnp.zeros((nB, C), F32)
        _load_rows(z_buf, stsc_ref, SC_CONV - 1, nB)

    if x_bm:
        slot = j % 2

        def fetch(tile, half):
            return [pltpu.make_async_copy(x_ref.at[b, pl.ds(tile * tT, tT), :],
                                          xt_buf.at[half, :, b, :], xt_sem.at[half, b])
                    for b in range(nB)]

        @pl.when(j == 0)
        def _():
            for copy in fetch(0, 0):
                copy.start()

        @pl.when(j + 1 < n_tiles)
        def _():
            for copy in fetch(j + 1, 1 - slot):
                copy.start()

        for copy in fetch(j, slot):
            copy.wait()
        x_tm = xt_buf.at[slot]
    else:
        x_tm = x_ref
    hn = _modulated_norm(x_tm[...], row(ng_ref), sc_ref[...], sh_ref[...])
    hn_buf[0:R] = hn.reshape(R, D_MODEL).astype(BF16)
    if guest:
        xg_tm = jnp.swapaxes(xg_ref[...], 0, 1) if x_bm else xg_ref[...]
        hn_g = _modulated_norm(xg_tm, row(ng_ref), scg_ref[...], shg_ref[...])
        hn_buf[R:Rt] = hn_g.reshape(Rg, D_MODEL).astype(BF16)

    in_proj = lambda k: _dot(hn_buf[...], win_ref[:, k * C:(k + 1) * C])

    xl = in_proj(0)
    xl_buf[3 * nB:3 * nB + R] = xl[0:R]
    b_buf[0:R] = conv(xl_buf, cw_ref, LRU_CONV, nB, R, row(cb_ref))
    if guest:
        xlg_buf[0:3 * nBg] = stlcg_ref[...].reshape(3 * nBg, C)
        xlg_buf[3 * nBg:3 * nBg + Rg] = xl[R:Rt]
        b_buf[R:Rt] = conv(xlg_buf, cw_ref, LRU_CONV, nBg, Rg, row(cb_ref))

    neg_c_sp = -RG_C * jnp.logaddexp(-row(lam_ref), 0.0)
    short = []
    for hf in range(C // MXU_COLS):
        lo, hi = hf * MXU_COLS, (hf + 1) * MXU_COLS
        xc = b_buf[:, lo:hi]
        gates = _dot(xc.astype(BF16), wg_ref[hf])
        r = jax.nn.sigmoid(gates[:, :MXU_COLS] + row(ba_ref)[:, lo:hi])
        ig = jax.nn.sigmoid(gates[:, MXU_COLS:] + row(bx_ref)[:, lo:hi])
        log_a = r * neg_c_sp[:, lo:hi]
        a = jnp.exp(log_a)
        mult = jnp.sqrt(-jnp.tanh(log_a) * (1.0 + a * a))
        a_buf[:, lo:hi] = a
        b_buf[:, lo:hi] = mult * (ig * xc)

        short.append(in_proj(3 + hf))
        _tie(b_buf, lo, short[hf], j < 0)

        def recur(h, first, steps, n):
            for t in range(steps):
                rows = pl.ds(first + t * n, n)
                h = a_buf[rows, lo:hi] * h + b_buf[rows, lo:hi]
                b_buf[rows, lo:hi] = h
            return h
        h_car[:, lo:hi] = recur(h_car[:, lo:hi], 0, tT, nB)
        if guest:
            nhg_ref[layer, :, lo:hi] = recur(sthg_ref[:, lo:hi], R, tTg, nBg)

    mix_buf[:, 0:C] = (b_buf[...] * jax.nn.gelu(in_proj(1))).astype(BF16)

    z = short[0] * short[1]
    bs = in_proj(2)
    z_buf[2 * nB:2 * nB + R] = z[0:R]
    mix_buf[0:R, C:2 * C] = (bs[0:R] * conv(z_buf, scw_ref, SC_CONV, nB, R)).astype(BF16)
    if guest:
        _load_rows(zg_buf, stscg_ref, SC_CONV - 1, nBg)
        zg_buf[2 * nBg:2 * nBg + Rg] = z[R:Rt]
        mix_buf[R:Rt, C:2 * C] = (bs[R:Rt] * conv(zg_buf, scw_ref, SC_CONV, nBg, Rg)).astype(BF16)

    mix = _dot(mix_buf[...], wout_ref[...])
    y_ref[...] = x_tm[...] + g_ref[...][None] * mix[0:R].reshape(tT, nB, D_MODEL)
    if guest:
        yg_ref[...] = xg_tm + gg_ref[...][None] * mix[R:Rt].reshape(tTg, nBg, D_MODEL)
        _keep_previous(prev_g, (nhg_ref, nlcg_ref, nscg_ref))
        nlcg_ref[layer] = xlg_buf[Rg:Rg + 3 * nBg].reshape(LRU_CONV - 1, nBg, C)
        _store_rows(nscg_ref, layer, zg_buf[Rg:Rg + 2 * nBg], SC_CONV - 1, nBg)

    lc_tail = xl_buf[R:R + 3 * nB]
    sc_tail = z_buf[R:R + 2 * nB]
    _keep_previous(prev, (nh_ref, nlc_ref, nsc_ref))
    nh_ref[layer] = h_car[...]
    nlc_ref[layer] = lc_tail.reshape(LRU_CONV - 1, nB, C)
    _store_rows(nsc_ref, layer, sc_tail, SC_CONV - 1, nB)
    xl_buf[0:3 * nB] = lc_tail
    z_buf[0:2 * nB] = sc_tail

    _cast_riders(ride_in, ride_out)


_MIX_SMALL = ("norm1_g", "lru_conv_w", "lru_conv_b", "lru_ba", "lru_bx", "lru_lambda", "sc_conv_w")


def _mix_call(l, x, mod, state, prev, small, w_gate, big, riders, *, tT, nB, x_bm, guest=None):
    if x_bm:
        B, T, _ = x.shape
        assert B == nB
        x_spec = pl.BlockSpec(memory_space=pl.ANY)
    else:
        T, B, _ = x.shape
        x_spec = pl.BlockSpec((tT, nB, D_MODEL), lambda i, j: (j, i, 0))
    R = tT * nB
    C = D_HALF
    n_steps = T // tT
    assert not riders or B == nB
    guest_specs, guest_args, guest_out_specs, guest_out_shapes, guest_dims = [], [], [], [], None
    Rg = 0
    if guest is not None:
        xg, modg, stateg, prevg = guest
        Bg, tTg = (xg.shape[0], xg.shape[1]) if x_bm else (xg.shape[1], xg.shape[0])
        assert B == nB and Bg % n_steps == 0
        nBg = Bg // n_steps
        guest_dims, Rg = (tTg, nBg), tTg * nBg
        gx = (pl.BlockSpec((nBg, tTg, D_MODEL), lambda i, j: (j, 0, 0)) if x_bm else
              pl.BlockSpec((tTg, nBg, D_MODEL), lambda i, j: (0, j, 0)))
        gmod = lambda k: pl.BlockSpec((None, nBg, D_MODEL), lambda i, j: (l, j, k))
        gh = lambda n, ll: pl.BlockSpec((n, nBg, C), lambda i, j: (ll, j, 0))
        glc = lambda n, ll: pl.BlockSpec((n, LRU_CONV - 1, nBg, C), lambda i, j: (ll, 0, j, 0))
        gsc = lambda n, ll: pl.BlockSpec((n, nBg, SC_CONV - 1, C), lambda i, j: (ll, j, 0, 0))
        guest_specs = [gx, gmod(0), gmod(1), gmod(2), gh(None, l), glc(None, l), gsc(None, l),
                       *([gh(l, 0), glc(l, 0), gsc(l, 0)] if l else [])]
        guest_args = [xg, modg, modg, modg, *stateg, *prevg]
        guest_out_specs = [pl.BlockSpec((tTg, nBg, D_MODEL), lambda i, j: (0, j, 0)),
                           gh(l + 1, 0), glc(l + 1, 0), gsc(l + 1, 0)]
        guest_out_shapes = [jax.ShapeDtypeStruct((tTg, Bg, D_MODEL), F32),
                            jax.ShapeDtypeStruct((l + 1, Bg, C), F32),
                            jax.ShapeDtypeStruct((l + 1, LRU_CONV - 1, Bg, C), F32),
                            jax.ShapeDtypeStruct((l + 1, Bg, SC_CONV - 1, C), F32)]
    whole = lambda i, j: (0, 0)
    mod_spec = lambda k: pl.BlockSpec((None, nB, D_MODEL), lambda i, j: (l, i, k))
    once = pl.Buffered(1)
    h_spec = lambda n: pl.BlockSpec((n, nB, C), lambda i, j: (0, i, 0))
    lc_spec = lambda n: pl.BlockSpec((n, LRU_CONV - 1, nB, C), lambda i, j: (0, 0, i, 0))
    sc_spec = lambda n: pl.BlockSpec((n, nB, SC_CONV - 1, C), lambda i, j: (0, i, 0, 0))
    state_specs, state_args = [], []
    if state is not None:
        state_specs = [
            pl.BlockSpec((None, nB, C), lambda i, j: (l, i, 0)),
            pl.BlockSpec((None, LRU_CONV - 1, nB, C), lambda i, j: (l, 0, i, 0)),
            pl.BlockSpec((None, nB, SC_CONV - 1, C), lambda i, j: (l, i, 0, 0)),
        ]
        state_args = list(state)
    prev_specs = [h_spec(l), lc_spec(l), sc_spec(l)] if l else []
    r_in, r_out, r_shapes = _rider_specs(riders, n_steps, lambda i, j: j)
    scratch = [
        pltpu.VMEM((R + Rg, D_MODEL), BF16),
        pltpu.VMEM((R + 3 * nB, C), F32),
        pltpu.VMEM((R + 2 * nB, C), F32),
        pltpu.VMEM((R + Rg, C), F32),
        pltpu.VMEM((R + Rg, C), F32),
        pltpu.VMEM((R + Rg, D_MODEL), BF16),
        pltpu.VMEM((nB, C), F32),
    ]
    if x_bm:
        scratch += [pltpu.VMEM((2, tT, nB, D_MODEL), F32),
                    pltpu.SemaphoreType.DMA((2, nB))]
    if guest is not None:
        scratch += [pltpu.VMEM((Rg + (LRU_CONV - 1) * guest_dims[1], C), F32),
                    pltpu.VMEM((Rg + (SC_CONV - 1) * guest_dims[1], C), F32)]
    outs = pl.pallas_call(
        functools.partial(_mix_kernel, layer=l, tT=tT, nB=nB, n_tiles=n_steps, x_bm=x_bm,
                          has_state=state is not None, guest=guest_dims, n_ride=len(riders)),
        grid=(B // nB, n_steps),
        in_specs=[
            x_spec, mod_spec(0), mod_spec(1), mod_spec(2),
            *[_whole_spec(small[name]) for name in _MIX_SMALL],
            *state_specs, *prev_specs, *guest_specs,
            pl.BlockSpec((D_MODEL, 5 * C), whole, pipeline_mode=once),
            pl.BlockSpec((None, C // MXU_COLS, MXU_COLS, 2 * MXU_COLS),
                         lambda i, j: (l, 0, 0, 0), pipeline_mode=once),
            pl.BlockSpec((D_MODEL, D_MODEL), whole, pipeline_mode=once),
            *r_in,
        ],
        out_specs=[
            pl.BlockSpec((tT, nB, D_MODEL), lambda i, j: (j, i, 0)),
            h_spec(l + 1), lc_spec(l + 1), sc_spec(l + 1),
            *guest_out_specs, *r_out,
        ],
        out_shape=[
            jax.ShapeDtypeStruct((T, B, D_MODEL), F32),
            jax.ShapeDtypeStruct((l + 1, B, C), F32),
            jax.ShapeDtypeStruct((l + 1, LRU_CONV - 1, B, C), F32),
            jax.ShapeDtypeStruct((l + 1, B, SC_CONV - 1, C), F32),
            *guest_out_shapes, *r_shapes,
        ],
        scratch_shapes=scratch,
        compiler_params=pltpu.CompilerParams(
            dimension_semantics=("arbitrary", "arbitrary"),
            vmem_limit_bytes=VMEM_LIMIT_BYTES),
        name="mix",
    )(x, mod, mod, mod, *[small[name] for name in _MIX_SMALL], *state_args, *prev, *guest_args,
      big["w_in"], w_gate, big["w_out"], *[src for src, _ in riders])
    n_out = 4 + len(guest_out_specs)
    return outs[:n_out], outs[n_out:]


def _ffn_kernel(*refs, layer, tT, nB, final, has_state, guest, n_ride):
    refs = list(refs)
    x_ref, sh_ref, sc_ref, g_ref, ng_ref, cw_ref, fg_ref = _take(refs, 7)
    stfc_ref = _take(refs, 1)[0] if has_state else None
    prev = _take(refs, 1) if layer else []
    if guest:
        xg_ref, shg_ref, scg_ref, gg_ref, stg_ref = _take(refs, 5)
        prev_g = _take(refs, 1) if layer else []
    wup_ref, wdn_ref = _take(refs, 2)
    ride_in = _take(refs, n_ride)
    y_ref, nfc_ref = _take(refs, 2)
    if guest:
        yg_ref, nfcg_ref = _take(refs, 2)
    ride_out = _take(refs, n_ride)
    hn_buf, u_buf, u_car, act_buf = _take(refs, 4)

    R = tT * nB
    tTg, nBg = guest if guest else (0, 0)
    Rg = tTg * nBg
    j = pl.program_id(1)
    gain = ng_ref[layer:layer + 1, :]

    @pl.when(j == 0)
    def _():
        _load_rows(u_car, stfc_ref, FFN_CONV - 1, nB)

    hn = _modulated_norm(x_ref[...], gain, sc_ref[...], sh_ref[...])
    hn_buf[0:R] = hn.reshape(R, D_MODEL).astype(BF16)
    if guest:
        ug_buf = refs[0]
        hn_g = _modulated_norm(xg_ref[...], gain, scg_ref[...], shg_ref[...])
        hn_buf[R:R + Rg] = hn_g.reshape(Rg, D_MODEL).astype(BF16)

    def conv_act(ub, n, rows, v, tap):
        uc = tap(0) * ub[0:rows]
        for k in range(1, FFN_CONV):
            uc = uc + tap(k) * ub[k * n:k * n + rows]
        return (jax.nn.gelu(uc) * v).astype(BF16)

    for ci, c0 in enumerate(range(0, D_FF, MXU_COLS)):
        cols = slice(c0, c0 + MXU_COLS)
        tap = lambda k: cw_ref[layer, k:k + 1, cols]
        u = _dot(hn_buf[...], wup_ref[:, cols])
        v = _dot(hn_buf[...], wup_ref[:, D_FF + c0:D_FF + c0 + MXU_COLS])
        ub = u_buf.at[ci % 2]
        ub[0:2 * nB] = u_car[:, cols]
        ub[2 * nB:2 * nB + R] = u[0:R]
        act_buf[0:R, cols] = conv_act(ub, nB, R, v[0:R], tap)
        u_car[:, cols] = ub[R:R + 2 * nB]
        if guest:
            ubg = ug_buf.at[ci % 2]
            for k in range(FFN_CONV - 1):
                ubg[k * nBg:(k + 1) * nBg] = stg_ref[:, k, cols]
            ubg[2 * nBg:2 * nBg + Rg] = u[R:R + Rg]
            act_buf[R:R + Rg, cols] = conv_act(ubg, nBg, Rg, v[R:R + Rg], tap)
            for k in range(FFN_CONV - 1):
                nfcg_ref[layer, :, k, cols] = ubg[Rg + k * nBg:Rg + (k + 1) * nBg]

    out = _dot(act_buf[...], wdn_ref[...])

    def finish(x_t, g_t, out_t, t_n, b_n, dst):
        xn = x_t + g_t[None] * out_t.reshape(t_n, b_n, D_MODEL)
        if final:
            ms = jnp.mean(xn * xn, axis=-1, keepdims=True)
            xn = xn * jax.lax.rsqrt(ms + EPS) * fg_ref[...][None]
            dst[...] = jnp.swapaxes(xn, 0, 1)
        else:
            dst[...] = xn

    finish(x_ref[...], g_ref[...], out[0:R], tT, nB, y_ref)
    if guest:
        finish(xg_ref[...], gg_ref[...], out[R:R + Rg], tTg, nBg, yg_ref)
        _keep_previous(prev_g, (nfcg_ref,))

    _keep_previous(prev, (nfc_ref,))
    _store_rows(nfc_ref, layer, u_car[...], FFN_CONV - 1, nB)
    _cast_riders(ride_in, ride_out)


def _ffn_call(l, x, mod, st_fc, prev, small, big, riders, *, tT, nB, final, guest=None):
    T, B, _ = x.shape
    R = tT * nB
    n_steps = T // tT
    assert not riders or B == nB
    guest_specs, guest_args, guest_out_specs, guest_out_shapes, guest_dims = [], [], [], [], None
    Rg = 0
    if guest is not None:
        xg, modg, stg, prevg = guest
        tTg, Bg, _ = xg.shape
        assert B == nB and Bg % n_steps == 0
        nBg = Bg // n_steps
        guest_dims, Rg = (tTg, nBg), tTg * nBg
        gmod = lambda k: pl.BlockSpec((None, nBg, D_MODEL), lambda i, j: (l, j, k))
        gfc = lambda n: pl.BlockSpec((n, nBg, FFN_CONV - 1, D_FF), lambda i, j: (0, j, 0, 0))
        guest_specs = [
            pl.BlockSpec((tTg, nBg, D_MODEL), lambda i, j: (0, j, 0)), gmod(3), gmod(4), gmod(5),
            pl.BlockSpec((None, nBg, FFN_CONV - 1, D_FF), lambda i, j: (l, j, 0, 0)),
            *([gfc(l)] if l else [])]
        guest_args = [xg, modg, modg, modg, stg, *prevg]
        if final:
            guest_out_specs = [pl.BlockSpec((nBg, tTg, D_MODEL), lambda i, j: (j, 0, 0))]
            guest_out_shapes = [jax.ShapeDtypeStruct((Bg, tTg, D_MODEL), F32)]
        else:
            guest_out_specs = [pl.BlockSpec((tTg, nBg, D_MODEL), lambda i, j: (0, j, 0))]
            guest_out_shapes = [jax.ShapeDtypeStruct((tTg, Bg, D_MODEL), F32)]
        guest_out_specs.append(gfc(l + 1))
        guest_out_shapes.append(jax.ShapeDtypeStruct((l + 1, Bg, FFN_CONV - 1, D_FF), F32))
    whole = lambda i, j: (0, 0)
    mod_spec = lambda k: pl.BlockSpec((None, nB, D_MODEL), lambda i, j: (l, i, k))
    once = pl.Buffered(1)
    fc_spec = lambda n: pl.BlockSpec((n, nB, FFN_CONV - 1, D_FF), lambda i, j: (0, i, 0, 0))
    state_specs, state_args = [], []
    if st_fc is not None:
        state_specs = [pl.BlockSpec((None, nB, FFN_CONV - 1, D_FF), lambda i, j: (l, i, 0, 0))]
        state_args = [st_fc]
    prev_specs = [fc_spec(l)] if l else []
    if final:
        y_spec = pl.BlockSpec((nB, tT, D_MODEL), lambda i, j: (i, j, 0))
        y_shape = jax.ShapeDtypeStruct((B, T, D_MODEL), F32)
    else:
        y_spec = pl.BlockSpec((tT, nB, D_MODEL), lambda i, j: (j, i, 0))
        y_shape = jax.ShapeDtypeStruct((T, B, D_MODEL), F32)
    r_in, r_out, r_shapes = _rider_specs(riders, n_steps, lambda i, j: j)
    outs = pl.pallas_call(
        functools.partial(_ffn_kernel, layer=l, tT=tT, nB=nB, final=final,
                          has_state=st_fc is not None, guest=guest_dims, n_ride=len(riders)),
        grid=(B // nB, n_steps),
        in_specs=[
            pl.BlockSpec((tT, nB, D_MODEL), lambda i, j: (j, i, 0)),
            mod_spec(3), mod_spec(4), mod_spec(5),
            _whole_spec(small["norm2_g"]), _whole_spec(small["ffn_conv_w"]),
            _whole_spec(small["final_g"]),
            *state_specs, *prev_specs, *guest_specs,
            pl.BlockSpec((D_MODEL, 2 * D_FF), whole, pipeline_mode=once),
            pl.BlockSpec((D_FF, D_MODEL), whole, pipeline_mode=once),
            *r_in,
        ],
        out_specs=[y_spec, fc_spec(l + 1), *guest_out_specs, *r_out],
        out_shape=[
            y_shape,
            jax.ShapeDtypeStruct((l + 1, B, FFN_CONV - 1, D_FF), F32),
            *guest_out_shapes,
            *r_shapes,
        ],
        scratch_shapes=[
            pltpu.VMEM((R + Rg, D_MODEL), BF16),
            pltpu.VMEM((2, R + 2 * nB, MXU_COLS), F32),
            pltpu.VMEM((2 * nB, D_FF), F32),
            pltpu.VMEM((R + Rg, D_FF), BF16),
            *([pltpu.VMEM((2, Rg + 2 * guest_dims[1], MXU_COLS), F32)] if guest else []),
        ],
        compiler_params=pltpu.CompilerParams(
            dimension_semantics=("arbitrary", "arbitrary"),
            vmem_limit_bytes=VMEM_LIMIT_BYTES),
        name="ffn",
    )(x, mod, mod, mod, small["norm2_g"], small["ffn_conv_w"], small["final_g"],
      *state_args, *prev, *guest_args, big["w_up"], big["w_dn"], *[src for src, _ in riders])
    n_out = 2 + len(guest_out_specs)
    return outs[:n_out], outs[n_out:]


def _gate_weights(lru_wa, lru_wx):
    eye = jnp.eye(LRU_HEADS, dtype=lru_wa.dtype)
    dense = lambda w: jnp.einsum("lhij,hk->lhikj", w, eye).reshape(DEPTH, D_HALF, D_HALF)
    wa, wx = dense(lru_wa), dense(lru_wx)
    ws = []
    for hf in range(D_HALF // MXU_COLS):
        s = slice(hf * MXU_COLS, (hf + 1) * MXU_COLS)
        ws.append(jnp.concatenate([wa[:, s, s], wx[:, s, s]], axis=-1))
    return jnp.stack(ws, axis=1).astype(BF16)


def kernel(x_prompt, x_sample, c_prompt, c_sample, state_lru_h, state_lru_conv, state_sc_conv, state_ffn_conv, w_ada, b_ada, norm1_g, norm2_g, w_in, lru_conv_w, lru_conv_b, lru_wa, lru_ba, lru_wx, lru_bx, lru_lambda, sc_conv_w, w_out, ffn_w_up, ffn_conv_w, ffn_w_down, final_g):
    small = dict(
        norm1_g=norm1_g, norm2_g=norm2_g, final_g=final_g[None, :], lru_conv_w=lru_conv_w,
        lru_conv_b=lru_conv_b, lru_ba=lru_ba, lru_bx=lru_bx, lru_lambda=lru_lambda,
        sc_conv_w=sc_conv_w, ffn_conv_w=ffn_conv_w)
    w_gate = _gate_weights(lru_wa, lru_wx)

    (mod_p, mod_s), cast = _ada_call(c_prompt, c_sample, w_ada, b_ada,
                                     [(w_in, 0), (w_out, 0)])
    w_bf = dict(w_in=cast[0], w_out=cast[1])

    tile_p = dict(tT=128, nB=x_prompt.shape[0])
    x_p, x_s = x_prompt, x_sample
    mix_state_s = (state_lru_h, jnp.swapaxes(state_lru_conv, 1, 2), state_sc_conv)
    mix_p, mix_s, fc_p, fc_s = [], [], [], []
    for l in range(DEPTH):
        (x_p, *mix_p, x_s, nh_s, nlc_s, nsc_s), cast = _mix_call(
            l, x_p, mod_p, None, mix_p, small, w_gate, w_bf, [(ffn_w_up, l), (ffn_w_down, l)],
            x_bm=(l == 0), guest=(x_s, mod_s, mix_state_s, mix_s), **tile_p)
        mix_s = [nh_s, nlc_s, nsc_s]
        w_bf = dict(w_up=cast[0], w_dn=cast[1])
        (x_p, nfc_p, x_s, nfc_s), cast = _ffn_call(
            l, x_p, mod_p, None, fc_p, small, w_bf,
            [(w_in, l + 1), (w_out, l + 1)] if l + 1 < DEPTH else [],
            final=(l == DEPTH - 1), guest=(x_s, mod_s, state_ffn_conv, fc_s), **tile_p)
        fc_p, fc_s = [nfc_p], [nfc_s]
        if cast:
            w_bf = dict(w_in=cast[0], w_out=cast[1])
    states = [(nh, jnp.swapaxes(nlc, 1, 2), nsc, nfc)
              for (nh, nlc, nsc), (nfc,) in ((mix_p, fc_p), (mix_s, fc_s))]
    return (x_p, x_s) + states[0] + states[1]
```

```python
import functools

import jax
import jax.numpy as jnp
from jax.experimental import pallas as pl
from jax.experimental.pallas import tpu as pltpu

D_MODEL = 1024
DEPTH = 2
D_HALF = 512
LRU_HEADS = 8
LRU_HEAD_DIM = 64
LRU_CONV = 4
SC_CONV = 3
FFN_CONV = 3
RG_C = 8.0
D_FF = 2816
EPS = 1e-6
N_MOD = 6

MXU_COLS = 256
BF16_SUBLANES = 16
VMEM_LIMIT_BYTES = 56 * 1024 * 1024
ADA_COL_BLOCKS = 4

BF16 = jnp.bfloat16
F32 = jnp.float32


def _dot(a, b):
    return jnp.dot(a, b, preferred_element_type=F32)


def _take(refs, n):
    head = refs[:n]
    del refs[:n]
    return head


def _whole_spec(arr):
    return pl.BlockSpec(arr.shape, lambda *g: (0,) * arr.ndim)


def _modulated_norm(x3, gain, scale, shift):
    ms = jnp.mean(x3 * x3, axis=-1, keepdims=True)
    y = x3 * jax.lax.rsqrt(ms + EPS)
    return y * (gain * (1.0 + scale))[None] + shift[None]


def _tie(buf, col0, value, never):
    rows, lanes = BF16_SUBLANES, 128
    fold = value[0:rows]
    for r0 in range(rows, value.shape[0], rows):
        fold = fold + value[r0:r0 + rows]
    dep = fold[:, 0:lanes]
    for c0 in range(lanes, fold.shape[1], lanes):
        dep = dep + fold[:, c0:c0 + lanes]
    tile = (slice(0, rows), slice(col0, col0 + lanes))
    buf[tile] = jnp.where(never, dep.astype(buf.dtype), buf[tile])


def _load_rows(buf, st_ref, n_rows, nB):
    if st_ref is None:
        buf[0:n_rows * nB] = jnp.zeros((n_rows * nB, buf.shape[1]), buf.dtype)
    else:
        for k in range(n_rows):
            buf[k * nB:(k + 1) * nB] = st_ref[:, k, :]


def _store_rows(out_ref, layer, tail, n_rows, nB):
    for k in range(n_rows):
        out_ref[layer, :, k, :] = tail[k * nB:(k + 1) * nB]


def _keep_previous(prev_refs, out_refs):
    for prev, out in zip(prev_refs, out_refs):
        out[0:prev.shape[0]] = prev[...]


def _rider_specs(riders, n_steps, step_of):
    in_specs, out_specs, out_shapes = [], [], []
    for src, layer in riders:
        _, K, N = src.shape
        hold = 1
        while (K * hold) % n_steps or (K * hold // n_steps) % BF16_SUBLANES:
            hold *= 2
            assert hold <= n_steps, (K, n_steps)
        rows = K * hold // n_steps
        in_specs.append(pl.BlockSpec(
            (None, rows, N), lambda *g, l=layer, h=hold: (l, step_of(*g) // h, 0)))
        out_specs.append(pl.BlockSpec((rows, N), lambda *g, h=hold: (step_of(*g) // h, 0)))
        out_shapes.append(jax.ShapeDtypeStruct((K, N), BF16))
    return in_specs, out_specs, out_shapes


def _cast_riders(ride_in, ride_out):
    for src, dst in zip(ride_in, ride_out):
        dst[...] = src[...].astype(BF16)


def _ada_kernel(*refs, n_ride):
    refs = list(refs)
    cp_ref, cs_ref, w_ref, b_ref = _take(refs, 4)
    ride_in = _take(refs, n_ride)
    op_ref, os_ref = _take(refs, 2)
    ride_out = _take(refs, n_ride)
    w = w_ref[...].astype(BF16)
    b = b_ref[pl.ds(pl.program_id(0) // ADA_COL_BLOCKS, 1), :]
    op_ref[...] = _dot(jax.nn.silu(cp_ref[...]).astype(BF16), w) + b
    os_ref[...] = _dot(jax.nn.silu(cs_ref[...]).astype(BF16), w) + b
    _cast_riders(ride_in, ride_out)


def _ada_call(c_prompt, c_sample, w_ada, b_ada, riders):
    n_p, n_s = c_prompt.shape[0], c_sample.shape[0]
    n_steps = DEPTH * ADA_COL_BLOCKS
    cols = N_MOD * D_MODEL // ADA_COL_BLOCKS
    lk = lambda s: (s // ADA_COL_BLOCKS, 0, s % ADA_COL_BLOCKS)
    c_spec = lambda n: pl.BlockSpec((n, D_MODEL), lambda s: (0, 0))
    o_spec = lambda n: pl.BlockSpec((None, n, cols), lk)
    r_in, r_out, r_shapes = _rider_specs(riders, n_steps, lambda s: s)
    outs = pl.pallas_call(
        functools.partial(_ada_kernel, n_ride=len(riders)),
        grid=(n_steps,),
        in_specs=[c_spec(n_p), c_spec(n_s),
                  pl.BlockSpec((None, D_MODEL, cols), lk),
                  pl.BlockSpec((DEPTH, cols), lambda s: (0, s % ADA_COL_BLOCKS)),
                  *r_in],
        out_specs=[o_spec(n_p), o_spec(n_s), *r_out],
        out_shape=[jax.ShapeDtypeStruct((DEPTH, n_p, N_MOD * D_MODEL), F32),
                   jax.ShapeDtypeStruct((DEPTH, n_s, N_MOD * D_MODEL), F32),
                   *r_shapes],
        name="ada",
    )(c_prompt, c_sample, w_ada, b_ada, *[src for src, _ in riders])
    return outs[:2], outs[2:]


def _mix_kernel(*refs, layer, tT, nB, n_tiles, x_bm, has_state, guest, n_ride):
    refs = list(refs)
    x_ref, sh_ref, sc_ref, g_ref = _take(refs, 4)
    ng_ref, cw_ref, cb_ref, ba_ref, bx_ref, lam_ref, scw_ref = _take(refs, 7)
    sth_ref, stlc_ref, stsc_ref = _take(refs, 3) if has_state else (None, None, None)
    prev = _take(refs, 3) if layer else []
    if guest:
        xg_ref, shg_ref, scg_ref, gg_ref, sthg_ref, stlcg_ref, stscg_ref = _take(refs, 7)
        prev_g = _take(refs, 3) if layer else []
    win_ref, wg_ref, wout_ref = _take(refs, 3)
    ride_in = _take(refs, n_ride)
    y_ref, nh_ref, nlc_ref, nsc_ref = _take(refs, 4)
    if guest:
        yg_ref, nhg_ref, nlcg_ref, nscg_ref = _take(refs, 4)
    ride_out = _take(refs, n_ride)
    hn_buf, xl_buf, z_buf, a_buf, b_buf, mix_buf, h_car = _take(refs, 7)
    xt_buf, xt_sem = _take(refs, 2) if x_bm else (None, None)
    xlg_buf, zg_buf = refs if guest else (None, None)

    R = tT * nB
    tTg, nBg = guest if guest else (0, 0)
    Rg = tTg * nBg
    Rt = R + Rg
    C = D_HALF
    j = pl.program_id(1)
    row = lambda ref: ref[layer:layer + 1, :]
    tap = lambda ref, k: ref[layer, k:k + 1, :]

    def conv(buf, w_ref, taps, n, rows, bias=None):
        acc = tap(w_ref, 0) * buf[0:rows]
        if bias is not None:
            acc = bias + acc
        for k in range(1, taps):
            acc = acc + tap(w_ref, k) * buf[k * n:k * n + rows]
        return acc

    @pl.when(j == 0)
    def _():
        if has_state:
            xl_buf[0:3 * nB] = stlc_ref[...].reshape(3 * nB, C)
            h_car[...] = sth_ref[...]
        else:
            xl_buf[0:3 * nB] = jnp.zeros((3 * nB, C), F32)
            h_car[...] = jnp.zeros((nB, C), F32)
        _load_rows(z_buf, stsc_ref, SC_CONV - 1, nB)

    if x_bm:
        slot = j % 2

        def fetch(tile, half):
            return [pltpu.make_async_copy(x_ref.at[b, pl.ds(tile * tT, tT), :],
                                          xt_buf.at[half, :, b, :], xt_sem.at[half, b])
                    for b in range(nB)]

        @pl.when(j == 0)
        def _():
            for copy in fetch(0, 0):
                copy.start()

        @pl.when(j + 1 < n_tiles)
        def _():
            for copy in fetch(j + 1, 1 - slot):
                copy.start()

        for copy in fetch(j, slot):
            copy.wait()
        x_tm = xt_buf.at[slot]
    else:
        x_tm = x_ref
    hn = _modulated_norm(x_tm[...], row(ng_ref), sc_ref[...], sh_ref[...])
    hn_buf[0:R] = hn.reshape(R, D_MODEL).astype(BF16)
    if guest:
        xg_tm = jnp.swapaxes(xg_ref[...], 0, 1) if x_bm else xg_ref[...]
        hn_g = _modulated_norm(xg_tm, row(ng_ref), scg_ref[...], shg_ref[...])
        hn_buf[R:Rt] = hn_g.reshape(Rg, D_MODEL).astype(BF16)

    in_proj = lambda k: _dot(hn_buf[...], win_ref[:, k * C:(k + 1) * C])

    xl = in_proj(0)
    xl_buf[3 * nB:3 * nB + R] = xl[0:R]
    b_buf[0:R] = conv(xl_buf, cw_ref, LRU_CONV, nB, R, row(cb_ref))
    if guest:
        xlg_buf[0:3 * nBg] = stlcg_ref[...].reshape(3 * nBg, C)
        xlg_buf[3 * nBg:3 * nBg + Rg] = xl[R:Rt]
        b_buf[R:Rt] = conv(xlg_buf, cw_ref, LRU_CONV, nBg, Rg, row(cb_ref))

    neg_c_sp = -RG_C * jnp.logaddexp(-row(lam_ref), 0.0)
    short = []
    for hf in range(C // MXU_COLS):
        lo, hi = hf * MXU_COLS, (hf + 1) * MXU_COLS
        xc = b_buf[:, lo:hi]
        gates = _dot(xc.astype(BF16), wg_ref[hf])
        r = jax.nn.sigmoid(gates[:, :MXU_COLS] + row(ba_ref)[:, lo:hi])
        ig = jax.nn.sigmoid(gates[:, MXU_COLS:] + row(bx_ref)[:, lo:hi])
        log_a = r * neg_c_sp[:, lo:hi]
        a = jnp.exp(log_a)
        mult = jnp.sqrt(-jnp.tanh(log_a) * (1.0 + a * a))
        a_buf[:, lo:hi] = a
        b_buf[:, lo:hi] = mult * (ig * xc)

        short.append(in_proj(3 + hf))
        _tie(b_buf, lo, short[hf], j < 0)

        def recur(h, first, steps, n):
            for t in range(steps):
                rows = pl.ds(first + t * n, n)
                h = a_buf[rows, lo:hi] * h + b_buf[rows, lo:hi]
                b_buf[rows, lo:hi] = h
            return h
        h_car[:, lo:hi] = recur(h_car[:, lo:hi], 0, tT, nB)
        if guest:
            nhg_ref[layer, :, lo:hi] = recur(sthg_ref[:, lo:hi], R, tTg, nBg)

    mix_buf[:, 0:C] = (b_buf[...] * jax.nn.gelu(in_proj(1))).astype(BF16)

    z = short[0] * short[1]
    bs = in_proj(2)
    z_buf[2 * nB:2 * nB + R] = z[0:R]
    mix_buf[0:R, C:2 * C] = (bs[0:R] * conv(z_buf, scw_ref, SC_CONV, nB, R)).astype(BF16)
    if guest:
        _load_rows(zg_buf, stscg_ref, SC_CONV - 1, nBg)
        zg_buf[2 * nBg:2 * nBg + Rg] = z[R:Rt]
        mix_buf[R:Rt, C:2 * C] = (bs[R:Rt] * conv(zg_buf, scw_ref, SC_CONV, nBg, Rg)).astype(BF16)

    mix = _dot(mix_buf[...], wout_ref[...])
    y_ref[...] = x_tm[...] + g_ref[...][None] * mix[0:R].reshape(tT, nB, D_MODEL)
    if guest:
        yg_ref[...] = xg_tm + gg_ref[...][None] * mix[R:Rt].reshape(tTg, nBg, D_MODEL)
        _keep_previous(prev_g, (nhg_ref, nlcg_ref, nscg_ref))
        nlcg_ref[layer] = xlg_buf[Rg:Rg + 3 * nBg].reshape(LRU_CONV - 1, nBg, C)
        _store_rows(nscg_ref, layer, zg_buf[Rg:Rg + 2 * nBg], SC_CONV - 1, nBg)

    lc_tail = xl_buf[R:R + 3 * nB]
    sc_tail = z_buf[R:R + 2 * nB]
    _keep_previous(prev, (nh_ref, nlc_ref, nsc_ref))
    nh_ref[layer] = h_car[...]
    nlc_ref[layer] = lc_tail.reshape(LRU_CONV - 1, nB, C)
    _store_rows(nsc_ref, layer, sc_tail, SC_CONV - 1, nB)
    xl_buf[0:3 * nB] = lc_tail
    z_buf[0:2 * nB] = sc_tail

    _cast_riders(ride_in, ride_out)


_MIX_SMALL = ("norm1_g", "lru_conv_w", "lru_conv_b", "lru_ba", "lru_bx", "lru_lambda", "sc_conv_w")


def _mix_call(l, x, mod, state, prev, small, w_gate, big, riders, *, tT, nB, x_bm, guest=None):
    if x_bm:
        B, T, _ = x.shape
        assert B == nB
        x_spec = pl.BlockSpec(memory_space=pl.ANY)
    else:
        T, B, _ = x.shape
        x_spec = pl.BlockSpec((tT, nB, D_MODEL), lambda i, j: (j, i, 0))
    R = tT * nB
    C = D_HALF
    n_steps = T // tT
    assert not riders or B == nB
    guest_specs, guest_args, guest_out_specs, guest_out_shapes, guest_dims = [], [], [], [], None
    Rg = 0
    if guest is not None:
        xg, modg, stateg, prevg = guest
        Bg, tTg = (xg.shape[0], xg.shape[1]) if x_bm else (xg.shape[1], xg.shape[0])
        assert B == nB and Bg % n_steps == 0
        nBg = Bg // n_steps
        guest_dims, Rg = (tTg, nBg), tTg * nBg
        gx = (pl.BlockSpec((nBg, tTg, D_MODEL), lambda i, j: (j, 0, 0)) if x_bm else
              pl.BlockSpec((tTg, nBg, D_MODEL), lambda i, j: (0, j, 0)))
        gmod = lambda k: pl.BlockSpec((None, nBg, D_MODEL), lambda i, j: (l, j, k))
        gh = lambda n, ll: pl.BlockSpec((n, nBg, C), lambda i, j: (ll, j, 0))
        glc = lambda n, ll: pl.BlockSpec((n, LRU_CONV - 1, nBg, C), lambda i, j: (ll, 0, j, 0))
        gsc = lambda n, ll: pl.BlockSpec((n, nBg, SC_CONV - 1, C), lambda i, j: (ll, j, 0, 0))
        guest_specs = [gx, gmod(0), gmod(1), gmod(2), gh(None, l), glc(None, l), gsc(None, l),
                       *([gh(l, 0), glc(l, 0), gsc(l, 0)] if l else [])]
        guest_args = [xg, modg, modg, modg, *stateg, *prevg]
        guest_out_specs = [pl.BlockSpec((tTg, nBg, D_MODEL), lambda i, j: (0, j, 0)),
                           gh(l + 1, 0), glc(l + 1, 0), gsc(l + 1, 0)]
        guest_out_shapes = [jax.ShapeDtypeStruct((tTg, Bg, D_MODEL), F32),
                            jax.ShapeDtypeStruct((l + 1, Bg, C), F32),
                            jax.ShapeDtypeStruct((l + 1, LRU_CONV - 1, Bg, C), F32),
                            jax.ShapeDtypeStruct((l + 1, Bg, SC_CONV - 1, C), F32)]
    whole = lambda i, j: (0, 0)
    mod_spec = lambda k: pl.BlockSpec((None, nB, D_MODEL), lambda i, j: (l, i, k))
    once = pl.Buffered(1)
    h_spec = lambda n: pl.BlockSpec((n, nB, C), lambda i, j: (0, i, 0))
    lc_spec = lambda n: pl.BlockSpec((n, LRU_CONV - 1, nB, C), lambda i, j: (0, 0, i, 0))
    sc_spec = lambda n: pl.BlockSpec((n, nB, SC_CONV - 1, C), lambda i, j: (0, i, 0, 0))
    state_specs, state_args = [], []
    if state is not None:
        state_specs = [
            pl.BlockSpec((None, nB, C), lambda i, j: (l, i, 0)),
            pl.BlockSpec((None, LRU_CONV - 1, nB, C), lambda i, j: (l, 0, i, 0)),
            pl.BlockSpec((None, nB, SC_CONV - 1, C), lambda i, j: (l, i, 0, 0)),
        ]
        state_args = list(state)
    prev_specs = [h_spec(l), lc_spec(l), sc_spec(l)] if l else []
    r_in, r_out, r_shapes = _rider_specs(riders, n_steps, lambda i, j: j)
    scratch = [
        pltpu.VMEM((R + Rg, D_MODEL), BF16),
        pltpu.VMEM((R + 3 * nB, C), F32),
        pltpu.VMEM((R + 2 * nB, C), F32),
        pltpu.VMEM((R + Rg, C), F32),
        pltpu.VMEM((R + Rg, C), F32),
        pltpu.VMEM((R + Rg, D_MODEL), BF16),
        pltpu.VMEM((nB, C), F32),
    ]
    if x_bm:
        scratch += [pltpu.VMEM((2, tT, nB, D_MODEL), F32),
                    pltpu.SemaphoreType.DMA((2, nB))]
    if guest is not None:
        scratch += [pltpu.VMEM((Rg + (LRU_CONV - 1) * guest_dims[1], C), F32),
                    pltpu.VMEM((Rg + (SC_CONV - 1) * guest_dims[1], C), F32)]
    outs = pl.pallas_call(
        functools.partial(_mix_kernel, layer=l, tT=tT, nB=nB, n_tiles=n_steps, x_bm=x_bm,
                          has_state=state is not None, guest=guest_dims, n_ride=len(riders)),
        grid=(B // nB, n_steps),
        in_specs=[
            x_spec, mod_spec(0), mod_spec(1), mod_spec(2),
            *[_whole_spec(small[name]) for name in _MIX_SMALL],
            *state_specs, *prev_specs, *guest_specs,
            pl.BlockSpec((D_MODEL, 5 * C), whole, pipeline_mode=once),
            pl.BlockSpec((None, C // MXU_COLS, MXU_COLS, 2 * MXU_COLS),
                         lambda i, j: (l, 0, 0, 0), pipeline_mode=once),
            pl.BlockSpec((D_MODEL, D_MODEL), whole, pipeline_mode=once),
            *r_in,
        ],
        out_specs=[
            pl.BlockSpec((tT, nB, D_MODEL), lambda i, j: (j, i, 0)),
            h_spec(l + 1), lc_spec(l + 1), sc_spec(l + 1),
            *guest_out_specs, *r_out,
        ],
        out_shape=[
            jax.ShapeDtypeStruct((T, B, D_MODEL), F32),
            jax.ShapeDtypeStruct((l + 1, B, C), F32),
            jax.ShapeDtypeStruct((l + 1, LRU_CONV - 1, B, C), F32),
            jax.ShapeDtypeStruct((l + 1, B, SC_CONV - 1, C), F32),
            *guest_out_shapes, *r_shapes,
        ],
        scratch_shapes=scratch,
        compiler_params=pltpu.CompilerParams(
            dimension_semantics=("arbitrary", "arbitrary"),
            vmem_limit_bytes=VMEM_LIMIT_BYTES),
        name="mix",
    )(x, mod, mod, mod, *[small[name] for name in _MIX_SMALL], *state_args, *prev, *guest_args,
      big["w_in"], w_gate, big["w_out"], *[src for src, _ in riders])
    n_out = 4 + len(guest_out_specs)
    return outs[:n_out], outs[n_out:]


def _ffn_kernel(*refs, layer, tT, nB, n_tiles, final, has_state, guest, n_ride):
    refs = list(refs)
    x_ref, sh_ref, sc_ref, g_ref, ng_ref, cw_ref, fg_ref = _take(refs, 7)
    stfc_ref = _take(refs, 1)[0] if has_state else None
    prev = _take(refs, 1) if layer else []
    if guest:
        xg_ref, shg_ref, scg_ref, gg_ref, stg_ref = _take(refs, 5)
        prev_g = _take(refs, 1) if layer else []
    wup_ref, wdn_ref = _take(refs, 2)
    ride_in = _take(refs, n_ride)
    y_ref, nfc_ref = _take(refs, 2)
    if guest:
        yg_ref, nfcg_ref = _take(refs, 2)
    ride_out = _take(refs, n_ride)
    hn_buf, u_buf, u_car, act_buf = _take(refs, 4)
    yt_buf, yt_sem = _take(refs, 2) if final else (None, None)

    R = tT * nB
    tTg, nBg = guest if guest else (0, 0)
    Rg = tTg * nBg
    j = pl.program_id(1)
    gain = ng_ref[layer:layer + 1, :]

    @pl.when(j == 0)
    def _():
        _load_rows(u_car, stfc_ref, FFN_CONV - 1, nB)

    hn = _modulated_norm(x_ref[...], gain, sc_ref[...], sh_ref[...])
    hn_buf[0:R] = hn.reshape(R, D_MODEL).astype(BF16)
    if guest:
        ug_buf = refs[0]
        hn_g = _modulated_norm(xg_ref[...], gain, scg_ref[...], shg_ref[...])
        hn_buf[R:R + Rg] = hn_g.reshape(Rg, D_MODEL).astype(BF16)

    def conv_act(ub, n, rows, v, tap):
        uc = tap(0) * ub[0:rows]
        for k in range(1, FFN_CONV):
            uc = uc + tap(k) * ub[k * n:k * n + rows]
        return (jax.nn.gelu(uc) * v).astype(BF16)

    for ci, c0 in enumerate(range(0, D_FF, MXU_COLS)):
        cols = slice(c0, c0 + MXU_COLS)
        tap = lambda k: cw_ref[layer, k:k + 1, cols]
        u = _dot(hn_buf[...], wup_ref[:, cols])
        v = _dot(hn_buf[...], wup_ref[:, D_FF + c0:D_FF + c0 + MXU_COLS])
        ub = u_buf.at[ci % 2]
        ub[0:2 * nB] = u_car[:, cols]
        ub[2 * nB:2 * nB + R] = u[0:R]
        act_buf[0:R, cols] = conv_act(ub, nB, R, v[0:R], tap)
        u_car[:, cols] = ub[R:R + 2 * nB]
        if guest:
            ubg = ug_buf.at[ci % 2]
            for k in range(FFN_CONV - 1):
                ubg[k * nBg:(k + 1) * nBg] = stg_ref[:, k, cols]
            ubg[2 * nBg:2 * nBg + Rg] = u[R:R + Rg]
            act_buf[R:R + Rg, cols] = conv_act(ubg, nBg, Rg, v[R:R + Rg], tap)
            for k in range(FFN_CONV - 1):
                nfcg_ref[layer, :, k, cols] = ubg[Rg + k * nBg:Rg + (k + 1) * nBg]

    out = _dot(act_buf[...], wdn_ref[...])

    def residual(x_t, g_t, out_t, t_n, b_n):
        xn = x_t + g_t[None] * out_t.reshape(t_n, b_n, D_MODEL)
        if final:
            ms = jnp.mean(xn * xn, axis=-1, keepdims=True)
            xn = xn * jax.lax.rsqrt(ms + EPS) * fg_ref[...][None]
        return xn

    y_host = residual(x_ref[...], g_ref[...], out[0:R], tT, nB)
    if final:
        slot = j % 2

        def send(tile, half):
            return [pltpu.make_async_copy(yt_buf.at[half, :, b, :],
                                          y_ref.at[b, pl.ds(tile * tT, tT), :], yt_sem.at[half, b])
                    for b in range(nB)]

        @pl.when(j >= 2)
        def _():
            for copy in send(j - 2, slot):
                copy.wait()

        yt_buf[slot] = y_host
        for copy in send(j, slot):
            copy.start()

        @pl.when(j == n_tiles - 1)
        def _():
            for tile, half in ((j - 1, 1 - slot), (j, slot))[0 if n_tiles > 1 else 1:]:
                for copy in send(tile, half):
                    copy.wait()
    else:
        y_ref[...] = y_host
    if guest:
        y_guest = residual(xg_ref[...], gg_ref[...], out[R:R + Rg], tTg, nBg)
        yg_ref[...] = jnp.swapaxes(y_guest, 0, 1) if final else y_guest
        _keep_previous(prev_g, (nfcg_ref,))

    _keep_previous(prev, (nfc_ref,))
    _store_rows(nfc_ref, layer, u_car[...], FFN_CONV - 1, nB)
    _cast_riders(ride_in, ride_out)


def _ffn_call(l, x, mod, st_fc, prev, small, big, riders, *, tT, nB, final, guest=None):
    T, B, _ = x.shape
    R = tT * nB
    n_steps = T // tT
    assert not riders or B == nB
    guest_specs, guest_args, guest_out_specs, guest_out_shapes, guest_dims = [], [], [], [], None
    Rg = 0
    if guest is not None:
        xg, modg, stg, prevg = guest
        tTg, Bg, _ = xg.shape
        assert B == nB and Bg % n_steps == 0
        nBg = Bg // n_steps
        guest_dims, Rg = (tTg, nBg), tTg * nBg
        gmod = lambda k: pl.BlockSpec((None, nBg, D_MODEL), lambda i, j: (l, j, k))
        gfc = lambda n: pl.BlockSpec((n, nBg, FFN_CONV - 1, D_FF), lambda i, j: (0, j, 0, 0))
        guest_specs = [
            pl.BlockSpec((tTg, nBg, D_MODEL), lambda i, j: (0, j, 0)), gmod(3), gmod(4), gmod(5),
            pl.BlockSpec((None, nBg, FFN_CONV - 1, D_FF), lambda i, j: (l, j, 0, 0)),
            *([gfc(l)] if l else [])]
        guest_args = [xg, modg, modg, modg, stg, *prevg]
        if final:
            guest_out_specs = [pl.BlockSpec((nBg, tTg, D_MODEL), lambda i, j: (j, 0, 0))]
            guest_out_shapes = [jax.ShapeDtypeStruct((Bg, tTg, D_MODEL), F32)]
        else:
            guest_out_specs = [pl.BlockSpec((tTg, nBg, D_MODEL), lambda i, j: (0, j, 0))]
            guest_out_shapes = [jax.ShapeDtypeStruct((tTg, Bg, D_MODEL), F32)]
        guest_out_specs.append(gfc(l + 1))
        guest_out_shapes.append(jax.ShapeDtypeStruct((l + 1, Bg, FFN_CONV - 1, D_FF), F32))
    whole = lambda i, j: (0, 0)
    mod_spec = lambda k: pl.BlockSpec((None, nB, D_MODEL), lambda i, j: (l, i, k))
    once = pl.Buffered(1)
    fc_spec = lambda n: pl.BlockSpec((n, nB, FFN_CONV - 1, D_FF), lambda i, j: (0, i, 0, 0))
    state_specs, state_args = [], []
    if st_fc is not None:
        state_specs = [pl.BlockSpec((None, nB, FFN_CONV - 1, D_FF), lambda i, j: (l, i, 0, 0))]
        state_args = [st_fc]
    prev_specs = [fc_spec(l)] if l else []
    if final:
        assert B == nB
        y_spec = pl.BlockSpec(memory_space=pl.ANY)
        y_shape = jax.ShapeDtypeStruct((B, T, D_MODEL), F32)
    else:
        y_spec = pl.BlockSpec((tT, nB, D_MODEL), lambda i, j: (j, i, 0))
        y_shape = jax.ShapeDtypeStruct((T, B, D_MODEL), F32)
    r_in, r_out, r_shapes = _rider_specs(riders, n_steps, lambda i, j: j)
    outs = pl.pallas_call(
        functools.partial(_ffn_kernel, layer=l, tT=tT, nB=nB, n_tiles=n_steps, final=final,
                          has_state=st_fc is not None, guest=guest_dims, n_ride=len(riders)),
        grid=(B // nB, n_steps),
        in_specs=[
            pl.BlockSpec((tT, nB, D_MODEL), lambda i, j: (j, i, 0)),
            mod_spec(3), mod_spec(4), mod_spec(5),
            _whole_spec(small["norm2_g"]), _whole_spec(small["ffn_conv_w"]),
            _whole_spec(small["final_g"]),
            *state_specs, *prev_specs, *guest_specs,
            pl.BlockSpec((D_MODEL, 2 * D_FF), whole, pipeline_mode=once),
            pl.BlockSpec((D_FF, D_MODEL), whole, pipeline_mode=once),
            *r_in,
        ],
        out_specs=[y_spec, fc_spec(l + 1), *guest_out_specs, *r_out],
        out_shape=[
            y_shape,
            jax.ShapeDtypeStruct((l + 1, B, FFN_CONV - 1, D_FF), F32),
            *guest_out_shapes,
            *r_shapes,
        ],
        scratch_shapes=[
            pltpu.VMEM((R + Rg, D_MODEL), BF16),
            pltpu.VMEM((2, R + 2 * nB, MXU_COLS), F32),
            pltpu.VMEM((2 * nB, D_FF), F32),
            pltpu.VMEM((R + Rg, D_FF), BF16),
            *([pltpu.VMEM((2, tT, nB, D_MODEL), F32),
               pltpu.SemaphoreType.DMA((2, nB))] if final else []),
            *([pltpu.VMEM((2, Rg + 2 * guest_dims[1], MXU_COLS), F32)] if guest else []),
        ],
        compiler_params=pltpu.CompilerParams(
            dimension_semantics=("arbitrary", "arbitrary"),
            vmem_limit_bytes=VMEM_LIMIT_BYTES),
        name="ffn",
    )(x, mod, mod, mod, small["norm2_g"], small["ffn_conv_w"], small["final_g"],
      *state_args, *prev, *guest_args, big["w_up"], big["w_dn"], *[src for src, _ in riders])
    n_out = 2 + len(guest_out_specs)
    return outs[:n_out], outs[n_out:]


def _gate_weights(lru_wa, lru_wx):
    eye = jnp.eye(LRU_HEADS, dtype=lru_wa.dtype)
    dense = lambda w: jnp.einsum("lhij,hk->lhikj", w, eye).reshape(DEPTH, D_HALF, D_HALF)
    wa, wx = dense(lru_wa), dense(lru_wx)
    ws = []
    for hf in range(D_HALF // MXU_COLS):
        s = slice(hf * MXU_COLS, (hf + 1) * MXU_COLS)
        ws.append(jnp.concatenate([wa[:, s, s], wx[:, s, s]], axis=-1))
    return jnp.stack(ws, axis=1).astype(BF16)


def kernel(x_prompt, x_sample, c_prompt, c_sample, state_lru_h, state_lru_conv, state_sc_conv, state_ffn_conv, w_ada, b_ada, norm1_g, norm2_g, w_in, lru_conv_w, lru_conv_b, lru_wa, lru_ba, lru_wx, lru_bx, lru_lambda, sc_conv_w, w_out, ffn_w_up, ffn_conv_w, ffn_w_down, final_g):
    small = dict(
        norm1_g=norm1_g, norm2_g=norm2_g, final_g=final_g[None, :], lru_conv_w=lru_conv_w,
        lru_conv_b=lru_conv_b, lru_ba=lru_ba, lru_bx=lru_bx, lru_lambda=lru_lambda,
        sc_conv_w=sc_conv_w, ffn_conv_w=ffn_conv_w)
    w_gate = _gate_weights(lru_wa, lru_wx)

    (mod_p, mod_s), cast = _ada_call(c_prompt, c_sample, w_ada, b_ada,
                                     [(w_in, 0), (w_out, 0)])
    w_bf = dict(w_in=cast[0], w_out=cast[1])

    tile_p = dict(tT=128, nB=x_prompt.shape[0])
    x_p, x_s = x_prompt, x_sample
    mix_state_s = (state_lru_h, jnp.swapaxes(state_lru_conv, 1, 2), state_sc_conv)
    mix_p, mix_s, fc_p, fc_s = [], [], [], []
    for l in range(DEPTH):
        (x_p, *mix_p, x_s, nh_s, nlc_s, nsc_s), cast = _mix_call(
            l, x_p, mod_p, None, mix_p, small, w_gate, w_bf, [(ffn_w_up, l), (ffn_w_down, l)],
            x_bm=(l == 0), guest=(x_s, mod_s, mix_state_s, mix_s), **tile_p)
        mix_s = [nh_s, nlc_s, nsc_s]
        w_bf = dict(w_up=cast[0], w_dn=cast[1])
        (x_p, nfc_p, x_s, nfc_s), cast = _ffn_call(
            l, x_p, mod_p, None, fc_p, small, w_bf,
            [(w_in, l + 1), (w_out, l + 1)] if l + 1 < DEPTH else [],
            final=(l == DEPTH - 1), guest=(x_s, mod_s, state_ffn_conv, fc_s), **tile_p)
        fc_p, fc_s = [nfc_p], [nfc_s]
        if cast:
            w_bf = dict(w_in=cast[0], w_out=cast[1])
    states = [(nh, jnp.swapaxes(nlc, 1, 2), nsc, nfc)
              for (nh, nlc, nsc), (nfc,) in ((mix_p, fc_p), (mix_s, fc_s))]
    return (x_p, x_s) + states[0] + states[1]
```

```python
import functools

import jax
import jax.numpy as jnp
from jax.experimental import pallas as pl
from jax.experimental.pallas import tpu as pltpu

D_MODEL = 1024
DEPTH = 2
D_HALF = 512
LRU_HEADS = 8
LRU_HEAD_DIM = 64
LRU_CONV = 4
SC_CONV = 3
FFN_CONV = 3
RG_C = 8.0
D_FF = 2816
EPS = 1e-6
N_MOD = 6

MXU_COLS = 256
BF16_SUBLANES = 16
VMEM_LIMIT_BYTES = 56 * 1024 * 1024
ADA_COL_BLOCKS = 4
HOST_TILE_ROWS = 1024

BF16 = jnp.bfloat16
F32 = jnp.float32


def _dot(a, b):
    return jnp.dot(a, b, preferred_element_type=F32)


def _take(refs, n):
    head = refs[:n]
    del refs[:n]
    return head


def _whole_spec(arr):
    return pl.BlockSpec(arr.shape, lambda *g: (0,) * arr.ndim)


def _modulated_norm(x3, gain, scale, shift):
    ms = jnp.mean(x3 * x3, axis=-1, keepdims=True)
    y = x3 * jax.lax.rsqrt(ms + EPS)
    return y * (gain * (1.0 + scale))[None] + shift[None]


def _tie(buf, col0, value, never):
    rows, lanes = BF16_SUBLANES, 128
    fold = value[0:rows]
    for r0 in range(rows, value.shape[0], rows):
        fold = fold + value[r0:r0 + rows]
    dep = fold[:, 0:lanes]
    for c0 in range(lanes, fold.shape[1], lanes):
        dep = dep + fold[:, c0:c0 + lanes]
    tile = (slice(0, rows), slice(col0, col0 + lanes))
    buf[tile] = jnp.where(never, dep.astype(buf.dtype), buf[tile])


def _load_rows(buf, st_ref, n_rows, nB):
    if st_ref is None:
        buf[0:n_rows * nB] = jnp.zeros((n_rows * nB, buf.shape[1]), buf.dtype)
    else:
        for k in range(n_rows):
            buf[k * nB:(k + 1) * nB] = st_ref[:, k, :]


def _store_rows(out_ref, layer, tail, n_rows, nB):
    for k in range(n_rows):
        out_ref[layer, :, k, :] = tail[k * nB:(k + 1) * nB]


def _keep_previous(prev_refs, out_refs):
    for prev, out in zip(prev_refs, out_refs):
        out[0:prev.shape[0]] = prev[...]


def _rider_specs(riders, n_steps, step_of):
    in_specs, out_specs, out_shapes = [], [], []
    for src, layer in riders:
        _, K, N = src.shape
        hold = 1
        while (K * hold) % n_steps or (K * hold // n_steps) % BF16_SUBLANES:
            hold *= 2
            assert hold <= n_steps, (K, n_steps)
        rows = K * hold // n_steps
        in_specs.append(pl.BlockSpec(
            (None, rows, N), lambda *g, l=layer, h=hold: (l, step_of(*g) // h, 0)))
        out_specs.append(pl.BlockSpec((rows, N), lambda *g, h=hold: (step_of(*g) // h, 0)))
        out_shapes.append(jax.ShapeDtypeStruct((K, N), BF16))
    return in_specs, out_specs, out_shapes


def _cast_riders(ride_in, ride_out):
    for src, dst in zip(ride_in, ride_out):
        dst[...] = src[...].astype(BF16)


def _ada_kernel(*refs, n_ride):
    refs = list(refs)
    cp_ref, cs_ref, w_ref, b_ref = _take(refs, 4)
    ride_in = _take(refs, n_ride)
    op_ref, os_ref = _take(refs, 2)
    ride_out = _take(refs, n_ride)
    w = w_ref[...].astype(BF16)
    b = b_ref[pl.ds(pl.program_id(0) // ADA_COL_BLOCKS, 1), :]
    op_ref[...] = _dot(jax.nn.silu(cp_ref[...]).astype(BF16), w) + b
    os_ref[...] = _dot(jax.nn.silu(cs_ref[...]).astype(BF16), w) + b
    _cast_riders(ride_in, ride_out)


def _ada_call(c_prompt, c_sample, w_ada, b_ada, riders):
    n_p, n_s = c_prompt.shape[0], c_sample.shape[0]
    n_steps = DEPTH * ADA_COL_BLOCKS
    cols = N_MOD * D_MODEL // ADA_COL_BLOCKS
    lk = lambda s: (s // ADA_COL_BLOCKS, 0, s % ADA_COL_BLOCKS)
    c_spec = lambda n: pl.BlockSpec((n, D_MODEL), lambda s: (0, 0))
    o_spec = lambda n: pl.BlockSpec((None, n, cols), lk)
    r_in, r_out, r_shapes = _rider_specs(riders, n_steps, lambda s: s)
    outs = pl.pallas_call(
        functools.partial(_ada_kernel, n_ride=len(riders)),
        grid=(n_steps,),
        in_specs=[c_spec(n_p), c_spec(n_s),
                  pl.BlockSpec((None, D_MODEL, cols), lk),
                  pl.BlockSpec((DEPTH, cols), lambda s: (0, s % ADA_COL_BLOCKS)),
                  *r_in],
        out_specs=[o_spec(n_p), o_spec(n_s), *r_out],
        out_shape=[jax.ShapeDtypeStruct((DEPTH, n_p, N_MOD * D_MODEL), F32),
                   jax.ShapeDtypeStruct((DEPTH, n_s, N_MOD * D_MODEL), F32),
                   *r_shapes],
        name="ada",
    )(c_prompt, c_sample, w_ada, b_ada, *[src for src, _ in riders])
    return outs[:2], outs[2:]


def _mix_kernel(*refs, layer, tT, nB, n_tiles, x_bm, has_state, guest, n_ride):
    refs = list(refs)
    x_ref, sh_ref, sc_ref, g_ref = _take(refs, 4)
    ng_ref, cw_ref, cb_ref, ba_ref, bx_ref, lam_ref, scw_ref = _take(refs, 7)
    sth_ref, stlc_ref, stsc_ref = _take(refs, 3) if has_state else (None, None, None)
    prev = _take(refs, 3) if layer else []
    if guest:
        xg_ref, shg_ref, scg_ref, gg_ref, sthg_ref, stlcg_ref, stscg_ref = _take(refs, 7)
        prev_g = _take(refs, 3) if layer else []
    win_ref, wg_ref, wout_ref = _take(refs, 3)
    ride_in = _take(refs, n_ride)
    y_ref, nh_ref, nlc_ref, nsc_ref = _take(refs, 4)
    if guest:
        yg_ref, nhg_ref, nlcg_ref, nscg_ref = _take(refs, 4)
    ride_out = _take(refs, n_ride)
    hn_buf, xl_buf, z_buf, a_buf, b_buf, mix_buf, h_car = _take(refs, 7)
    xt_buf, xt_sem = _take(refs, 2) if x_bm else (None, None)
    xlg_buf, zg_buf = refs if guest else (None, None)

    R = tT * nB
    tTg, nBg = guest if guest else (0, 0)
    Rg = tTg * nBg
    Rt = R + Rg
    C = D_HALF
    j = pl.program_id(1)
    row = lambda ref: ref[layer:layer + 1, :]
    tap = lambda ref, k: ref[layer, k:k + 1, :]

    def conv(buf, w_ref, taps, n, rows, bias=None):
        acc = tap(w_ref, 0) * buf[0:rows]
        if bias is not None:
            acc = bias + acc
        for k in range(1, taps):
            acc = acc + tap(w_ref, k) * buf[k * n:k * n + rows]
        return acc

    @pl.when(j == 0)
    def _():
        if has_state:
            xl_buf[0:3 * nB] = stlc_ref[...].reshape(3 * nB, C)
            h_car[...] = sth_ref[...]
        else:
            xl_buf[0:3 * nB] = jnp.zeros((3 * nB, C), F32)
            h_car[...] = jnp.zeros((nB, C), F32)
        _load_rows(z_buf, stsc_ref, SC_CONV - 1, nB)

    if x_bm:
        slot = j % 2

        def fetch(tile, half):
            return [pltpu.make_async_copy(x_ref.at[b, pl.ds(tile * tT, tT), :],
                                          xt_buf.at[half, :, b, :], xt_sem.at[half, b])
                    for b in range(nB)]

        @pl.when(j == 0)
        def _():
            for copy in fetch(0, 0):
                copy.start()

        @pl.when(j + 1 < n_tiles)
        def _():
            for copy in fetch(j + 1, 1 - slot):
                copy.start()

        for copy in fetch(j, slot):
            copy.wait()
        x_tm = xt_buf.at[slot]
    else:
        x_tm = x_ref
    hn = _modulated_norm(x_tm[...], row(ng_ref), sc_ref[...], sh_ref[...])
    hn_buf[0:R] = hn.reshape(R, D_MODEL).astype(BF16)
    if guest:
        xg_tm = jnp.swapaxes(xg_ref[...], 0, 1) if x_bm else xg_ref[...]
        hn_g = _modulated_norm(xg_tm, row(ng_ref), scg_ref[...], shg_ref[...])
        hn_buf[R:Rt] = hn_g.reshape(Rg, D_MODEL).astype(BF16)

    in_proj = lambda k: _dot(hn_buf[...], win_ref[:, k * C:(k + 1) * C])

    xl = in_proj(0)
    xl_buf[3 * nB:3 * nB + R] = xl[0:R]
    b_buf[0:R] = conv(xl_buf, cw_ref, LRU_CONV, nB, R, row(cb_ref))
    if guest:
        xlg_buf[0:3 * nBg] = stlcg_ref[...].reshape(3 * nBg, C)
        xlg_buf[3 * nBg:3 * nBg + Rg] = xl[R:Rt]
        b_buf[R:Rt] = conv(xlg_buf, cw_ref, LRU_CONV, nBg, Rg, row(cb_ref))

    neg_c_sp = -RG_C * jnp.logaddexp(-row(lam_ref), 0.0)
    short = []
    for hf in range(C // MXU_COLS):
        lo, hi = hf * MXU_COLS, (hf + 1) * MXU_COLS
        xc = b_buf[:, lo:hi]
        gates = _dot(xc.astype(BF16), wg_ref[hf])
        r = jax.nn.sigmoid(gates[:, :MXU_COLS] + row(ba_ref)[:, lo:hi])
        ig = jax.nn.sigmoid(gates[:, MXU_COLS:] + row(bx_ref)[:, lo:hi])
        log_a = r * neg_c_sp[:, lo:hi]
        a = jnp.exp(log_a)
        mult = jnp.sqrt(-jnp.tanh(log_a) * (1.0 + a * a))
        a_buf[:, lo:hi] = a
        b_buf[:, lo:hi] = mult * (ig * xc)

        short.append(in_proj(3 + hf))
        _tie(b_buf, lo, short[hf], j < 0)

        def recur(h, first, steps, n):
            for t in range(steps):
                rows = pl.ds(first + t * n, n)
                h = a_buf[rows, lo:hi] * h + b_buf[rows, lo:hi]
                b_buf[rows, lo:hi] = h
            return h
        h_car[:, lo:hi] = recur(h_car[:, lo:hi], 0, tT, nB)
        if guest:
            nhg_ref[layer, :, lo:hi] = recur(sthg_ref[:, lo:hi], R, tTg, nBg)

    mix_buf[:, 0:C] = (b_buf[...] * jax.nn.gelu(in_proj(1))).astype(BF16)

    z = short[0] * short[1]
    bs = in_proj(2)
    z_buf[2 * nB:2 * nB + R] = z[0:R]
    mix_buf[0:R, C:2 * C] = (bs[0:R] * conv(z_buf, scw_ref, SC_CONV, nB, R)).astype(BF16)
    if guest:
        _load_rows(zg_buf, stscg_ref, SC_CONV - 1, nBg)
        zg_buf[2 * nBg:2 * nBg + Rg] = z[R:Rt]
        mix_buf[R:Rt, C:2 * C] = (bs[R:Rt] * conv(zg_buf, scw_ref, SC_CONV, nBg, Rg)).astype(BF16)

    mix = _dot(mix_buf[...], wout_ref[...])
    y_ref[...] = x_tm[...] + g_ref[...][None] * mix[0:R].reshape(tT, nB, D_MODEL)
    if guest:
        yg_ref[...] = xg_tm + gg_ref[...][None] * mix[R:Rt].reshape(tTg, nBg, D_MODEL)
        _keep_previous(prev_g, (nhg_ref, nlcg_ref, nscg_ref))
        nlcg_ref[layer] = xlg_buf[Rg:Rg + 3 * nBg].reshape(LRU_CONV - 1, nBg, C)
        _store_rows(nscg_ref, layer, zg_buf[Rg:Rg + 2 * nBg], SC_CONV - 1, nBg)

    lc_tail = xl_buf[R:R + 3 * nB]
    sc_tail = z_buf[R:R + 2 * nB]
    _keep_previous(prev, (nh_ref, nlc_ref, nsc_ref))
    nh_ref[layer] = h_car[...]
    nlc_ref[layer] = lc_tail.reshape(LRU_CONV - 1, nB, C)
    _store_rows(nsc_ref, layer, sc_tail, SC_CONV - 1, nB)
    xl_buf[0:3 * nB] = lc_tail
    z_buf[0:2 * nB] = sc_tail

    _cast_riders(ride_in, ride_out)


_MIX_SMALL = ("norm1_g", "lru_conv_w", "lru_conv_b", "lru_ba", "lru_bx", "lru_lambda", "sc_conv_w")


def _mix_call(l, x, mod, state, prev, small, w_gate, big, riders, *, tT, nB, x_bm, guest=None):
    if x_bm:
        B, T, _ = x.shape
        assert B == nB
        x_spec = pl.BlockSpec(memory_space=pl.ANY)
    else:
        T, B, _ = x.shape
        x_spec = pl.BlockSpec((tT, nB, D_MODEL), lambda i, j: (j, i, 0))
    R = tT * nB
    C = D_HALF
    n_steps = T // tT
    assert not riders or B == nB
    guest_specs, guest_args, guest_out_specs, guest_out_shapes, guest_dims = [], [], [], [], None
    Rg = 0
    if guest is not None:
        xg, modg, stateg, prevg = guest
        Bg, tTg = (xg.shape[0], xg.shape[1]) if x_bm else (xg.shape[1], xg.shape[0])
        assert B == nB and Bg % n_steps == 0
        nBg = Bg // n_steps
        guest_dims, Rg = (tTg, nBg), tTg * nBg
        gx = (pl.BlockSpec((nBg, tTg, D_MODEL), lambda i, j: (j, 0, 0)) if x_bm else
              pl.BlockSpec((tTg, nBg, D_MODEL), lambda i, j: (0, j, 0)))
        gmod = lambda k: pl.BlockSpec((None, nBg, D_MODEL), lambda i, j: (l, j, k))
        gh = lambda n, ll: pl.BlockSpec((n, nBg, C), lambda i, j: (ll, j, 0))
        glc = lambda n, ll: pl.BlockSpec((n, LRU_CONV - 1, nBg, C), lambda i, j: (ll, 0, j, 0))
        gsc = lambda n, ll: pl.BlockSpec((n, nBg, SC_CONV - 1, C), lambda i, j: (ll, j, 0, 0))
        guest_specs = [gx, gmod(0), gmod(1), gmod(2), gh(None, l), glc(None, l), gsc(None, l),
                       *([gh(l, 0), glc(l, 0), gsc(l, 0)] if l else [])]
        guest_args = [xg, modg, modg, modg, *stateg, *prevg]
        guest_out_specs = [pl.BlockSpec((tTg, nBg, D_MODEL), lambda i, j: (0, j, 0)),
                           gh(l + 1, 0), glc(l + 1, 0), gsc(l + 1, 0)]
        guest_out_shapes = [jax.ShapeDtypeStruct((tTg, Bg, D_MODEL), F32),
                            jax.ShapeDtypeStruct((l + 1, Bg, C), F32),
                            jax.ShapeDtypeStruct((l + 1, LRU_CONV - 1, Bg, C), F32),
                            jax.ShapeDtypeStruct((l + 1, Bg, SC_CONV - 1, C), F32)]
    whole = lambda i, j: (0, 0)
    mod_spec = lambda k: pl.BlockSpec((None, nB, D_MODEL), lambda i, j: (l, i, k))
    once = pl.Buffered(1)
    h_spec = lambda n: pl.BlockSpec((n, nB, C), lambda i, j: (0, i, 0))
    lc_spec = lambda n: pl.BlockSpec((n, LRU_CONV - 1, nB, C), lambda i, j: (0, 0, i, 0))
    sc_spec = lambda n: pl.BlockSpec((n, nB, SC_CONV - 1, C), lambda i, j: (0, i, 0, 0))
    state_specs, state_args = [], []
    if state is not None:
        state_specs = [
            pl.BlockSpec((None, nB, C), lambda i, j: (l, i, 0)),
            pl.BlockSpec((None, LRU_CONV - 1, nB, C), lambda i, j: (l, 0, i, 0)),
            pl.BlockSpec((None, nB, SC_CONV - 1, C), lambda i, j: (l, i, 0, 0)),
        ]
        state_args = list(state)
    prev_specs = [h_spec(l), lc_spec(l), sc_spec(l)] if l else []
    r_in, r_out, r_shapes = _rider_specs(riders, n_steps, lambda i, j: j)
    scratch = [
        pltpu.VMEM((R + Rg, D_MODEL), BF16),
        pltpu.VMEM((R + 3 * nB, C), F32),
        pltpu.VMEM((R + 2 * nB, C), F32),
        pltpu.VMEM((R + Rg, C), F32),
        pltpu.VMEM((R + Rg, C), F32),
        pltpu.VMEM((R + Rg, D_MODEL), BF16),
        pltpu.VMEM((nB, C), F32),
    ]
    if x_bm:
        scratch += [pltpu.VMEM((2, tT, nB, D_MODEL), F32),
                    pltpu.SemaphoreType.DMA((2, nB))]
    if guest is not None:
        scratch += [pltpu.VMEM((Rg + (LRU_CONV - 1) * guest_dims[1], C), F32),
                    pltpu.VMEM((Rg + (SC_CONV - 1) * guest_dims[1], C), F32)]
    outs = pl.pallas_call(
        functools.partial(_mix_kernel, layer=l, tT=tT, nB=nB, n_tiles=n_steps, x_bm=x_bm,
                          has_state=state is not None, guest=guest_dims, n_ride=len(riders)),
        grid=(B // nB, n_steps),
        in_specs=[
            x_spec, mod_spec(0), mod_spec(1), mod_spec(2),
            *[_whole_spec(small[name]) for name in _MIX_SMALL],
            *state_specs, *prev_specs, *guest_specs,
            pl.BlockSpec((D_MODEL, 5 * C), whole, pipeline_mode=once),
            pl.BlockSpec((None, C // MXU_COLS, MXU_COLS, 2 * MXU_COLS),
                         lambda i, j: (l, 0, 0, 0), pipeline_mode=once),
            pl.BlockSpec((D_MODEL, D_MODEL), whole, pipeline_mode=once),
            *r_in,
        ],
        out_specs=[
            pl.BlockSpec((tT, nB, D_MODEL), lambda i, j: (j, i, 0)),
            h_spec(l + 1), lc_spec(l + 1), sc_spec(l + 1),
            *guest_out_specs, *r_out,
        ],
        out_shape=[
            jax.ShapeDtypeStruct((T, B, D_MODEL), F32),
            jax.ShapeDtypeStruct((l + 1, B, C), F32),
            jax.ShapeDtypeStruct((l + 1, LRU_CONV - 1, B, C), F32),
            jax.ShapeDtypeStruct((l + 1, B, SC_CONV - 1, C), F32),
            *guest_out_shapes, *r_shapes,
        ],
        scratch_shapes=scratch,
        compiler_params=pltpu.CompilerParams(
            dimension_semantics=("arbitrary", "arbitrary"),
            vmem_limit_bytes=VMEM_LIMIT_BYTES),
        name="mix",
    )(x, mod, mod, mod, *[small[name] for name in _MIX_SMALL], *state_args, *prev, *guest_args,
      big["w_in"], w_gate, big["w_out"], *[src for src, _ in riders])
    n_out = 4 + len(guest_out_specs)
    return outs[:n_out], outs[n_out:]


def _ffn_kernel(*refs, layer, tT, nB, n_tiles, final, has_state, guest, n_ride):
    refs = list(refs)
    x_ref, sh_ref, sc_ref, g_ref, ng_ref, cw_ref, fg_ref = _take(refs, 7)
    stfc_ref = _take(refs, 1)[0] if has_state else None
    prev = _take(refs, 1) if layer else []
    if guest:
        xg_ref, shg_ref, scg_ref, gg_ref, stg_ref = _take(refs, 5)
        prev_g = _take(refs, 1) if layer else []
    wup_ref, wdn_ref = _take(refs, 2)
    ride_in = _take(refs, n_ride)
    y_ref, nfc_ref = _take(refs, 2)
    if guest:
        yg_ref, nfcg_ref = _take(refs, 2)
    ride_out = _take(refs, n_ride)
    hn_buf, u_buf, u_car, act_buf = _take(refs, 4)
    yt_buf, yt_sem = _take(refs, 2) if final else (None, None)

    R = tT * nB
    tTg, nBg = guest if guest else (0, 0)
    Rg = tTg * nBg
    j = pl.program_id(1)
    gain = ng_ref[layer:layer + 1, :]

    @pl.when(j == 0)
    def _():
        _load_rows(u_car, stfc_ref, FFN_CONV - 1, nB)

    hn = _modulated_norm(x_ref[...], gain, sc_ref[...], sh_ref[...])
    hn_buf[0:R] = hn.reshape(R, D_MODEL).astype(BF16)
    if guest:
        ug_buf = refs[0]
        hn_g = _modulated_norm(xg_ref[...], gain, scg_ref[...], shg_ref[...])
        hn_buf[R:R + Rg] = hn_g.reshape(Rg, D_MODEL).astype(BF16)

    def conv_act(ub, n, rows, v, tap):
        uc = tap(0) * ub[0:rows]
        for k in range(1, FFN_CONV):
            uc = uc + tap(k) * ub[k * n:k * n + rows]
        return (jax.nn.gelu(uc) * v).astype(BF16)

    for ci, c0 in enumerate(range(0, D_FF, MXU_COLS)):
        cols = slice(c0, c0 + MXU_COLS)
        tap = lambda k: cw_ref[layer, k:k + 1, cols]
        u = _dot(hn_buf[...], wup_ref[:, cols])
        v = _dot(hn_buf[...], wup_ref[:, D_FF + c0:D_FF + c0 + MXU_COLS])
        ub = u_buf.at[ci % 2]
        ub[0:2 * nB] = u_car[:, cols]
        ub[2 * nB:2 * nB + R] = u[0:R]
        act_buf[0:R, cols] = conv_act(ub, nB, R, v[0:R], tap)
        u_car[:, cols] = ub[R:R + 2 * nB]
        if guest:
            ubg = ug_buf.at[ci % 2]
            for k in range(FFN_CONV - 1):
                ubg[k * nBg:(k + 1) * nBg] = stg_ref[:, k, cols]
            ubg[2 * nBg:2 * nBg + Rg] = u[R:R + Rg]
            act_buf[R:R + Rg, cols] = conv_act(ubg, nBg, Rg, v[R:R + Rg], tap)
            for k in range(FFN_CONV - 1):
                nfcg_ref[layer, :, k, cols] = ubg[Rg + k * nBg:Rg + (k + 1) * nBg]

    out = _dot(act_buf[...], wdn_ref[...])

    def residual(x_t, g_t, out_t, t_n, b_n):
        xn = x_t + g_t[None] * out_t.reshape(t_n, b_n, D_MODEL)
        if final:
            ms = jnp.mean(xn * xn, axis=-1, keepdims=True)
            xn = xn * jax.lax.rsqrt(ms + EPS) * fg_ref[...][None]
        return xn

    y_host = residual(x_ref[...], g_ref[...], out[0:R], tT, nB)
    if final:
        slot = j % 2

        def send(tile, half):
            return [pltpu.make_async_copy(yt_buf.at[half, :, b, :],
                                          y_ref.at[b, pl.ds(tile * tT, tT), :], yt_sem.at[half, b])
                    for b in range(nB)]

        @pl.when(j >= 2)
        def _():
            for copy in send(j - 2, slot):
                copy.wait()

        yt_buf[slot] = y_host
        for copy in send(j, slot):
            copy.start()

        @pl.when(j == n_tiles - 1)
        def _():
            for tile, half in ((j - 1, 1 - slot), (j, slot))[0 if n_tiles > 1 else 1:]:
                for copy in send(tile, half):
                    copy.wait()
    else:
        y_ref[...] = y_host
    if guest:
        y_guest = residual(xg_ref[...], gg_ref[...], out[R:R + Rg], tTg, nBg)
        yg_ref[...] = jnp.swapaxes(y_guest, 0, 1) if final else y_guest
        _keep_previous(prev_g, (nfcg_ref,))

    _keep_previous(prev, (nfc_ref,))
    _store_rows(nfc_ref, layer, u_car[...], FFN_CONV - 1, nB)
    _cast_riders(ride_in, ride_out)


def _ffn_call(l, x, mod, st_fc, prev, small, big, riders, *, tT, nB, final, guest=None):
    T, B, _ = x.shape
    R = tT * nB
    n_steps = T // tT
    assert not riders or B == nB
    guest_specs, guest_args, guest_out_specs, guest_out_shapes, guest_dims = [], [], [], [], None
    Rg = 0
    if guest is not None:
        xg, modg, stg, prevg = guest
        tTg, Bg, _ = xg.shape
        assert B == nB and Bg % n_steps == 0
        nBg = Bg // n_steps
        guest_dims, Rg = (tTg, nBg), tTg * nBg
        gmod = lambda k: pl.BlockSpec((None, nBg, D_MODEL), lambda i, j: (l, j, k))
        gfc = lambda n: pl.BlockSpec((n, nBg, FFN_CONV - 1, D_FF), lambda i, j: (0, j, 0, 0))
        guest_specs = [
            pl.BlockSpec((tTg, nBg, D_MODEL), lambda i, j: (0, j, 0)), gmod(3), gmod(4), gmod(5),
            pl.BlockSpec((None, nBg, FFN_CONV - 1, D_FF), lambda i, j: (l, j, 0, 0)),
            *([gfc(l)] if l else [])]
        guest_args = [xg, modg, modg, modg, stg, *prevg]
        if final:
            guest_out_specs = [pl.BlockSpec((nBg, tTg, D_MODEL), lambda i, j: (j, 0, 0))]
            guest_out_shapes = [jax.ShapeDtypeStruct((Bg, tTg, D_MODEL), F32)]
        else:
            guest_out_specs = [pl.BlockSpec((tTg, nBg, D_MODEL), lambda i, j: (0, j, 0))]
            guest_out_shapes = [jax.ShapeDtypeStruct((tTg, Bg, D_MODEL), F32)]
        guest_out_specs.append(gfc(l + 1))
        guest_out_shapes.append(jax.ShapeDtypeStruct((l + 1, Bg, FFN_CONV - 1, D_FF), F32))
    whole = lambda i, j: (0, 0)
    mod_spec = lambda k: pl.BlockSpec((None, nB, D_MODEL), lambda i, j: (l, i, k))
    once = pl.Buffered(1)
    fc_spec = lambda n: pl.BlockSpec((n, nB, FFN_CONV - 1, D_FF), lambda i, j: (0, i, 0, 0))
    state_specs, state_args = [], []
    if st_fc is not None:
        state_specs = [pl.BlockSpec((None, nB, FFN_CONV - 1, D_FF), lambda i, j: (l, i, 0, 0))]
        state_args = [st_fc]
    prev_specs = [fc_spec(l)] if l else []
    if final:
        assert B == nB
        y_spec = pl.BlockSpec(memory_space=pl.ANY)
        y_shape = jax.ShapeDtypeStruct((B, T, D_MODEL), F32)
    else:
        y_spec = pl.BlockSpec((tT, nB, D_MODEL), lambda i, j: (j, i, 0))
        y_shape = jax.ShapeDtypeStruct((T, B, D_MODEL), F32)
    r_in, r_out, r_shapes = _rider_specs(riders, n_steps, lambda i, j: j)
    outs = pl.pallas_call(
        functools.partial(_ffn_kernel, layer=l, tT=tT, nB=nB, n_tiles=n_steps, final=final,
                          has_state=st_fc is not None, guest=guest_dims, n_ride=len(riders)),
        grid=(B // nB, n_steps),
        in_specs=[
            pl.BlockSpec((tT, nB, D_MODEL), lambda i, j: (j, i, 0)),
            mod_spec(3), mod_spec(4), mod_spec(5),
            _whole_spec(small["norm2_g"]), _whole_spec(small["ffn_conv_w"]),
            _whole_spec(small["final_g"]),
            *state_specs, *prev_specs, *guest_specs,
            pl.BlockSpec((D_MODEL, 2 * D_FF), whole, pipeline_mode=once),
            pl.BlockSpec((D_FF, D_MODEL), whole, pipeline_mode=once),
            *r_in,
        ],
        out_specs=[y_spec, fc_spec(l + 1), *guest_out_specs, *r_out],
        out_shape=[
            y_shape,
            jax.ShapeDtypeStruct((l + 1, B, FFN_CONV - 1, D_FF), F32),
            *guest_out_shapes,
            *r_shapes,
        ],
        scratch_shapes=[
            pltpu.VMEM((R + Rg, D_MODEL), BF16),
            pltpu.VMEM((2, R + 2 * nB, MXU_COLS), F32),
            pltpu.VMEM((2 * nB, D_FF), F32),
            pltpu.VMEM((R + Rg, D_FF), BF16),
            *([pltpu.VMEM((2, tT, nB, D_MODEL), F32),
               pltpu.SemaphoreType.DMA((2, nB))] if final else []),
            *([pltpu.VMEM((2, Rg + 2 * guest_dims[1], MXU_COLS), F32)] if guest else []),
        ],
        compiler_params=pltpu.CompilerParams(
            dimension_semantics=("arbitrary", "arbitrary"),
            vmem_limit_bytes=VMEM_LIMIT_BYTES),
        name="ffn",
    )(x, mod, mod, mod, small["norm2_g"], small["ffn_conv_w"], small["final_g"],
      *state_args, *prev, *guest_args, big["w_up"], big["w_dn"], *[src for src, _ in riders])
    n_out = 2 + len(guest_out_specs)
    return outs[:n_out], outs[n_out:]


def _gate_weights(lru_wa, lru_wx):
    eye = jnp.eye(LRU_HEADS, dtype=lru_wa.dtype)
    dense = lambda w: jnp.einsum("lhij,hk->lhikj", w, eye).reshape(DEPTH, D_HALF, D_HALF)
    wa, wx = dense(lru_wa), dense(lru_wx)
    ws = []
    for hf in range(D_HALF // MXU_COLS):
        s = slice(hf * MXU_COLS, (hf + 1) * MXU_COLS)
        ws.append(jnp.concatenate([wa[:, s, s], wx[:, s, s]], axis=-1))
    return jnp.stack(ws, axis=1).astype(BF16)


def kernel(x_prompt, x_sample, c_prompt, c_sample, state_lru_h, state_lru_conv, state_sc_conv, state_ffn_conv, w_ada, b_ada, norm1_g, norm2_g, w_in, lru_conv_w, lru_conv_b, lru_wa, lru_ba, lru_wx, lru_bx, lru_lambda, sc_conv_w, w_out, ffn_w_up, ffn_conv_w, ffn_w_down, final_g):
    small = dict(
        norm1_g=norm1_g, norm2_g=norm2_g, final_g=final_g[None, :], lru_conv_w=lru_conv_w,
        lru_conv_b=lru_conv_b, lru_ba=lru_ba, lru_bx=lru_bx, lru_lambda=lru_lambda,
        sc_conv_w=sc_conv_w, ffn_conv_w=ffn_conv_w)
    w_gate = _gate_weights(lru_wa, lru_wx)

    (mod_p, mod_s), cast = _ada_call(c_prompt, c_sample, w_ada, b_ada,
                                     [(w_in, 0), (w_out, 0)])
    w_bf = dict(w_in=cast[0], w_out=cast[1])

    tile_p = dict(tT=HOST_TILE_ROWS // x_prompt.shape[0], nB=x_prompt.shape[0])
    x_p, x_s = x_prompt, x_sample
    mix_state_s = (state_lru_h, jnp.swapaxes(state_lru_conv, 1, 2), state_sc_conv)
    mix_p, mix_s, fc_p, fc_s = [], [], [], []
    for l in range(DEPTH):
        (x_p, *mix_p, x_s, nh_s, nlc_s, nsc_s), cast = _mix_call(
            l, x_p, mod_p, None, mix_p, small, w_gate, w_bf, [(ffn_w_up, l), (ffn_w_down, l)],
            x_bm=(l == 0), guest=(x_s, mod_s, mix_state_s, mix_s), **tile_p)
        mix_s = [nh_s, nlc_s, nsc_s]
        w_bf = dict(w_up=cast[0], w_dn=cast[1])
        (x_p, nfc_p, x_s, nfc_s), cast = _ffn_call(
            l, x_p, mod_p, None, fc_p, small, w_bf,
            [(w_in, l + 1), (w_out, l + 1)] if l + 1 < DEPTH else [],
            final=(l == DEPTH - 1), guest=(x_s, mod_s, state_ffn_conv, fc_s), **tile_p)
        fc_p, fc_s = [nfc_p], [nfc_s]
        if cast:
            w_bf = dict(w_in=cast[0], w_out=cast[1])
    states = [(nh, jnp.swapaxes(nlc, 1, 2), nsc, nfc)
              for (nh, nlc, nsc), (nfc,) in ((mix_p, fc_p), (mix_s, fc_s))]
    return (x_p, x_s) + states[0] + states[1]
```
